```python
import math
import jax, jax.numpy as jnp
from jax import lax
import numpy as np

D_MODEL = 1024
BATCH = 4
SEQ = 4096
DEPTH = 2

MEM_LEN = 256
DEEPNORM_ALPHA = (2 * DEPTH) ** 0.25
DEEPNORM_BETA = (8 * DEPTH) ** -0.25
LN_EPS = 1e-5
RMS_EPS = 1e-6
ROPE_THETA = 10000.0

GLA_HEADS = 4
GLA_KEY_DIM = D_MODEL // 2
GLA_VAL_DIM = D_MODEL
GLA_GATE_RANK = 16
GLA_GATE_NORMALIZER = 16.0
GLA_CHUNK = 64

DIL_PATTERNS = ((128, 1), (512, 4), (2048, 16))
DIL_HEADS = 4
DIL_HEAD_DIM = 128
DIL_WIDTH = DIL_HEADS * DIL_HEAD_DIM
DIL_Q_WIDTH = len(DIL_PATTERNS) * DIL_WIDTH
DIL_BLOCK = 128

MEM_HEADS = 4
MEM_HEAD_DIM = 128
MEM_WIDTH = MEM_HEADS * MEM_HEAD_DIM

SSD_INNER = 3 * D_MODEL // 2
SSD_HEAD_DIM = 64
SSD_HEADS = SSD_INNER // SSD_HEAD_DIM
SSD_GROUPS = 4
SSD_STATE = 128
SSD_CONV = 4
SSD_CHUNK = 128
SSD_CONV_DIM = SSD_INNER + 2 * SSD_GROUPS * SSD_STATE

L0_SPLITS = (GLA_KEY_DIM, GLA_KEY_DIM, GLA_VAL_DIM, GLA_VAL_DIM, GLA_GATE_RANK,
             DIL_Q_WIDTH, DIL_WIDTH, DIL_WIDTH, DIL_WIDTH, MEM_WIDTH, MEM_WIDTH)
L0_IN = sum(L0_SPLITS)
L0_OUT = GLA_VAL_DIM + DIL_WIDTH + MEM_WIDTH
L1_SPLITS = (SSD_INNER, SSD_CONV_DIM, SSD_HEADS, MEM_WIDTH, MEM_WIDTH)
L1_IN = sum(L1_SPLITS)
L1_OUT = SSD_INNER + MEM_WIDTH

kernel_name = "hybrid_gla_dilated_ssd_deepnorm"


def _split(h, sizes):
    return jnp.split(h, [int(i) for i in np.cumsum(sizes)[:-1]], axis=-1)


def _layernorm(x, w, b):
    xf = x.astype(jnp.float32)
    mu = jnp.mean(xf, -1, keepdims=True)
    var = jnp.mean(jnp.square(xf - mu), -1, keepdims=True)
    return ((xf - mu) * lax.rsqrt(var + LN_EPS) * w.astype(jnp.float32)
            + b.astype(jnp.float32)).astype(x.dtype)


def _rmsnorm(x, w):
    xf = x.astype(jnp.float32)
    return xf * lax.rsqrt(jnp.mean(xf * xf, -1, keepdims=True) + RMS_EPS) * w.astype(jnp.float32)


def _rope_tables(positions, dim):
    inv = 1.0 / (ROPE_THETA ** (jnp.arange(0, dim, 2, dtype=jnp.float32) / dim))
    ang = positions.astype(jnp.float32)[..., None] * inv
    return jnp.cos(ang), jnp.sin(ang)


def _rope(x, cos, sin):
    x1, x2 = jnp.split(x.astype(jnp.float32), 2, axis=-1)
    c, s = cos[:, :, None, :], sin[:, :, None, :]
    return jnp.concatenate([x1 * c - x2 * s, x2 * c + x1 * s], axis=-1)


def _gla(q, k, v, log_a):
    b_, t_, h_, dk = q.shape
    dv = v.shape[-1]
    n = t_ // GLA_CHUNK

    def chunks(z):
        return z.reshape(b_, n, GLA_CHUNK, h_, z.shape[-1]).transpose(1, 0, 3, 2, 4).astype(jnp.float32)

    qc = chunks(q) * dk ** -0.5
    kc, vc, gc = chunks(k), chunks(v), chunks(log_a)
    bcum = jnp.cumsum(gc, axis=3)
    blast = bcum[:, :, :, -1:, :]
    q_dec = qc * jnp.exp(bcum)
    k_inv = kc * jnp.exp(-bcum)
    k_tail = kc * jnp.exp(blast - bcum)
    causal = jnp.tril(jnp.ones((GLA_CHUNK, GLA_CHUNK), dtype=bool))
    scores = jnp.where(causal, jnp.einsum('nbhtk,nbhsk->nbhts', q_dec, k_inv), 0.0)
    o_intra = jnp.einsum('nbhts,nbhsv->nbhtv', scores, vc)

    def step(state, inp):
        qd, kt, vv, bl = inp
        o = jnp.einsum('bhtk,bhkv->bhtv', qd, state)
        state = state * jnp.exp(bl[:, :, 0, :])[..., None] + jnp.einsum('bhsk,bhsv->bhkv', kt, vv)
        return state, o

    s0 = jnp.zeros((b_, h_, dk, dv), jnp.float32)
    _, o_inter = lax.scan(step, s0, (q_dec, k_tail, vc, blast))
    o = o_intra + o_inter
    return o.transpose(1, 0, 3, 2, 4).reshape(b_, t_, h_, dv)


def _dilated_group(q, k, v, window, dilation):
    b_, t_, h_, dh = q.shape
    span = window // dilation
    sub_len = t_ // dilation
    nb = -(-sub_len // DIL_BLOCK)
    sub_pad = nb * DIL_BLOCK

    def gather(z):
        z = z.astype(jnp.float32).reshape(b_, sub_len, dilation, h_, dh).transpose(0, 2, 3, 1, 4)
        z = jnp.pad(z, ((0, 0), (0, 0), (0, 0), (0, sub_pad - sub_len), (0, 0)))
        return z.reshape(b_, dilation, h_, nb, DIL_BLOCK, dh)

    def with_prev(z):
        prev = jnp.pad(z, ((0, 0), (0, 0), (0, 0), (1, 0), (0, 0), (0, 0)))[:, :, :, :-1]
        return jnp.concatenate([prev, z], axis=4)

    qg = gather(q)
    kb, vb = with_prev(gather(k)), with_prev(gather(v))
    s = jnp.einsum('brhnqd,brhnkd->brhnqk', qg, kb) * dh ** -0.5
    qi = jnp.arange(DIL_BLOCK)[:, None] + DIL_BLOCK
    ki = jnp.arange(2 * DIL_BLOCK)[None, :]
    dist = qi - ki
    blk = jnp.arange(nb)[:, None, None]
    valid = (dist >= 0) & (dist <= span) & ((blk > 0) | (ki >= DIL_BLOCK))
    s = jnp.where(valid, s, -jnp.inf)
    m = jnp.max(s, -1, keepdims=True)
    p = jnp.exp(s - m)
    den = jnp.sum(p, -1, keepdims=True)
    o = jnp.einsum('brhnqk,brhnkd->brhnqd', p, vb) / den
    lse = (m + jnp.log(den))[..., 0]
    o = o.reshape(b_, dilation, h_, sub_pad, dh)[:, :, :, :sub_len]
    o = o.transpose(0, 3, 1, 2, 4).reshape(b_, t_, h_, dh)
    lse = lse.reshape(b_, dilation, h_, sub_pad)[..., :sub_len]
    lse = lse.transpose(0, 3, 1, 2).reshape(b_, t_, h_)
    return o, lse


def _dilated_attention(q_all, k, v):
    outs, lses = [], []
    for g, (window, dilation) in enumerate(DIL_PATTERNS):
        o, lse = _dilated_group(q_all[:, :, g * DIL_HEADS:(g + 1) * DIL_HEADS], k, v, window, dilation)
        outs.append(o)
        lses.append(lse)
    wts = jax.nn.softmax(jnp.stack(lses, 0), axis=0)
    return jnp.sum(wts[..., None] * jnp.stack(outs, 0), axis=0)


def _memory_branch(mq, mg, mem, wk, wv):
    b_, t_, _ = mq.shape
    q = mq.reshape(b_, t_, MEM_HEADS, MEM_HEAD_DIM)
    k = (mem @ wk).reshape(b_, -1, MEM_HEADS, MEM_HEAD_DIM)
    v = (mem @ wv).reshape(b_, -1, MEM_HEADS, MEM_HEAD_DIM)
    s = jnp.einsum('bthd,bmhd->bhtm', q, k).astype(jnp.float32) * MEM_HEAD_DIM ** -0.5
    p = jax.nn.softmax(s, axis=-1)
    o = jnp.einsum('bhtm,bmhd->bthd', p, v.astype(jnp.float32))
    return o.reshape(b_, t_, MEM_WIDTH) * jax.nn.silu(mg.astype(jnp.float32))


def _causal_depthwise_conv(u, w, b):
    ch = u.shape[-1]
    out = lax.conv_general_dilated(u, w[:, None, :], window_strides=(1,),
                                   padding=((SSD_CONV - 1, 0),),
                                   dimension_numbers=('NWC', 'WIO', 'NWC'),
                                   feature_group_count=ch)
    return out + b


def _ssd(x, dt, a, bm, cm):
    b_, t_, h_, p_ = x.shape
    g_, n_ = bm.shape[-2:]
    hg = h_ // g_
    nc = t_ // SSD_CHUNK
    q = SSD_CHUNK
    xf = x.astype(jnp.float32)
    xc = (xf * dt[..., None]).reshape(b_, nc, q, g_, hg, p_)
    adt = (dt * a).reshape(b_, nc, q, g_, hg).transpose(0, 3, 4, 1, 2)
    bc = bm.astype(jnp.float32).reshape(b_, nc, q, g_, n_)
    cc = cm.astype(jnp.float32).reshape(b_, nc, q, g_, n_)
    acum = jnp.cumsum(adt, axis=-1)
    causal = jnp.tril(jnp.ones((q, q), dtype=bool))
    lmat = jnp.exp(jnp.where(causal, acum[..., :, None] - acum[..., None, :], -jnp.inf))
    cb = jnp.einsum('bclgn,bcsgn->bgcls', cc, bc)
    y_diag = jnp.einsum('bgcls,bghcls,bcsghp->bclghp', cb, lmat, xc)
    decay_to_end = jnp.exp(acum[..., -1:] - acum)
    chunk_states = jnp.einsum('bcsgn,bghcs,bcsghp->cbghpn', bc, decay_to_end, xc)
    chunk_decay = jnp.exp(acum[..., -1]).transpose(3, 0, 1, 2)

    def step(state, inp):
        st, dec = inp
        return state * dec[..., None, None] + st, state

    s0 = jnp.zeros((b_, g_, hg, p_, n_), jnp.float32)
    _, prev = lax.scan(step, s0, (chunk_states, chunk_decay))
    y_off = jnp.einsum('bclgn,cbghpn,bghcl->bclghp', cc, prev, jnp.exp(acum))
    return (y_diag + y_off).reshape(b_, t_, h_, p_)


def _gla_dilated_layer(x, mem, cos, sin, w_in, gla_w_gate, gla_b_gate, gla_norm_w,
                       mem_wk, mem_wv, w_out, ln_w, ln_b):
    b_, t_, _ = x.shape
    h = x @ w_in
    gq, gk, gv, gg, ga, dq, dk, dv, dg, mq, mg = _split(h, L0_SPLITS)
    log_a = jax.nn.log_sigmoid((ga @ gla_w_gate + gla_b_gate).astype(jnp.float32)) / GLA_GATE_NORMALIZER
    o_a = _gla(gq.reshape(b_, t_, GLA_HEADS, -1), gk.reshape(b_, t_, GLA_HEADS, -1),
               gv.reshape(b_, t_, GLA_HEADS, -1), log_a.reshape(b_, t_, GLA_HEADS, -1))
    o_a = _rmsnorm(o_a, gla_norm_w).reshape(b_, t_, GLA_VAL_DIM) * jax.nn.silu(gg.astype(jnp.float32))
    q = _rope(dq.reshape(b_, t_, len(DIL_PATTERNS) * DIL_HEADS, DIL_HEAD_DIM), cos, sin)
    k = _rope(dk.reshape(b_, t_, DIL_HEADS, DIL_HEAD_DIM), cos, sin)
    v = dv.reshape(b_, t_, DIL_HEADS, DIL_HEAD_DIM)
    o_b = _dilated_attention(q, k, v).reshape(b_, t_, DIL_WIDTH) * jax.nn.silu(dg.astype(jnp.float32))
    o_m = _memory_branch(mq, mg, mem, mem_wk, mem_wv)
    out = jnp.concatenate([o_a, o_b, o_m], axis=-1).astype(x.dtype) @ w_out
    return _layernorm(DEEPNORM_ALPHA * x + out, ln_w, ln_b)


def _ssd_layer(x, mem, w_in, conv_w, conv_b, dt_bias, a_log, d_skip, ssd_norm_w,
               mem_wk, mem_wv, w_out, ln_w, ln_b):
    b_, t_, _ = x.shape
    h = x @ w_in
    z, xbc, dt_raw, mq, mg = _split(h, L1_SPLITS)
    xbc = jax.nn.silu(_causal_depthwise_conv(xbc, conv_w, conv_b))
    xs, bm, cm = _split(xbc, (SSD_INNER, SSD_GROUPS * SSD_STATE, SSD_GROUPS * SSD_STATE))
    dt = jax.nn.softplus(dt_raw.astype(jnp.float32) + dt_bias.astype(jnp.float32))
    a = -jnp.exp(a_log.astype(jnp.float32))
    xs = xs.reshape(b_, t_, SSD_HEADS, SSD_HEAD_DIM)
    y = _ssd(xs, dt, a, bm.reshape(b_, t_, SSD_GROUPS, SSD_STATE), cm.reshape(b_, t_, SSD_GROUPS, SSD_STATE))
    y = y + d_skip.astype(jnp.float32)[:, None] * xs.astype(jnp.float32)
    y = y.reshape(b_, t_, SSD_INNER) * jax.nn.silu(z.astype(jnp.float32))
    y = _rmsnorm(y.reshape(b_, t_, SSD_GROUPS, -1), ssd_norm_w.reshape(SSD_GROUPS, -1)).reshape(b_, t_, SSD_INNER)
    o_m = _memory_branch(mq, mg, mem, mem_wk, mem_wv)
    out = jnp.concatenate([y, o_m], axis=-1).astype(x.dtype) @ w_out
    return _layernorm(DEEPNORM_ALPHA * x + out, ln_w, ln_b)


def setup_inputs(seed: int = 0) -> dict:
    key = jax.random.key(seed)
    ks = iter(jax.random.split(key, 40))
    f32 = jnp.float32

    def nrm(shape, fan_in, scale=1.0):
        return jax.random.normal(next(ks), shape, f32) * (scale * fan_in ** -0.5)

    def near_one(shape):
        return 1.0 + 0.02 * jax.random.normal(next(ks), shape, f32)

    def small(shape, s=0.02):
        return s * jax.random.normal(next(ks), shape, f32)

    x = jax.random.normal(next(ks), (BATCH, SEQ, D_MODEL), f32)
    mem = jax.random.normal(next(ks), (BATCH, MEM_LEN, D_MODEL), f32)
    offs = jax.random.randint(next(ks), (BATCH, 1), 0, 1024, dtype=jnp.int32)
    positions = (jnp.arange(SEQ, dtype=jnp.int32)[None, :] + offs).astype(jnp.int32)

    dt0 = jnp.exp(jax.random.uniform(next(ks), (SSD_HEADS,), f32, math.log(1e-3), math.log(1e-1)))
    dt_bias = dt0 + jnp.log(-jnp.expm1(-dt0))
    a_log = jnp.log(jax.random.uniform(next(ks), (SSD_HEADS,), f32, 1.0, 16.0))

    return {
        "x": x,
        "mem": mem,
        "positions": positions,
        "l0_w_in": nrm((D_MODEL, L0_IN), D_MODEL),
        "l0_gla_w_gate": nrm((GLA_GATE_RANK, GLA_KEY_DIM), GLA_GATE_RANK),
        "l0_gla_b_gate": small((GLA_KEY_DIM,), 0.1),
        "l0_gla_norm_w": near_one((GLA_VAL_DIM // GLA_HEADS,)),
        "l0_mem_wk": nrm((D_MODEL, MEM_WIDTH), D_MODEL),
        "l0_mem_wv": nrm((D_MODEL, MEM_WIDTH), D_MODEL),
        "l0_w_out": nrm((L0_OUT, D_MODEL), L0_OUT, DEEPNORM_BETA),
        "l0_ln_w": near_one((D_MODEL,)),
        "l0_ln_b": small((D_MODEL,)),
        "l1_w_in": nrm((D_MODEL, L1_IN), D_MODEL),
        "l1_conv_w": nrm((SSD_CONV, SSD_CONV_DIM), SSD_CONV),
        "l1_conv_b": small((SSD_CONV_DIM,)),
        "l1_dt_bias": dt_bias,
        "l1_a_log": a_log,
        "l1_d_skip": near_one((SSD_HEADS,)),
        "l1_ssd_norm_w": near_one((SSD_INNER,)),
        "l1_mem_wk": nrm((D_MODEL, MEM_WIDTH), D_MODEL),
        "l1_mem_wv": nrm((D_MODEL, MEM_WIDTH), D_MODEL),
        "l1_w_out": nrm((L1_OUT, D_MODEL), L1_OUT, DEEPNORM_BETA),
        "l1_ln_w": near_one((D_MODEL,)),
        "l1_ln_b": small((D_MODEL,)),
    }


def reference(x, mem, positions,
              l0_w_in, l0_gla_w_gate, l0_gla_b_gate, l0_gla_norm_w, l0_mem_wk, l0_mem_wv,
              l0_w_out, l0_ln_w, l0_ln_b,
              l1_w_in, l1_conv_w, l1_conv_b, l1_dt_bias, l1_a_log, l1_d_skip, l1_ssd_norm_w,
              l1_mem_wk, l1_mem_wv, l1_w_out, l1_ln_w, l1_ln_b):
    cos, sin = _rope_tables(positions, DIL_HEAD_DIM)
    layer_params = (
        (l0_w_in, l0_gla_w_gate, l0_gla_b_gate, l0_gla_norm_w, l0_mem_wk, l0_mem_wv,
         l0_w_out, l0_ln_w, l0_ln_b),
        (l1_w_in, l1_conv_w, l1_conv_b, l1_dt_bias, l1_a_log, l1_d_skip, l1_ssd_norm_w,
         l1_mem_wk, l1_mem_wv, l1_w_out, l1_ln_w, l1_ln_b),
    )
    for i in range(DEPTH):
        if i % 2 == 0:
            x = _gla_dilated_layer(x, mem, cos, sin, *layer_params[i])
        else:
            x = _ssd_layer(x, mem, *layer_params[i])
    return x
```

```python
import functools

import jax
import jax.numpy as jnp
from jax import lax
from jax.experimental import pallas as pl
from jax.experimental.pallas import tpu as pltpu

F32 = jnp.float32
BF16 = jnp.bfloat16

DEPTH = 2
DEEPNORM_ALPHA = (2 * DEPTH) ** 0.25
LN_EPS = 1e-5
RMS_EPS = 1e-6
ROPE_THETA = 10000.0
MEM_LEN = 256

GLA_HEADS = 4
GLA_DK = 128
GLA_DV = 256
GLA_GATE_RANK = 16
GLA_GATE_NORMALIZER = 16.0
GLA_CHUNK = 64

DIL_PATTERNS = ((128, 1), (512, 4), (2048, 16))
DIL_HEADS = 4
HEAD_DIM = 128
DIL_BLOCK = 128

MEM_HEADS = 4

SSD_HEADS = 24
SSD_HEAD_DIM = 64
SSD_GROUPS = 4
SSD_STATE = 128
SSD_CONV = 4
SSD_CHUNK = 128
SSD_INNER = SSD_HEADS * SSD_HEAD_DIM
SSD_GROUP_WIDTH = SSD_INNER // SSD_GROUPS
SSD_CONV_DIM = SSD_INNER + 2 * SSD_GROUPS * SSD_STATE

LANES = 128
VMEM_LIMIT = 48 * 1024 * 1024

L0_GQ, L0_GK, L0_GV, L0_GG = 0, 512, 1024, 2048
L0_DQ, L0_DK, L0_DV, L0_DG = 3072, 4608, 5120, 5632
L0_MQ, L0_MG = 6144, 6656
L0_MAIN = 7168
L1_Z, L1_XBC, L1_MQ, L1_MG = 0, 1536, 4096, 4608
L1_MAIN = 5120


def _nt(a, b):
    return lax.dot_general(a, b, (((1,), (1,)), ((), ())), preferred_element_type=F32)


def _tn(a, b):
    return lax.dot_general(a, b, (((0,), (0,)), ((), ())), preferred_element_type=F32)


def _nn(a, b):
    return jnp.dot(a, b, preferred_element_type=F32)


def _silu(x):
    return x * (1.0 / (1.0 + jnp.exp(-x)))


def _cparams(sem):
    return pltpu.CompilerParams(dimension_semantics=sem, vmem_limit_bytes=VMEM_LIMIT)


def _rope_table_kernel(pos_ref, inv_ref, sign_ref, cos_ref, sin_ref):
    ang = pos_ref[...].astype(F32) * inv_ref[...]
    cos_ref[...] = jnp.cos(ang)
    sin_ref[...] = jnp.sin(ang) * sign_ref[...]


def _rope_tables(positions):
    n = positions.size
    half = HEAD_DIM // 2
    inv = 1.0 / (ROPE_THETA ** (jnp.arange(0, HEAD_DIM, 2, dtype=F32) / HEAD_DIM))
    inv2 = jnp.concatenate([inv, inv]).reshape(1, HEAD_DIM)
    sign = jnp.concatenate([-jnp.ones((half,), F32), jnp.ones((half,), F32)]).reshape(1, HEAD_DIM)
    pos_b = jnp.broadcast_to(positions.reshape(n, 1), (n, HEAD_DIM))
    tb = 2048
    row = pl.BlockSpec((tb, HEAD_DIM), lambda i: (i, 0))
    vec = pl.BlockSpec((1, HEAD_DIM), lambda i: (0, 0))
    return pl.pallas_call(
        _rope_table_kernel,
        grid=(n // tb,),
        in_specs=[row, vec, vec],
        out_specs=[row, row],
        out_shape=[jax.ShapeDtypeStruct((n, HEAD_DIM), F32)] * 2,
        compiler_params=_cparams(("arbitrary",)),
        name="rope_tables",
    )(pos_b, inv2, sign)


def _proj_kernel(*refs, rope_tiles):
    if rope_tiles is None:
        x_ref, w_ref, ws_ref, o_ref, os_ref, xb_ref = refs
    else:
        x_ref, w_ref, ws_ref, c_ref, s_ref, sc_ref, o_ref, os_ref, xb_ref = refs
    j = pl.program_id(1)

    @pl.when(j == 0)
    def _():
        xb = x_ref[...].astype(BF16)
        xb_ref[...] = xb
        os_ref[...] = _nn(xb, ws_ref[...])

    acc = _nn(xb_ref[...], w_ref[...])
    if rope_tiles is None:
        o_ref[...] = acc.astype(o_ref.dtype)
        return
    lo, hi = rope_tiles
    is_rope = jnp.logical_and(j >= lo, j < hi)

    @pl.when(is_rope)
    def _():
        cos = c_ref[...]
        sin = s_ref[...]
        for k in range(acc.shape[1] // HEAD_DIM):
            sl = slice(k * HEAD_DIM, (k + 1) * HEAD_DIM)
            a = acc[:, sl] * sc_ref[:, sl]
            o_ref[:, sl] = (a * cos + pltpu.roll(a, HEAD_DIM // 2, 1) * sin).astype(o_ref.dtype)

    @pl.when(jnp.logical_not(is_rope))
    def _():
        o_ref[...] = acc.astype(o_ref.dtype)


def _project(x, w_main, w_small, rope=None, tm=1024, tn=1024):
    n, kdim = x.shape
    m = w_main.shape[1]
    in_specs = [
        pl.BlockSpec((tm, kdim), lambda i, j: (i, 0)),
        pl.BlockSpec((kdim, tn), lambda i, j: (0, j)),
        pl.BlockSpec((kdim, LANES), lambda i, j: (0, 0)),
    ]
    args = [x, w_main, w_small]
    rope_tiles = None
    if rope is not None:
        cos, sin, col_scale, first_col, last_col = rope
        rope_tiles = (first_col // tn, last_col // tn)
        in_specs += [
            pl.BlockSpec((tm, HEAD_DIM), lambda i, j: (i, 0)),
            pl.BlockSpec((tm, HEAD_DIM), lambda i, j: (i, 0)),
            pl.BlockSpec((1, tn), lambda i, j: (0, j)),
        ]
        args += [cos, sin, col_scale]
    return pl.pallas_call(
        functools.partial(_proj_kernel, rope_tiles=rope_tiles),
        grid=(n // tm, m // tn),
        in_specs=in_specs,
        out_specs=[
            pl.BlockSpec((tm, tn), lambda i, j: (i, j)),
            pl.BlockSpec((tm, LANES), lambda i, j: (i, 0)),
        ],
        out_shape=[jax.ShapeDtypeStruct((n, m), BF16), jax.ShapeDtypeStruct((n, LANES), F32)],
        scratch_shapes=[pltpu.VMEM((tm, kdim), BF16)],
        compiler_params=_cparams(("arbitrary", "arbitrary")),
        name="in_proj",
    )(*args)


def _gla_kernel(q_ref, k_ref, v_ref, g_ref, ga_ref, wg_ref, bg_ref, nw_ref, o_ref, st_ref, *, n_chunks):
    c_ = GLA_CHUNK

    @pl.when(pl.program_id(1) == 0)
    def _():
        st_ref[...] = jnp.zeros_like(st_ref)

    row = lax.broadcasted_iota(jnp.int32, (c_, c_), 0)
    col = lax.broadcasted_iota(jnp.int32, (c_, c_), 1)
    causal = col <= row
    tril = causal.astype(F32)
    wg = wg_ref[...]
    bg = bg_ref[...]
    nw = nw_ref[...]

    def chunk(ci, carry):
        r0 = pl.multiple_of(ci * c_, c_)
        rows = pl.ds(r0, c_)
        z = _nn(ga_ref[rows, :].astype(BF16), wg) + bg
        log_a = (jnp.minimum(z, 0.0) - jnp.log1p(jnp.exp(-jnp.abs(z)))) * (1.0 / GLA_GATE_NORMALIZER)
        bcum = jnp.dot(tril, log_a, preferred_element_type=F32, precision=lax.Precision.HIGHEST)
        blast = bcum[c_ - 1:c_, :]
        e_pos = jnp.exp(bcum)
        e_neg = jnp.exp(-bcum)
        e_tail = jnp.exp(blast - bcum)
        e_last = jnp.exp(blast)
        for h in range(GLA_HEADS):
            ks = slice(h * GLA_DK, (h + 1) * GLA_DK)
            vs = slice(h * GLA_DV, (h + 1) * GLA_DV)
            q = q_ref[rows, ks].astype(F32) * (GLA_DK ** -0.5)
            k = k_ref[rows, ks].astype(F32)
            q_dec = (q * e_pos[:, ks]).astype(BF16)
            k_inv = (k * e_neg[:, ks]).astype(BF16)
            k_tail = (k * e_tail[:, ks]).astype(BF16)
            v = v_ref[rows, vs]
            scores = jnp.where(causal, _nt(q_dec, k_inv), 0.0)
            st = st_ref[h]
            o = _nn(scores.astype(BF16), v) + _nt(q_dec, st.astype(BF16))
            st_ref[h] = st * e_last[:, ks] + _tn(v, k_tail)
            ms = jnp.mean(o * o, axis=-1, keepdims=True)
            o = o * lax.rsqrt(ms + RMS_EPS) * nw
            o_ref[rows, vs] = (o * _silu(g_ref[rows, vs].astype(F32))).astype(o_ref.dtype)
        return carry

    lax.fori_loop(0, n_chunks, chunk, 0)


def _gla(h_main, h_small, w_gate, b_gate, norm_w, batch, seq, tb=512):
    n = batch * seq
    nt = seq // tb
    wg = jnp.zeros((LANES, GLA_HEADS * GLA_DK), BF16).at[:GLA_GATE_RANK].set(w_gate.astype(BF16))
    idx = lambda c: (lambda b, t: (b * nt + t, c))
    const = lambda b, t: (0, 0)
    return pl.pallas_call(
        functools.partial(_gla_kernel, n_chunks=tb // GLA_CHUNK),
        grid=(batch, nt),
        in_specs=[
            pl.BlockSpec((tb, 512), idx(L0_GQ // 512)),
            pl.BlockSpec((tb, 512), idx(L0_GK // 512)),
            pl.BlockSpec((tb, 1024), idx(L0_GV // 1024)),
            pl.BlockSpec((tb, 1024), idx(L0_GG // 1024)),
            pl.BlockSpec((tb, LANES), idx(0)),
            pl.BlockSpec((LANES, 512), const),
            pl.BlockSpec((1, 512), const),
            pl.BlockSpec((1, GLA_DV), const),
        ],
        out_specs=pl.BlockSpec((tb, 1024), idx(0)),
        out_shape=jax.ShapeDtypeStruct((n, GLA_HEADS * GLA_DV), BF16),
        scratch_shapes=[pltpu.VMEM((GLA_HEADS, GLA_DV, GLA_DK), F32)],
        compiler_params=_cparams(("arbitrary", "arbitrary")),
        name="gla",
    )(h_main, h_main, h_main, h_main, h_small, wg, b_gate.reshape(1, -1), norm_w.reshape(1, -1))


def _dil_kernel(q_ref, kc_ref, kp_ref, vc_ref, vp_ref, o_ref, lse_ref, *, q_blocks):
    blk = DIL_BLOCK
    first = pl.program_id(2) == 0
    row = lax.broadcasted_iota(jnp.int32, (blk, blk), 0)
    col = lax.broadcasted_iota(jnp.int32, (blk, blk), 1)
    cur_ok = col <= row
    prev_ok = col >= row
    prev_ok0 = col >= row + jnp.where(first, blk, 0)
    lane_head = lax.broadcasted_iota(jnp.int32, (blk, LANES), 1) // (LANES // DIL_HEADS)
    neg = -jnp.inf
    for i in range(q_blocks):
        qrows = slice(i * blk, (i + 1) * blk)
        lse_tile = jnp.zeros((blk, LANES), F32)
        for h in range(DIL_HEADS):
            hs = slice(h * HEAD_DIM, (h + 1) * HEAD_DIM)
            q = q_ref[qrows, hs]
            if i == 0:
                k_prev, v_prev, ok_p = kp_ref[:, hs], vp_ref[:, hs], prev_ok0
            else:
                prows = slice((i - 1) * blk, i * blk)
                k_prev, v_prev, ok_p = kc_ref[prows, hs], vc_ref[prows, hs], prev_ok
            s_p = jnp.where(ok_p, _nt(q, k_prev), neg)
            s_c = jnp.where(cur_ok, _nt(q, kc_ref[qrows, hs]), neg)
            m = jnp.maximum(jnp.max(s_p, -1, keepdims=True), jnp.max(s_c, -1, keepdims=True))
            p_p = jnp.exp(s_p - m)
            p_c = jnp.exp(s_c - m)
            den = jnp.sum(p_p, -1, keepdims=True) + jnp.sum(p_c, -1, keepdims=True)
            o = _nn(p_p.astype(BF16), v_prev) + _nn(p_c.astype(BF16), vc_ref[qrows, hs])
            o_ref[qrows, hs] = o * (1.0 / den)
            lse_tile = jnp.where(lane_head == h, m + jnp.log(den), lse_tile)
        lse_ref[qrows, :] = lse_tile


def _dilated_group(h_main, batch, seq, g, dilation):
    n = batch * seq
    sub_len = seq // dilation
    q_blocks = min(4, sub_len // DIL_BLOCK)
    tq = q_blocks * DIL_BLOCK
    width = DIL_HEADS * HEAD_DIM
    cols = L0_MAIN // width
    hv = h_main.reshape(batch, sub_len, dilation * L0_MAIN)
    qc, kc, vc = L0_DQ // width + g, L0_DK // width, L0_DV // width

    def cur(c):
        return pl.BlockSpec((None, tq, width), lambda b, r, t: (b, t, r * cols + c))

    def prev(c):
        return pl.BlockSpec((None, DIL_BLOCK, width),
                            lambda b, r, t: (b, jnp.maximum(t * q_blocks - 1, 0), r * cols + c))

    o, lse = pl.pallas_call(
        functools.partial(_dil_kernel, q_blocks=q_blocks),
        grid=(batch, dilation, sub_len // tq),
        in_specs=[cur(qc), cur(kc), prev(kc), cur(vc), prev(vc)],
        out_specs=[
            pl.BlockSpec((None, tq, width), lambda b, r, t: (b, t, r)),
            pl.BlockSpec((None, tq, LANES), lambda b, r, t: (b, t, r)),
        ],
        out_shape=[
            jax.ShapeDtypeStruct((batch, sub_len, dilation * width), F32),
            jax.ShapeDtypeStruct((batch, sub_len, dilation * LANES), F32),
        ],
        compiler_params=_cparams(("arbitrary", "arbitrary", "arbitrary")),
        name=f"dilated_d{dilation}",
    )(hv, hv, hv, hv, hv)
    return o.reshape(n, width), lse.reshape(n, LANES)


def _dil_merge_kernel(o0_ref, o1_ref, o2_ref, l0_ref, l1_ref, l2_ref, g_ref, out_ref):
    lane_w = LANES // DIL_HEADS
    for h in range(DIL_HEADS):
        hs = slice(h * HEAD_DIM, (h + 1) * HEAD_DIM)
        ls = slice(h * lane_w, h * lane_w + 1)
        l0, l1, l2 = l0_ref[:, ls], l1_ref[:, ls], l2_ref[:, ls]
        m = jnp.maximum(jnp.maximum(l0, l1), l2)
        w0, w1, w2 = jnp.exp(l0 - m), jnp.exp(l1 - m), jnp.exp(l2 - m)
        o = (w0 * o0_ref[:, hs] + w1 * o1_ref[:, hs] + w2 * o2_ref[:, hs]) * (1.0 / (w0 + w1 + w2))
        out_ref[:, hs] = (o * _silu(g_ref[:, hs].astype(F32))).astype(out_ref.dtype)


def _dil_merge(outs, lses, h_main, tm=1024):
    n = h_main.shape[0]
    width = DIL_HEADS * HEAD_DIM
    o_spec = pl.BlockSpec((tm, width), lambda i: (i, 0))
    l_spec = pl.BlockSpec((tm, LANES), lambda i: (i, 0))
    return pl.pallas_call(
        _dil_merge_kernel,
        grid=(n // tm,),
        in_specs=[o_spec] * 3 + [l_spec] * 3 + [pl.BlockSpec((tm, width), lambda i: (i, L0_DG // width))],
        out_specs=o_spec,
        out_shape=jax.ShapeDtypeStruct((n, width), BF16),
        compiler_params=_cparams(("arbitrary",)),
        name="dilated_merge",
    )(*outs, *lses, h_main)


def _mem_kv_kernel(m_ref, w_ref, o_ref):
    o_ref[...] = _nn(m_ref[...].astype(BF16), w_ref[...]).astype(o_ref.dtype)


def _mem_kv(mem2d, wk, wv):
    rows, kdim = mem2d.shape
    w = jnp.concatenate([wk, wv], axis=1).astype(BF16)
    tm = MEM_LEN
    return pl.pallas_call(
        _mem_kv_kernel,
        grid=(rows // tm,),
        in_specs=[pl.BlockSpec((tm, kdim), lambda i: (i, 0)), pl.BlockSpec(w.shape, lambda i: (0, 0))],
        out_specs=pl.BlockSpec((tm, w.shape[1]), lambda i: (i, 0)),
        out_shape=jax.ShapeDtypeStruct((rows, w.shape[1]), BF16),
        compiler_params=_cparams(("arbitrary",)),
        name="mem_kv",
    )(mem2d, w)


def _mem_attn_kernel(q_ref, g_ref, k_ref, v_ref, o_ref, *, tq):
    sub = 128
    for i in range(tq // sub):
        rows = slice(i * sub, (i + 1) * sub)
        for h in range(MEM_HEADS):
            hs = slice(h * HEAD_DIM, (h + 1) * HEAD_DIM)
            s = _nt(q_ref[rows, hs], k_ref[:, hs]) * (HEAD_DIM ** -0.5)
            m = jnp.max(s, -1, keepdims=True)
            p = jnp.exp(s - m)
            den = jnp.sum(p, -1, keepdims=True)
            o = _nn(p.astype(BF16), v_ref[:, hs]) * (1.0 / den)
            o_ref[rows, hs] = (o * _silu(g_ref[rows, hs].astype(F32))).astype(o_ref.dtype)


def _mem_attn(h_main, kv, batch, seq, q_col, g_col, tq=512):
    n = batch * seq
    width = MEM_HEADS * HEAD_DIM
    nt = seq // tq
    return pl.pallas_call(
        functools.partial(_mem_attn_kernel, tq=tq),
        grid=(batch, nt),
        in_specs=[
            pl.BlockSpec((tq, width), lambda b, t: (b * nt + t, q_col // width)),
            pl.BlockSpec((tq, width), lambda b, t: (b * nt + t, g_col // width)),
            pl.BlockSpec((MEM_LEN, width), lambda b, t: (b, 0)),
            pl.BlockSpec((MEM_LEN, width), lambda b, t: (b, 1)),
        ],
        out_specs=pl.BlockSpec((tq, width), lambda b, t: (b * nt + t, 0)),
        out_shape=jax.ShapeDtypeStruct((n, width), BF16),
        compiler_params=_cparams(("arbitrary", "arbitrary")),
        name="mem_attn",
    )(h_main, h_main, kv, kv)


def _out_kernel(*refs, n_parts):
    a_refs = refs[:n_parts]
    w_refs = refs[n_parts:2 * n_parts]
    x_ref, lw_ref, lb_ref, o_ref = refs[2 * n_parts:]
    acc = DEEPNORM_ALPHA * x_ref[...]
    for a_ref, w_ref in zip(a_refs, w_refs):
        acc = acc + _nn(a_ref[...], w_ref[...])
    mu = jnp.mean(acc, -1, keepdims=True)
    d = acc - mu
    var = jnp.mean(d * d, -1, keepdims=True)
    o_ref[...] = d * lax.rsqrt(var + LN_EPS) * lw_ref[...] + lb_ref[...]


def _out_proj_ln(parts, w_out, x, ln_w, ln_b, tm=512):
    n, dm = x.shape
    w_bf = w_out.astype(BF16)
    ws, off = [], 0
    for p in parts:
        ws.append(w_bf[off:off + p.shape[1]])
        off += p.shape[1]
    row = lambda width: pl.BlockSpec((tm, width), lambda i: (i, 0))
    const = lambda shape: pl.BlockSpec(shape, lambda i: (0, 0))
    return pl.pallas_call(
        functools.partial(_out_kernel, n_parts=len(parts)),
        grid=(n // tm,),
        in_specs=[row(p.shape[1]) for p in parts] + [const(w.shape) for w in ws]
        + [row(dm), const((1, dm)), const((1, dm))],
        out_specs=row(dm),
        out_shape=jax.ShapeDtypeStruct((n, dm), F32),
        compiler_params=_cparams(("arbitrary",)),
        name="out_proj_ln",
    )(*parts, *ws, x, ln_w.reshape(1, dm), ln_b.reshape(1, dm))


def _ssd_kernel(z_ref, xs_ref, bm_ref, cm_ref, dt_ref, cw_ref, cb_ref, dtb_ref, a_ref, dsk_ref, nw_ref,
                o_ref, u_ref, st_ref, *, n_chunks):
    q_ = SSD_CHUNK
    halo = 8

    @pl.when(pl.program_id(1) == 0)
    def _():
        st_ref[...] = jnp.zeros_like(st_ref)
        u_ref[0:halo, :] = jnp.zeros((halo, SSD_CONV_DIM), F32)

    row = lax.broadcasted_iota(jnp.int32, (q_, q_), 0)
    col = lax.broadcasted_iota(jnp.int32, (q_, q_), 1)
    causal = col <= row
    tril = causal.astype(F32)
    lane = lax.broadcasted_iota(jnp.int32, (q_, LANES), 1)
    lo_half = lane < SSD_HEAD_DIM
    lane_row = lax.broadcasted_iota(jnp.int32, (1, LANES), 1)
    lo_half_row = lane_row < SSD_HEAD_DIM
    a_neg = -jnp.exp(a_ref[...])

    def pair_expand(t, h1):
        return jnp.where(lo_half, t[:, h1:h1 + 1], t[:, h1 + 1:h1 + 2])

    def chunk(ci, carry):
        r0 = pl.multiple_of(ci * q_, q_)
        rows = pl.ds(r0, q_)
        bc_w = SSD_GROUPS * SSD_STATE
        u_ref[halo:halo + q_, 0:SSD_INNER] = xs_ref[rows, :].astype(F32)
        u_ref[halo:halo + q_, SSD_INNER:SSD_INNER + bc_w] = bm_ref[rows, :].astype(F32)
        u_ref[halo:halo + q_, SSD_INNER + bc_w:SSD_CONV_DIM] = cm_ref[rows, :].astype(F32)
        conv = cb_ref[...]
        for k in range(SSD_CONV):
            s0 = halo - (SSD_CONV - 1) + k
            conv = conv + cw_ref[k:k + 1, :] * u_ref[s0:s0 + q_, :]
        u_ref[0:halo, :] = u_ref[q_:q_ + halo, :]
        u = _silu(conv)
        dt_in = dt_ref[rows, :] + dtb_ref[...]
        dt = jnp.maximum(dt_in, 0.0) + jnp.log1p(jnp.exp(-jnp.abs(dt_in)))
        adt = dt * a_neg
        acum = jnp.dot(tril, adt, preferred_element_type=F32, precision=lax.Precision.HIGHEST)
        acum_t = acum.T
        dt_t = dt.T
        last = acum[q_ - 1:q_, :]
        e_acum = jnp.exp(acum)
        w_state = dt * jnp.exp(last - acum)
        c_decay = jnp.exp(last)
        for g in range(SSD_GROUPS):
            gs = slice(g * SSD_GROUP_WIDTH, (g + 1) * SSD_GROUP_WIDTH)
            b_f = u[:, SSD_INNER + g * SSD_STATE:SSD_INNER + (g + 1) * SSD_STATE]
            c_f = u[:, SSD_INNER + (SSD_GROUPS + g) * SSD_STATE:SSD_INNER + (SSD_GROUPS + g + 1) * SSD_STATE]
            b_bf = b_f.astype(BF16)
            c_bf = c_f.astype(BF16)
            cb = _nt(c_bf, b_bf)
            bt_bf = b_f.T.astype(BF16)
            st = st_ref[g]
            y_off_all = _nn(c_bf, st.astype(BF16))
            ys, xdecs, decs = [], [], []
            ss = jnp.zeros((q_, 1), F32)
            for j in range(SSD_GROUP_WIDTH // LANES):
                h1 = (g * SSD_GROUP_WIDTH + j * LANES) // SSD_HEAD_DIM
                ps = slice(g * SSD_GROUP_WIDTH + j * LANES, g * SSD_GROUP_WIDTH + (j + 1) * LANES)
                xs = u[:, ps]
                mats = []
                for hh in (h1, h1 + 1):
                    diff = acum[:, hh:hh + 1] - acum_t[hh:hh + 1, :]
                    lm = jnp.exp(jnp.where(causal, diff, -jnp.inf))
                    mats.append((cb * lm * dt_t[hh:hh + 1, :]).astype(BF16))
                lhs = jnp.concatenate(mats, axis=1)
                rhs = jnp.concatenate([jnp.where(lo_half, xs, 0.0), jnp.where(lo_half, 0.0, xs)],
                                      axis=0).astype(BF16)
                y = _nn(lhs, rhs)
                y = y + y_off_all[:, j * LANES:(j + 1) * LANES] * pair_expand(e_acum, h1)
                y = y + dsk_ref[:, ps] * xs
                y = y * _silu(z_ref[rows, ps].astype(F32))
                ss = ss + jnp.sum(y * y, -1, keepdims=True)
                ys.append(y)
                xdecs.append((xs * pair_expand(w_state, h1)).astype(BF16))
                decs.append(jnp.where(lo_half_row, c_decay[:, h1:h1 + 1], c_decay[:, h1 + 1:h1 + 2]))
            st_ref[g] = st * jnp.concatenate(decs, axis=1) + _nn(bt_bf, jnp.concatenate(xdecs, axis=1))
            inv = lax.rsqrt(ss * (1.0 / SSD_GROUP_WIDTH) + RMS_EPS)
            for j, y in enumerate(ys):
                ps = slice(g * SSD_GROUP_WIDTH + j * LANES, g * SSD_GROUP_WIDTH + (j + 1) * LANES)
                o_ref[rows, ps] = (y * inv * nw_ref[:, ps]).astype(o_ref.dtype)
        return carry

    lax.fori_loop(0, n_chunks, chunk, 0)


def _ssd(h_main, h_small, conv_w, conv_b, dt_bias, a_log, d_skip, norm_w, batch, seq, tb=512):
    n = batch * seq
    nt = seq // tb
    bc_w = SSD_GROUPS * SSD_STATE
    pad = lambda v: jnp.zeros((1, LANES), F32).at[0, :SSD_HEADS].set(v.astype(F32))
    idx = lambda c: (lambda b, t: (b * nt + t, c))
    const = lambda b, t: (0, 0)
    return pl.pallas_call(
        functools.partial(_ssd_kernel, n_chunks=tb // SSD_CHUNK),
        grid=(batch, nt),
        in_specs=[
            pl.BlockSpec((tb, SSD_INNER), idx(L1_Z // SSD_INNER)),
            pl.BlockSpec((tb, SSD_INNER), idx(L1_XBC // SSD_INNER)),
            pl.BlockSpec((tb, bc_w), idx((L1_XBC + SSD_INNER) // bc_w)),
            pl.BlockSpec((tb, bc_w), idx((L1_XBC + SSD_INNER + bc_w) // bc_w)),
            pl.BlockSpec((tb, LANES), idx(0)),
            pl.BlockSpec((SSD_CONV, SSD_CONV_DIM), const),
            pl.BlockSpec((1, SSD_CONV_DIM), const),
            pl.BlockSpec((1, LANES), const),
            pl.BlockSpec((1, LANES), const),
            pl.BlockSpec((1, SSD_INNER), const),
            pl.BlockSpec((1, SSD_INNER), const),
        ],
        out_specs=pl.BlockSpec((tb, SSD_INNER), idx(0)),
        out_shape=jax.ShapeDtypeStruct((n, SSD_INNER), BF16),
        scratch_shapes=[
            pltpu.VMEM((SSD_CHUNK + 8, SSD_CONV_DIM), F32),
            pltpu.VMEM((SSD_GROUPS, SSD_STATE, SSD_GROUP_WIDTH), F32),
        ],
        compiler_params=_cparams(("arbitrary", "arbitrary")),
        name="ssd",
    )(h_main, h_main, h_main, h_main, h_small, conv_w, conv_b.reshape(1, -1),
      pad(dt_bias), pad(a_log), jnp.repeat(d_skip.astype(F32), SSD_HEAD_DIM).reshape(1, -1),
      norm_w.reshape(1, -1))


def _pad_cols(w, width=LANES):
    return jnp.zeros((w.shape[0], width), BF16).at[:, :w.shape[1]].set(w.astype(BF16))


def kernel(x, mem, positions, l0_w_in, l0_gla_w_gate, l0_gla_b_gate, l0_gla_norm_w, l0_mem_wk, l0_mem_wv, l0_w_out, l0_ln_w, l0_ln_b, l1_w_in, l1_conv_w, l1_conv_b, l1_dt_bias, l1_a_log, l1_d_skip, l1_ssd_norm_w, l1_mem_wk, l1_mem_wv, l1_w_out, l1_ln_w, l1_ln_b):
    batch, seq, dm = x.shape
    n = batch * seq
    x2 = x.reshape(n, dm)
    mem2 = mem.reshape(batch * MEM_LEN, dm)

    ga0 = L0_DQ + GLA_GATE_RANK
    w0_main = jnp.concatenate([l0_w_in[:, :L0_DQ], l0_w_in[:, ga0:]], axis=1).astype(BF16)
    w0_small = _pad_cols(l0_w_in[:, L0_DQ:ga0])
    cos, sin = _rope_tables(positions)
    col_scale = jnp.ones((1, L0_MAIN), F32).at[0, L0_DQ:L0_DK].set(HEAD_DIM ** -0.5)
    h0, h0s = _project(x2, w0_main, w0_small, rope=(cos, sin, col_scale, L0_DQ, L0_DV))
    o_a = _gla(h0, h0s, l0_gla_w_gate, l0_gla_b_gate, l0_gla_norm_w, batch, seq)
    outs, lses = [], []
    for g, (_, dilation) in enumerate(DIL_PATTERNS):
        o, lse = _dilated_group(h0, batch, seq, g, dilation)
        outs.append(o)
        lses.append(lse)
    o_b = _dil_merge(outs, lses, h0)
    kv0 = _mem_kv(mem2, l0_mem_wk, l0_mem_wv)
    o_m = _mem_attn(h0, kv0, batch, seq, L0_MQ, L0_MG)
    x1 = _out_proj_ln([o_a, o_b, o_m], l0_w_out, x2, l0_ln_w, l0_ln_b)

    dt0 = SSD_INNER + SSD_CONV_DIM
    w1_main = jnp.concatenate([l1_w_in[:, :dt0], l1_w_in[:, dt0 + SSD_HEADS:]], axis=1).astype(BF16)
    w1_small = _pad_cols(l1_w_in[:, dt0:dt0 + SSD_HEADS])
    h1, h1s = _project(x1, w1_main, w1_small)
    y = _ssd(h1, h1s, l1_conv_w, l1_conv_b, l1_dt_bias, l1_a_log, l1_d_skip, l1_ssd_norm_w, batch, seq)
    kv1 = _mem_kv(mem2, l1_mem_wk, l1_mem_wv)
    o_m1 = _mem_attn(h1, kv1, batch, seq, L1_MQ, L1_MG)
    x2_out = _out_proj_ln([y, o_m1], l1_w_out, x1, l1_ln_w, l1_ln_b)
    return x2_out.reshape(batch, seq, dm)
```

```python
import functools

import jax
import jax.numpy as jnp
from jax import lax
from jax.experimental import pallas as pl
from jax.experimental.pallas import tpu as pltpu

F32 = jnp.float32
BF16 = jnp.bfloat16

DEPTH = 2
DEEPNORM_ALPHA = (2 * DEPTH) ** 0.25
LN_EPS = 1e-5
RMS_EPS = 1e-6
ROPE_THETA = 10000.0
MEM_LEN = 256

GLA_HEADS = 4
GLA_DK = 128
GLA_DV = 256
GLA_GATE_RANK = 16
GLA_GATE_NORMALIZER = 16.0
GLA_CHUNK = 64

DIL_PATTERNS = ((128, 1), (512, 4), (2048, 16))
DIL_HEADS = 4
HEAD_DIM = 128
DIL_BLOCK = 128

MEM_HEADS = 4

SSD_HEADS = 24
SSD_HEAD_DIM = 64
SSD_GROUPS = 4
SSD_STATE = 128
SSD_CONV = 4
SSD_CHUNK = 128
SSD_INNER = SSD_HEADS * SSD_HEAD_DIM
SSD_GROUP_WIDTH = SSD_INNER // SSD_GROUPS
SSD_CONV_DIM = SSD_INNER + 2 * SSD_GROUPS * SSD_STATE

LANES = 128
LOG2_E = 1.4426950408889634
VMEM_LIMIT = 48 * 1024 * 1024

L0_GQ, L0_GK, L0_GV, L0_GG = 0, 512, 1024, 2048
L0_DQ, L0_DK, L0_DV, L0_DG = 3072, 4608, 5120, 5632
L0_MQ, L0_MG = 6144, 6656
L0_MAIN = 7168
L1_Z, L1_XBC, L1_MQ, L1_MG = 0, 1536, 4096, 4608
L1_MAIN = 5120


def _nt(a, b):
    return lax.dot_general(a, b, (((1,), (1,)), ((), ())), preferred_element_type=F32)


def _tn(a, b):
    return lax.dot_general(a, b, (((0,), (0,)), ((), ())), preferred_element_type=F32)


def _nn(a, b):
    return jnp.dot(a, b, preferred_element_type=F32)


def _silu(x):
    return 0.5 * x * (1.0 + jnp.tanh(0.5 * x))


def _cparams(sem):
    return pltpu.CompilerParams(dimension_semantics=sem, vmem_limit_bytes=VMEM_LIMIT)


def _rope_table_kernel(pos_ref, inv_ref, sign_ref, cos_ref, sin_ref):
    ang = pos_ref[...].astype(F32) * inv_ref[...]
    cos_ref[...] = jnp.cos(ang)
    sin_ref[...] = jnp.sin(ang) * sign_ref[...]


def _rope_tables(positions):
    n = positions.size
    half = HEAD_DIM // 2
    inv = 1.0 / (ROPE_THETA ** (jnp.arange(0, HEAD_DIM, 2, dtype=F32) / HEAD_DIM))
    inv2 = jnp.concatenate([inv, inv]).reshape(1, HEAD_DIM)
    sign = jnp.concatenate([-jnp.ones((half,), F32), jnp.ones((half,), F32)]).reshape(1, HEAD_DIM)
    pos_b = jnp.broadcast_to(positions.reshape(n, 1), (n, HEAD_DIM))
    tb = 2048
    row = pl.BlockSpec((tb, HEAD_DIM), lambda i: (i, 0))
    vec = pl.BlockSpec((1, HEAD_DIM), lambda i: (0, 0))
    return pl.pallas_call(
        _rope_table_kernel,
        grid=(n // tb,),
        in_specs=[row, vec, vec],
        out_specs=[row, row],
        out_shape=[jax.ShapeDtypeStruct((n, HEAD_DIM), F32)] * 2,
        compiler_params=_cparams(("arbitrary",)),
        name="rope_tables",
    )(pos_b, inv2, sign)


def _proj_kernel(*refs, rope_tiles):
    if rope_tiles is None:
        x_ref, w_ref, ws_ref, o_ref, os_ref, xb_ref = refs
    else:
        x_ref, w_ref, ws_ref, c_ref, s_ref, sc_ref, o_ref, os_ref, xb_ref = refs
    j = pl.program_id(1)

    @pl.when(j == 0)
    def _():
        xb = x_ref[...].astype(BF16)
        xb_ref[...] = xb
        os_ref[...] = _nn(xb, ws_ref[...])

    acc = _nn(xb_ref[...], w_ref[...])
    if rope_tiles is None:
        o_ref[...] = acc.astype(o_ref.dtype)
        return
    lo, hi = rope_tiles
    is_rope = jnp.logical_and(j >= lo, j < hi)

    @pl.when(is_rope)
    def _():
        cos = c_ref[...]
        sin = s_ref[...]
        for k in range(acc.shape[1] // HEAD_DIM):
            sl = slice(k * HEAD_DIM, (k + 1) * HEAD_DIM)
            a = acc[:, sl] * sc_ref[:, sl]
            o_ref[:, sl] = (a * cos + pltpu.roll(a, HEAD_DIM // 2, 1) * sin).astype(o_ref.dtype)

    @pl.when(jnp.logical_not(is_rope))
    def _():
        o_ref[...] = acc.astype(o_ref.dtype)


def _project(x, w_main, w_small, rope=None, tm=1024, tn=1024):
    n, kdim = x.shape
    m = w_main.shape[1]
    in_specs = [
        pl.BlockSpec((tm, kdim), lambda i, j: (i, 0)),
        pl.BlockSpec((kdim, tn), lambda i, j: (0, j)),
        pl.BlockSpec((kdim, LANES), lambda i, j: (0, 0)),
    ]
    args = [x, w_main, w_small]
    rope_tiles = None
    if rope is not None:
        cos, sin, col_scale, first_col, last_col = rope
        rope_tiles = (first_col // tn, last_col // tn)
        in_specs += [
            pl.BlockSpec((tm, HEAD_DIM), lambda i, j: (i, 0)),
            pl.BlockSpec((tm, HEAD_DIM), lambda i, j: (i, 0)),
            pl.BlockSpec((1, tn), lambda i, j: (0, j)),
        ]
        args += [cos, sin, col_scale]
    return pl.pallas_call(
        functools.partial(_proj_kernel, rope_tiles=rope_tiles),
        grid=(n // tm, m // tn),
        in_specs=in_specs,
        out_specs=[
            pl.BlockSpec((tm, tn), lambda i, j: (i, j)),
            pl.BlockSpec((tm, LANES), lambda i, j: (i, 0)),
        ],
        out_shape=[jax.ShapeDtypeStruct((n, m), BF16), jax.ShapeDtypeStruct((n, LANES), F32)],
        scratch_shapes=[pltpu.VMEM((tm, kdim), BF16)],
        compiler_params=_cparams(("arbitrary", "arbitrary")),
        name="in_proj",
    )(*args)


def _gla_kernel(q_ref, k_ref, v_ref, g_ref, ga_ref, wg_ref, bg_ref, nw_ref, o_ref, st_ref, *, n_chunks):
    c_ = GLA_CHUNK

    @pl.when(pl.program_id(1) == 0)
    def _():
        st_ref[...] = jnp.zeros_like(st_ref)

    row = lax.broadcasted_iota(jnp.int32, (c_, c_), 0)
    col = lax.broadcasted_iota(jnp.int32, (c_, c_), 1)
    causal = col <= row
    tril = causal.astype(F32)
    wg = wg_ref[...]
    bg = bg_ref[...]
    nw = nw_ref[...]

    def chunk(ci, carry):
        r0 = pl.multiple_of(ci * c_, c_)
        rows = pl.ds(r0, c_)
        z = _nn(ga_ref[rows, :].astype(BF16), wg) + bg
        log_a = (jnp.minimum(z, 0.0) - jnp.log1p(jnp.exp(-jnp.abs(z)))) * (1.0 / GLA_GATE_NORMALIZER)
        bcum = jnp.dot(tril, log_a, preferred_element_type=F32, precision=lax.Precision.HIGHEST)
        blast = bcum[c_ - 1:c_, :]
        e_pos = jnp.exp(bcum)
        e_neg = jnp.exp(-bcum)
        e_tail = jnp.exp(blast - bcum)
        e_last = jnp.exp(blast)
        for h in range(GLA_HEADS):
            ks = slice(h * GLA_DK, (h + 1) * GLA_DK)
            vs = slice(h * GLA_DV, (h + 1) * GLA_DV)
            q = q_ref[rows, ks].astype(F32) * (GLA_DK ** -0.5)
            k = k_ref[rows, ks].astype(F32)
            q_dec = (q * e_pos[:, ks]).astype(BF16)
            k_inv = (k * e_neg[:, ks]).astype(BF16)
            k_tail = (k * e_tail[:, ks]).astype(BF16)
            v = v_ref[rows, vs]
            scores = jnp.where(causal, _nt(q_dec, k_inv), 0.0)
            st = st_ref[h]
            o = _nn(scores.astype(BF16), v) + _nt(q_dec, st.astype(BF16))
            st_ref[h] = st * e_last[:, ks] + _tn(v, k_tail)
            ms = jnp.mean(o * o, axis=-1, keepdims=True)
            o = o * lax.rsqrt(ms + RMS_EPS) * nw
            o_ref[rows, vs] = (o * _silu(g_ref[rows, vs].astype(F32))).astype(o_ref.dtype)
        return carry

    lax.fori_loop(0, n_chunks, chunk, 0, unroll=2)


def _gla(h_main, h_small, w_gate, b_gate, norm_w, batch, seq, tb=512):
    n = batch * seq
    nt = seq // tb
    wg = jnp.zeros((LANES, GLA_HEADS * GLA_DK), BF16).at[:GLA_GATE_RANK].set(w_gate.astype(BF16))
    idx = lambda c: (lambda b, t: (b * nt + t, c))
    const = lambda b, t: (0, 0)
    return pl.pallas_call(
        functools.partial(_gla_kernel, n_chunks=tb // GLA_CHUNK),
        grid=(batch, nt),
        in_specs=[
            pl.BlockSpec((tb, 512), idx(L0_GQ // 512)),
            pl.BlockSpec((tb, 512), idx(L0_GK // 512)),
            pl.BlockSpec((tb, 1024), idx(L0_GV // 1024)),
            pl.BlockSpec((tb, 1024), idx(L0_GG // 1024)),
            pl.BlockSpec((tb, LANES), idx(0)),
            pl.BlockSpec((LANES, 512), const),
            pl.BlockSpec((1, 512), const),
            pl.BlockSpec((1, GLA_DV), const),
        ],
        out_specs=pl.BlockSpec((tb, 1024), idx(0)),
        out_shape=jax.ShapeDtypeStruct((n, GLA_HEADS * GLA_DV), BF16),
        scratch_shapes=[pltpu.VMEM((GLA_HEADS, GLA_DV, GLA_DK), F32)],
        compiler_params=_cparams(("arbitrary", "arbitrary")),
        name="gla",
    )(h_main, h_main, h_main, h_main, h_small, wg, b_gate.reshape(1, -1), norm_w.reshape(1, -1))


DIL_WINDOW = 2048
SLAB_GROUP = 16
SLAB_PITCH = 24
SEG = 32
TILE_UNROLL = 8


def _dil_kernel(q0_ref, q1_ref, q2_ref, kc_ref, kp_ref, vc_ref, vp_ref, g_ref, out_ref,
                kcat, vcat, ks, vs, q1s, q2s, o0s, l0s, o1s, l1s, o2s, l2s, bias):
    blk = DIL_BLOCK
    w = DIL_WINDOW
    groups = w // SLAB_GROUP
    t = pl.program_id(1)
    first_window = t == 0

    rho = lax.broadcasted_iota(jnp.int32, (blk, 2 * blk), 0)
    kap = lax.broadcasted_iota(jnp.int32, (blk, 2 * blk), 1)
    in_prev = kap < blk
    kap_l = jnp.where(in_prev, kap, kap - blk)
    perm = lambda x: 4 * (x % SEG) + x // SEG
    neg = jnp.float32(-jnp.inf)
    for slot, (jq, jk) in enumerate(((rho, kap_l), (perm(rho), perm(kap_l)))):
        ok_prev = jnp.where(jk >= jq, 0.0, neg)
        ok_cur = jnp.where(jk <= jq, 0.0, neg)
        bias[2 * slot] = jnp.where(in_prev, ok_prev, ok_cur)
        bias[2 * slot + 1] = jnp.where(in_prev, neg, ok_cur)

    kcat[0:blk, :] = kp_ref[w - blk:w, :]
    kcat[blk:blk + w, :] = kc_ref[...]
    vcat[0:blk, :] = vp_ref[w - blk:w, :]
    vcat[blk:blk + w, :] = vc_ref[...]

    def to_slabs(i, carry):
        src = pl.ds(pl.multiple_of(i * SLAB_GROUP, SLAB_GROUP), SLAB_GROUP)
        prev = pl.ds(pl.multiple_of(i * SLAB_PITCH, 8), SLAB_GROUP)
        cur = pl.ds(pl.multiple_of((groups + i) * SLAB_PITCH, 8), SLAB_GROUP)
        ks[prev, :] = kp_ref[src, :].astype(F32)
        ks[cur, :] = kc_ref[src, :].astype(F32)
        vs[prev, :] = vp_ref[src, :].astype(F32)
        vs[cur, :] = vc_ref[src, :].astype(F32)
        q1s[prev, :] = q1_ref[src, :].astype(F32)
        q2s[prev, :] = q2_ref[src, :].astype(F32)
        return carry

    lax.fori_loop(0, groups, to_slabs, 0, unroll=4)

    def attend(q, k, v, b):
        s = _nt(q, k) + b
        m = jnp.max(s, -1, keepdims=True)
        p = jnp.exp2(s - m)
        den = jnp.sum(p, -1, keepdims=True)
        o = _nn(p.astype(BF16), v) * (1.0 / den)
        return o, jnp.broadcast_to(m + jnp.log2(den), (blk, LANES))

    def group0(i, carry):
        r0 = pl.multiple_of(i * blk, blk)
        sel = jnp.where(jnp.logical_and(first_window, i == 0), 1, 0)
        o, l = attend(q0_ref[pl.ds(r0, blk), :], kcat[pl.ds(r0, 2 * blk), :], vcat[pl.ds(r0, 2 * blk), :],
                      bias[sel])
        o0s[pl.ds(r0, blk), :] = o
        l0s[pl.ds(r0, blk), :] = l
        return carry

    lax.fori_loop(0, w // blk, group0, 0, unroll=TILE_UNROLL)

    def seg(ref, group, b):
        return ref[pl.ds(group * SLAB_PITCH + b, SEG, stride=SLAB_PITCH), :]

    def group1(idx, carry):
        r = idx // 4
        n = idx % 4
        g_cur = groups + SEG * n
        q = jnp.concatenate([seg(q1s, SEG * n, r + 4 * j) for j in range(4)], axis=0).astype(BF16)
        k = jnp.concatenate([seg(ks, g_cur - SEG, r + 4 * j) for j in range(4)]
                            + [seg(ks, g_cur, r + 4 * j) for j in range(4)], axis=0).astype(BF16)
        v = jnp.concatenate([seg(vs, g_cur - SEG, r + 4 * j) for j in range(4)]
                            + [seg(vs, g_cur, r + 4 * j) for j in range(4)], axis=0).astype(BF16)
        sel = jnp.where(jnp.logical_and(first_window, n == 0), 3, 2)
        o, l = attend(q, k, v, bias[sel])
        for j in range(4):
            dst = pl.ds(SEG * n * SLAB_PITCH + r + 4 * j, SEG, stride=SLAB_PITCH)
            o1s[dst, :] = o[j * SEG:(j + 1) * SEG]
            l1s[dst, :] = l[j * SEG:(j + 1) * SEG]
        return carry

    lax.fori_loop(0, 16, group1, 0, unroll=TILE_UNROLL)

    def group2(r, carry):
        prev = pl.ds(r, blk, stride=SLAB_PITCH)
        cur = pl.ds(groups * SLAB_PITCH + r, blk, stride=SLAB_PITCH)
        q = q2s[prev, :].astype(BF16)
        k = jnp.concatenate([ks[prev, :], ks[cur, :]], axis=0).astype(BF16)
        v = jnp.concatenate([vs[prev, :], vs[cur, :]], axis=0).astype(BF16)
        o, l = attend(q, k, v, bias[jnp.where(first_window, 1, 0)])
        o2s[prev, :] = o
        l2s[prev, :] = l
        return carry

    lax.fori_loop(0, SLAB_GROUP, group2, 0, unroll=TILE_UNROLL)

    def merge(i, carry):
        r0 = pl.multiple_of(i * blk, blk)
        rows = pl.ds(r0, blk)
        pieces = [pl.ds(pl.multiple_of((i * (blk // SLAB_GROUP) + j) * SLAB_PITCH, 8), SLAB_GROUP)
                  for j in range(blk // SLAB_GROUP)]
        gather = lambda ref: jnp.concatenate([ref[p, :] for p in pieces], axis=0)
        l0, l1, l2 = l0s[rows, :], gather(l1s), gather(l2s)
        m = jnp.maximum(jnp.maximum(l0, l1), l2)
        w0, w1, w2 = jnp.exp2(l0 - m), jnp.exp2(l1 - m), jnp.exp2(l2 - m)
        o = (w0 * o0s[rows, :] + w1 * gather(o1s) + w2 * gather(o2s)) * (1.0 / (w0 + w1 + w2))
        out_ref[rows, :] = (o * _silu(g_ref[rows, :].astype(F32))).astype(out_ref.dtype)
        return carry

    lax.fori_loop(0, w // blk, merge, 0)


def _dilated_attention(h_main, batch, seq):
    n = batch * seq
    w = DIL_WINDOW
    nt = seq // w
    slab_rows = (w // SLAB_GROUP) * SLAB_PITCH

    def cur(col):
        return pl.BlockSpec((w, HEAD_DIM), lambda b, t, h: (b * nt + t, col // HEAD_DIM + h))

    def prev(col):
        return pl.BlockSpec((w, HEAD_DIM), lambda b, t, h: (b * nt + jnp.maximum(t - 1, 0), col // HEAD_DIM + h))

    width = DIL_HEADS * HEAD_DIM
    slab = lambda rows: pltpu.VMEM((rows, LANES), F32)
    return pl.pallas_call(
        _dil_kernel,
        grid=(batch, nt, DIL_HEADS),
        in_specs=[cur(L0_DQ), cur(L0_DQ + width), cur(L0_DQ + 2 * width),
                  cur(L0_DK), prev(L0_DK), cur(L0_DV), prev(L0_DV), cur(L0_DG)],
        out_specs=pl.BlockSpec((w, HEAD_DIM), lambda b, t, h: (b * nt + t, h)),
        out_shape=jax.ShapeDtypeStruct((n, width), BF16),
        scratch_shapes=[
            pltpu.VMEM((w + DIL_BLOCK, HEAD_DIM), BF16), pltpu.VMEM((w + DIL_BLOCK, HEAD_DIM), BF16),
            slab(2 * slab_rows), slab(2 * slab_rows), slab(slab_rows), slab(slab_rows),
            slab(w), slab(w), slab(slab_rows), slab(slab_rows), slab(slab_rows), slab(slab_rows),
            pltpu.VMEM((4, DIL_BLOCK, 2 * DIL_BLOCK), F32),
        ],
        compiler_params=_cparams(("arbitrary", "arbitrary", "arbitrary")),
        name="dilated_attn",
    )(*([h_main] * 8))


def _mem_kv_kernel(m_ref, w_ref, o_ref):
    o_ref[...] = _nn(m_ref[...].astype(BF16), w_ref[...]).astype(o_ref.dtype)


def _mem_kv(mem2d, wk, wv):
    rows, kdim = mem2d.shape
    w = jnp.concatenate([wk, wv], axis=1).astype(BF16)
    tm = MEM_LEN
    return pl.pallas_call(
        _mem_kv_kernel,
        grid=(rows // tm,),
        in_specs=[pl.BlockSpec((tm, kdim), lambda i: (i, 0)), pl.BlockSpec(w.shape, lambda i: (0, 0))],
        out_specs=pl.BlockSpec((tm, w.shape[1]), lambda i: (i, 0)),
        out_shape=jax.ShapeDtypeStruct((rows, w.shape[1]), BF16),
        compiler_params=_cparams(("arbitrary",)),
        name="mem_kv",
    )(mem2d, w)


def _mem_attn_kernel(q_ref, g_ref, k_ref, v_ref, o_ref, *, tq):
    sub = 128
    for i in range(tq // sub):
        rows = slice(i * sub, (i + 1) * sub)
        for h in range(MEM_HEADS):
            hs = slice(h * HEAD_DIM, (h + 1) * HEAD_DIM)
            s = _nt(q_ref[rows, hs], k_ref[:, hs]) * (HEAD_DIM ** -0.5)
            m = jnp.max(s, -1, keepdims=True)
            p = jnp.exp(s - m)
            den = jnp.sum(p, -1, keepdims=True)
            o = _nn(p.astype(BF16), v_ref[:, hs]) * (1.0 / den)
            o_ref[rows, hs] = (o * _silu(g_ref[rows, hs].astype(F32))).astype(o_ref.dtype)


def _mem_attn(h_main, kv, batch, seq, q_col, g_col, tq=512):
    n = batch * seq
    width = MEM_HEADS * HEAD_DIM
    nt = seq // tq
    return pl.pallas_call(
        functools.partial(_mem_attn_kernel, tq=tq),
        grid=(batch, nt),
        in_specs=[
            pl.BlockSpec((tq, width), lambda b, t: (b * nt + t, q_col // width)),
            pl.BlockSpec((tq, width), lambda b, t: (b * nt + t, g_col // width)),
            pl.BlockSpec((MEM_LEN, width), lambda b, t: (b, 0)),
            pl.BlockSpec((MEM_LEN, width), lambda b, t: (b, 1)),
        ],
        out_specs=pl.BlockSpec((tq, width), lambda b, t: (b * nt + t, 0)),
        out_shape=jax.ShapeDtypeStruct((n, width), BF16),
        compiler_params=_cparams(("arbitrary", "arbitrary")),
        name="mem_attn",
    )(h_main, h_main, kv, kv)


def _out_kernel(*refs, n_parts):
    a_refs = refs[:n_parts]
    w_refs = refs[n_parts:2 * n_parts]
    x_ref, lw_ref, lb_ref, o_ref = refs[2 * n_parts:]
    acc = DEEPNORM_ALPHA * x_ref[...]
    for a_ref, w_ref in zip(a_refs, w_refs):
        acc = acc + _nn(a_ref[...], w_ref[...])
    mu = jnp.mean(acc, -1, keepdims=True)
    d = acc - mu
    var = jnp.mean(d * d, -1, keepdims=True)
    o_ref[...] = d * lax.rsqrt(var + LN_EPS) * lw_ref[...] + lb_ref[...]


def _out_proj_ln(parts, w_out, x, ln_w, ln_b, tm=512):
    n, dm = x.shape
    w_bf = w_out.astype(BF16)
    ws, off = [], 0
    for p in parts:
        ws.append(w_bf[off:off + p.shape[1]])
        off += p.shape[1]
    row = lambda width: pl.BlockSpec((tm, width), lambda i: (i, 0))
    const = lambda shape: pl.BlockSpec(shape, lambda i: (0, 0))
    return pl.pallas_call(
        functools.partial(_out_kernel, n_parts=len(parts)),
        grid=(n // tm,),
        in_specs=[row(p.shape[1]) for p in parts] + [const(w.shape) for w in ws]
        + [row(dm), const((1, dm)), const((1, dm))],
        out_specs=row(dm),
        out_shape=jax.ShapeDtypeStruct((n, dm), F32),
        compiler_params=_cparams(("arbitrary",)),
        name="out_proj_ln",
    )(*parts, *ws, x, ln_w.reshape(1, dm), ln_b.reshape(1, dm))


def _ssd_kernel(z_ref, xs_ref, bm_ref, cm_ref, dt_ref, cw_ref, cb_ref, dtb_ref, a_ref, dsk_ref, nw_ref,
                o_ref, u_ref, st_ref, *, n_chunks):
    q_ = SSD_CHUNK
    halo = 8

    @pl.when(pl.program_id(1) == 0)
    def _():
        st_ref[...] = jnp.zeros_like(st_ref)
        u_ref[0:halo, :] = jnp.zeros((halo, SSD_CONV_DIM), F32)

    row = lax.broadcasted_iota(jnp.int32, (q_, q_), 0)
    col = lax.broadcasted_iota(jnp.int32, (q_, q_), 1)
    causal = col <= row
    tril = causal.astype(F32)
    lane = lax.broadcasted_iota(jnp.int32, (q_, LANES), 1)
    lo_half = lane < SSD_HEAD_DIM
    lane_row = lax.broadcasted_iota(jnp.int32, (1, LANES), 1)
    lo_half_row = lane_row < SSD_HEAD_DIM
    a_neg = -jnp.exp(a_ref[...])

    def pair_expand(t, h1):
        return jnp.where(lo_half, t[:, h1:h1 + 1], t[:, h1 + 1:h1 + 2])

    def chunk(ci, carry):
        r0 = pl.multiple_of(ci * q_, q_)
        rows = pl.ds(r0, q_)
        bc_w = SSD_GROUPS * SSD_STATE
        u_ref[halo:halo + q_, 0:SSD_INNER] = xs_ref[rows, :].astype(F32)
        u_ref[halo:halo + q_, SSD_INNER:SSD_INNER + bc_w] = bm_ref[rows, :].astype(F32)
        u_ref[halo:halo + q_, SSD_INNER + bc_w:SSD_CONV_DIM] = cm_ref[rows, :].astype(F32)
        conv = cb_ref[...]
        for k in range(SSD_CONV):
            s0 = halo - (SSD_CONV - 1) + k
            conv = conv + cw_ref[k:k + 1, :] * u_ref[s0:s0 + q_, :]
        u_ref[0:halo, :] = u_ref[q_:q_ + halo, :]
        u = _silu(conv)
        dt_in = dt_ref[rows, :] + dtb_ref[...]
        dt = jnp.maximum(dt_in, 0.0) + jnp.log1p(jnp.exp(-jnp.abs(dt_in)))
        adt = dt * a_neg
        acum = jnp.dot(tril, adt, preferred_element_type=F32, precision=lax.Precision.HIGHEST)
        acum_t = acum.T
        dt_t = dt.T
        last = acum[q_ - 1:q_, :]
        e_acum = jnp.exp(acum)
        w_state = dt * jnp.exp(last - acum)
        c_decay = jnp.exp(last)
        for g in range(SSD_GROUPS):
            gs = slice(g * SSD_GROUP_WIDTH, (g + 1) * SSD_GROUP_WIDTH)
            b_f = u[:, SSD_INNER + g * SSD_STATE:SSD_INNER + (g + 1) * SSD_STATE]
            c_f = u[:, SSD_INNER + (SSD_GROUPS + g) * SSD_STATE:SSD_INNER + (SSD_GROUPS + g + 1) * SSD_STATE]
            b_bf = b_f.astype(BF16)
            c_bf = c_f.astype(BF16)
            cb = _nt(c_bf, b_bf)
            bt_bf = b_f.T.astype(BF16)
            st = st_ref[g]
            y_off_all = _nn(c_bf, st.astype(BF16))
            ys, xdecs, decs = [], [], []
            ss = jnp.zeros((q_, 1), F32)
            for j in range(SSD_GROUP_WIDTH // LANES):
                h1 = (g * SSD_GROUP_WIDTH + j * LANES) // SSD_HEAD_DIM
                ps = slice(g * SSD_GROUP_WIDTH + j * LANES, g * SSD_GROUP_WIDTH + (j + 1) * LANES)
                xs = u[:, ps]
                mats = []
                for hh in (h1, h1 + 1):
                    diff = acum[:, hh:hh + 1] - acum_t[hh:hh + 1, :]
                    lm = jnp.exp(jnp.where(causal, diff, -jnp.inf))
                    mats.append((cb * lm * dt_t[hh:hh + 1, :]).astype(BF16))
                lhs = jnp.concatenate(mats, axis=1)
                rhs = jnp.concatenate([jnp.where(lo_half, xs, 0.0), jnp.where(lo_half, 0.0, xs)],
                                      axis=0).astype(BF16)
                y = _nn(lhs, rhs)
                y = y + y_off_all[:, j * LANES:(j + 1) * LANES] * pair_expand(e_acum, h1)
                y = y + dsk_ref[:, ps] * xs
                y = y * _silu(z_ref[rows, ps].astype(F32))
                ss = ss + jnp.sum(y * y, -1, keepdims=True)
                ys.append(y)
                xdecs.append((xs * pair_expand(w_state, h1)).astype(BF16))
                decs.append(jnp.where(lo_half_row, c_decay[:, h1:h1 + 1], c_decay[:, h1 + 1:h1 + 2]))
            st_ref[g] = st * jnp.concatenate(decs, axis=1) + _nn(bt_bf, jnp.concatenate(xdecs, axis=1))
            inv = lax.rsqrt(ss * (1.0 / SSD_GROUP_WIDTH) + RMS_EPS)
            for j, y in enumerate(ys):
                ps = slice(g * SSD_GROUP_WIDTH + j * LANES, g * SSD_GROUP_WIDTH + (j + 1) * LANES)
                o_ref[rows, ps] = (y * inv * nw_ref[:, ps]).astype(o_ref.dtype)
        return carry

    lax.fori_loop(0, n_chunks, chunk, 0)


def _ssd(h_main, h_small, conv_w, conv_b, dt_bias, a_log, d_skip, norm_w, batch, seq, tb=512):
    n = batch * seq
    nt = seq // tb
    bc_w = SSD_GROUPS * SSD_STATE
    pad = lambda v: jnp.zeros((1, LANES), F32).at[0, :SSD_HEADS].set(v.astype(F32))
    idx = lambda c: (lambda b, t: (b * nt + t, c))
    const = lambda b, t: (0, 0)
    return pl.pallas_call(
        functools.partial(_ssd_kernel, n_chunks=tb // SSD_CHUNK),
        grid=(batch, nt),
        in_specs=[
            pl.BlockSpec((tb, SSD_INNER), idx(L1_Z // SSD_INNER)),
            pl.BlockSpec((tb, SSD_INNER), idx(L1_XBC // SSD_INNER)),
            pl.BlockSpec((tb, bc_w), idx((L1_XBC + SSD_INNER) // bc_w)),
            pl.BlockSpec((tb, bc_w), idx((L1_XBC + SSD_INNER + bc_w) // bc_w)),
            pl.BlockSpec((tb, LANES), idx(0)),
            pl.BlockSpec((SSD_CONV, SSD_CONV_DIM), const),
            pl.BlockSpec((1, SSD_CONV_DIM), const),
            pl.BlockSpec((1, LANES), const),
            pl.BlockSpec((1, LANES), const),
            pl.BlockSpec((1, SSD_INNER), const),
            pl.BlockSpec((1, SSD_INNER), const),
        ],
        out_specs=pl.BlockSpec((tb, SSD_INNER), idx(0)),
        out_shape=jax.ShapeDtypeStruct((n, SSD_INNER), BF16),
        scratch_shapes=[
            pltpu.VMEM((SSD_CHUNK + 8, SSD_CONV_DIM), F32),
            pltpu.VMEM((SSD_GROUPS, SSD_STATE, SSD_GROUP_WIDTH), F32),
        ],
        compiler_params=_cparams(("arbitrary", "arbitrary")),
        name="ssd",
    )(h_main, h_main, h_main, h_main, h_small, conv_w, conv_b.reshape(1, -1),
      pad(dt_bias), pad(a_log), jnp.repeat(d_skip.astype(F32), SSD_HEAD_DIM).reshape(1, -1),
      norm_w.reshape(1, -1))


def _pad_cols(w, width=LANES):
    return jnp.zeros((w.shape[0], width), BF16).at[:, :w.shape[1]].set(w.astype(BF16))


def kernel(x, mem, positions, l0_w_in, l0_gla_w_gate, l0_gla_b_gate, l0_gla_norm_w, l0_mem_wk, l0_mem_wv, l0_w_out, l0_ln_w, l0_ln_b, l1_w_in, l1_conv_w, l1_conv_b, l1_dt_bias, l1_a_log, l1_d_skip, l1_ssd_norm_w, l1_mem_wk, l1_mem_wv, l1_w_out, l1_ln_w, l1_ln_b):
    batch, seq, dm = x.shape
    n = batch * seq
    x2 = x.reshape(n, dm)
    mem2 = mem.reshape(batch * MEM_LEN, dm)

    ga0 = L0_DQ + GLA_GATE_RANK
    w0_main = jnp.concatenate([l0_w_in[:, :L0_DQ], l0_w_in[:, ga0:]], axis=1).astype(BF16)
    w0_small = _pad_cols(l0_w_in[:, L0_DQ:ga0])
    cos, sin = _rope_tables(positions)
    col_scale = jnp.ones((1, L0_MAIN), F32).at[0, L0_DQ:L0_DK].set(HEAD_DIM ** -0.5 * LOG2_E)
    h0, h0s = _project(x2, w0_main, w0_small, rope=(cos, sin, col_scale, L0_DQ, L0_DV))
    o_a = _gla(h0, h0s, l0_gla_w_gate, l0_gla_b_gate, l0_gla_norm_w, batch, seq)
    o_b = _dilated_attention(h0, batch, seq)
    kv0 = _mem_kv(mem2, l0_mem_wk, l0_mem_wv)
    o_m = _mem_attn(h0, kv0, batch, seq, L0_MQ, L0_MG)
    x1 = _out_proj_ln([o_a, o_b, o_m], l0_w_out, x2, l0_ln_w, l0_ln_b)

    dt0 = SSD_INNER + SSD_CONV_DIM
    w1_main = jnp.concatenate([l1_w_in[:, :dt0], l1_w_in[:, dt0 + SSD_HEADS:]], axis=1).astype(BF16)
    w1_small = _pad_cols(l1_w_in[:, dt0:dt0 + SSD_HEADS])
    h1, h1s = _project(x1, w1_main, w1_small)
    y = _ssd(h1, h1s, l1_conv_w, l1_conv_b, l1_dt_bias, l1_a_log, l1_d_skip, l1_ssd_norm_w, batch, seq)
    kv1 = _mem_kv(mem2, l1_mem_wk, l1_mem_wv)
    o_m1 = _mem_attn(h1, kv1, batch, seq, L1_MQ, L1_MG)
    x2_out = _out_proj_ln([y, o_m1], l1_w_out, x1, l1_ln_w, l1_ln_b)
    return x2_out.reshape(batch, seq, dm)
```

```python
import functools

import jax
import jax.numpy as jnp
from jax import lax
from jax.experimental import pallas as pl
from jax.experimental.pallas import tpu as pltpu

F32 = jnp.float32
BF16 = jnp.bfloat16

DEPTH = 2
DEEPNORM_ALPHA = (2 * DEPTH) ** 0.25
LN_EPS = 1e-5
RMS_EPS = 1e-6
ROPE_THETA = 10000.0
MEM_LEN = 256

GLA_HEADS = 4
GLA_DK = 128
GLA_DV = 256
GLA_GATE_RANK = 16
GLA_GATE_NORMALIZER = 16.0
GLA_CHUNK = 64

DIL_PATTERNS = ((128, 1), (512, 4), (2048, 16))
DIL_HEADS = 4
HEAD_DIM = 128
DIL_BLOCK = 128

MEM_HEADS = 4

SSD_HEADS = 24
SSD_HEAD_DIM = 64
SSD_GROUPS = 4
SSD_STATE = 128
SSD_CONV = 4
SSD_CHUNK = 128
SSD_INNER = SSD_HEADS * SSD_HEAD_DIM
SSD_GROUP_WIDTH = SSD_INNER // SSD_GROUPS
SSD_CONV_DIM = SSD_INNER + 2 * SSD_GROUPS * SSD_STATE
CONV_COLS = 256

LANES = 128
LOG2_E = 1.4426950408889634
VMEM_LIMIT = 48 * 1024 * 1024

L0_GQ, L0_GK, L0_GV, L0_GG = 0, 512, 1024, 2048
L0_DQ, L0_DK, L0_DV, L0_DG = 3072, 4608, 5120, 5632
L0_MQ, L0_MG = 6144, 6656
L0_MAIN = 7168
L1_Z, L1_XBC, L1_MQ, L1_MG = 0, 1536, 4096, 4608
L1_MAIN = 5120


def _nt(a, b):
    return lax.dot_general(a, b, (((1,), (1,)), ((), ())), preferred_element_type=F32)


def _tn(a, b):
    return lax.dot_general(a, b, (((0,), (0,)), ((), ())), preferred_element_type=F32)


def _nn(a, b):
    return jnp.dot(a, b, preferred_element_type=F32)


def _silu(x):
    return 0.5 * x * (1.0 + jnp.tanh(0.5 * x))


def _cparams(sem):
    return pltpu.CompilerParams(dimension_semantics=sem, vmem_limit_bytes=VMEM_LIMIT)


def _rope_table_kernel(pos_ref, inv_ref, sign_ref, cos_ref, sin_ref):
    ang = pos_ref[...].astype(F32) * inv_ref[...]
    cos_ref[...] = jnp.cos(ang)
    sin_ref[...] = jnp.sin(ang) * sign_ref[...]


def _rope_tables(positions):
    n = positions.size
    half = HEAD_DIM // 2
    inv = 1.0 / (ROPE_THETA ** (jnp.arange(0, HEAD_DIM, 2, dtype=F32) / HEAD_DIM))
    inv2 = jnp.concatenate([inv, inv]).reshape(1, HEAD_DIM)
    sign = jnp.concatenate([-jnp.ones((half,), F32), jnp.ones((half,), F32)]).reshape(1, HEAD_DIM)
    pos_b = jnp.broadcast_to(positions.reshape(n, 1), (n, HEAD_DIM))
    tb = 2048
    row = pl.BlockSpec((tb, HEAD_DIM), lambda i: (i, 0))
    vec = pl.BlockSpec((1, HEAD_DIM), lambda i: (0, 0))
    return pl.pallas_call(
        _rope_table_kernel,
        grid=(n // tb,),
        in_specs=[row, vec, vec],
        out_specs=[row, row],
        out_shape=[jax.ShapeDtypeStruct((n, HEAD_DIM), F32)] * 2,
        compiler_params=_cparams(("arbitrary",)),
        name="rope_tables",
    )(pos_b, inv2, sign)


def _proj_kernel(*refs, rope_tiles):
    if rope_tiles is None:
        x_ref, w_ref, ws_ref, o_ref, os_ref, xb_ref = refs
    else:
        x_ref, w_ref, ws_ref, c_ref, s_ref, sc_ref, o_ref, os_ref, xb_ref = refs
    j = pl.program_id(1)

    @pl.when(j == 0)
    def _():
        xb = x_ref[...].astype(BF16)
        xb_ref[...] = xb
        os_ref[...] = _nn(xb, ws_ref[...])

    acc = _nn(xb_ref[...], w_ref[...])
    if rope_tiles is None:
        o_ref[...] = acc.astype(o_ref.dtype)
        return
    lo, hi = rope_tiles
    is_rope = jnp.logical_and(j >= lo, j < hi)

    @pl.when(is_rope)
    def _():
        cos = c_ref[...]
        sin = s_ref[...]
        for k in range(acc.shape[1] // HEAD_DIM):
            sl = slice(k * HEAD_DIM, (k + 1) * HEAD_DIM)
            a = acc[:, sl] * sc_ref[:, sl]
            o_ref[:, sl] = (a * cos + pltpu.roll(a, HEAD_DIM // 2, 1) * sin).astype(o_ref.dtype)

    @pl.when(jnp.logical_not(is_rope))
    def _():
        o_ref[...] = acc.astype(o_ref.dtype)


def _project(x, w_main, w_small, rope=None, tm=1024, tn=1024):
    n, kdim = x.shape
    m = w_main.shape[1]
    in_specs = [
        pl.BlockSpec((tm, kdim), lambda i, j: (i, 0)),
        pl.BlockSpec((kdim, tn), lambda i, j: (0, j)),
        pl.BlockSpec((kdim, LANES), lambda i, j: (0, 0)),
    ]
    args = [x, w_main, w_small]
    rope_tiles = None
    if rope is not None:
        cos, sin, col_scale, first_col, last_col = rope
        rope_tiles = (first_col // tn, last_col // tn)
        in_specs += [
            pl.BlockSpec((tm, HEAD_DIM), lambda i, j: (i, 0)),
            pl.BlockSpec((tm, HEAD_DIM), lambda i, j: (i, 0)),
            pl.BlockSpec((1, tn), lambda i, j: (0, j)),
        ]
        args += [cos, sin, col_scale]
    return pl.pallas_call(
        functools.partial(_proj_kernel, rope_tiles=rope_tiles),
        grid=(n // tm, m // tn),
        in_specs=in_specs,
        out_specs=[
            pl.BlockSpec((tm, tn), lambda i, j: (i, j)),
            pl.BlockSpec((tm, LANES), lambda i, j: (i, 0)),
        ],
        out_shape=[jax.ShapeDtypeStruct((n, m), BF16), jax.ShapeDtypeStruct((n, LANES), F32)],
        scratch_shapes=[pltpu.VMEM((tm, kdim), BF16)],
        compiler_params=_cparams(("arbitrary", "arbitrary")),
        name="in_proj",
    )(*args)


def _gla_kernel(q_ref, k_ref, v_ref, g_ref, ga_ref, wg_ref, bg_ref, nw_ref, o_ref, st_ref, *, n_chunks):
    c_ = GLA_CHUNK

    @pl.when(pl.program_id(1) == 0)
    def _():
        st_ref[...] = jnp.zeros_like(st_ref)

    row = lax.broadcasted_iota(jnp.int32, (c_, c_), 0)
    col = lax.broadcasted_iota(jnp.int32, (c_, c_), 1)
    causal = col <= row
    tril = causal.astype(F32)
    wg = wg_ref[...]
    bg = bg_ref[...]
    nw = nw_ref[...]

    def chunk(ci, carry):
        r0 = pl.multiple_of(ci * c_, c_)
        rows = pl.ds(r0, c_)
        z = _nn(ga_ref[rows, :].astype(BF16), wg) + bg
        log_a = (jnp.minimum(z, 0.0) - jnp.log1p(jnp.exp(-jnp.abs(z)))) * (1.0 / GLA_GATE_NORMALIZER)
        bcum = jnp.dot(tril, log_a, preferred_element_type=F32, precision=lax.Precision.HIGHEST)
        blast = bcum[c_ - 1:c_, :]
        e_pos = jnp.exp(bcum)
        e_neg = jnp.exp(-bcum)
        e_tail = jnp.exp(blast - bcum)
        e_last = jnp.exp(blast)
        for h in range(GLA_HEADS):
            ks = slice(h * GLA_DK, (h + 1) * GLA_DK)
            vs = slice(h * GLA_DV, (h + 1) * GLA_DV)
            q = q_ref[rows, ks].astype(F32) * (GLA_DK ** -0.5)
            k = k_ref[rows, ks].astype(F32)
            q_dec = (q * e_pos[:, ks]).astype(BF16)
            k_inv = (k * e_neg[:, ks]).astype(BF16)
            k_tail = (k * e_tail[:, ks]).astype(BF16)
            v = v_ref[rows, vs]
            scores = jnp.where(causal, _nt(q_dec, k_inv), 0.0)
            st = st_ref[h]
            o = _nn(scores.astype(BF16), v) + _nt(q_dec, st.astype(BF16))
            st_ref[h] = st * e_last[:, ks] + _tn(v, k_tail)
            ms = jnp.mean(o * o, axis=-1, keepdims=True)
            o = o * lax.rsqrt(ms + RMS_EPS) * nw
            o_ref[rows, vs] = (o * _silu(g_ref[rows, vs].astype(F32))).astype(o_ref.dtype)
        return carry

    lax.fori_loop(0, n_chunks, chunk, 0, unroll=2)


def _gla(h_main, h_small, w_gate, b_gate, norm_w, batch, seq, tb=512):
    n = batch * seq
    nt = seq // tb
    wg = jnp.zeros((LANES, GLA_HEADS * GLA_DK), BF16).at[:GLA_GATE_RANK].set(w_gate.astype(BF16))
    idx = lambda c: (lambda b, t: (b * nt + t, c))
    const = lambda b, t: (0, 0)
    return pl.pallas_call(
        functools.partial(_gla_kernel, n_chunks=tb // GLA_CHUNK),
        grid=(batch, nt),
        in_specs=[
            pl.BlockSpec((tb, 512), idx(L0_GQ // 512)),
            pl.BlockSpec((tb, 512), idx(L0_GK // 512)),
            pl.BlockSpec((tb, 1024), idx(L0_GV // 1024)),
            pl.BlockSpec((tb, 1024), idx(L0_GG // 1024)),
            pl.BlockSpec((tb, LANES), idx(0)),
            pl.BlockSpec((LANES, 512), const),
            pl.BlockSpec((1, 512), const),
            pl.BlockSpec((1, GLA_DV), const),
        ],
        out_specs=pl.BlockSpec((tb, 1024), idx(0)),
        out_shape=jax.ShapeDtypeStruct((n, GLA_HEADS * GLA_DV), BF16),
        scratch_shapes=[pltpu.VMEM((GLA_HEADS, GLA_DV, GLA_DK), F32)],
        compiler_params=_cparams(("arbitrary", "arbitrary")),
        name="gla",
    )(h_main, h_main, h_main, h_main, h_small, wg, b_gate.reshape(1, -1), norm_w.reshape(1, -1))


DIL_WINDOW = 2048
SLAB_GROUP = 16
SLAB_PITCH = 24
SEG = 32
TILE_UNROLL = 8


def _dil_kernel(q0_ref, q1_ref, q2_ref, kc_ref, kp_ref, vc_ref, vp_ref, g_ref, out_ref,
                kcat, vcat, ks, vs, q1s, q2s, o0s, l0s, o1s, l1s, o2s, l2s, bias):
    blk = DIL_BLOCK
    w = DIL_WINDOW
    groups = w // SLAB_GROUP
    t = pl.program_id(1)
    first_window = t == 0

    rho = lax.broadcasted_iota(jnp.int32, (blk, 2 * blk), 0)
    kap = lax.broadcasted_iota(jnp.int32, (blk, 2 * blk), 1)
    in_prev = kap < blk
    kap_l = jnp.where(in_prev, kap, kap - blk)
    perm = lambda x: 4 * (x % SEG) + x // SEG
    neg = jnp.float32(-jnp.inf)
    for slot, (jq, jk) in enumerate(((rho, kap_l), (perm(rho), perm(kap_l)))):
        ok_prev = jnp.where(jk >= jq, 0.0, neg)
        ok_cur = jnp.where(jk <= jq, 0.0, neg)
        bias[2 * slot] = jnp.where(in_prev, ok_prev, ok_cur)
        bias[2 * slot + 1] = jnp.where(in_prev, neg, ok_cur)

    kcat[0:blk, :] = kp_ref[w - blk:w, :]
    kcat[blk:blk + w, :] = kc_ref[...]
    vcat[0:blk, :] = vp_ref[w - blk:w, :]
    vcat[blk:blk + w, :] = vc_ref[...]

    def to_slabs(i, carry):
        src = pl.ds(pl.multiple_of(i * SLAB_GROUP, SLAB_GROUP), SLAB_GROUP)
        prev = pl.ds(pl.multiple_of(i * SLAB_PITCH, 8), SLAB_GROUP)
        cur = pl.ds(pl.multiple_of((groups + i) * SLAB_PITCH, 8), SLAB_GROUP)
        ks[prev, :] = kp_ref[src, :].astype(F32)
        ks[cur, :] = kc_ref[src, :].astype(F32)
        vs[prev, :] = vp_ref[src, :].astype(F32)
        vs[cur, :] = vc_ref[src, :].astype(F32)
        q1s[prev, :] = q1_ref[src, :].astype(F32)
        q2s[prev, :] = q2_ref[src, :].astype(F32)
        return carry

    lax.fori_loop(0, groups, to_slabs, 0, unroll=4)

    def attend(q, k, v, b):
        s = _nt(q, k) + b
        m = jnp.max(s, -1, keepdims=True)
        p = jnp.exp2(s - m)
        den = jnp.sum(p, -1, keepdims=True)
        o = _nn(p.astype(BF16), v) * (1.0 / den)
        return o, jnp.broadcast_to(m + jnp.log2(den), (blk, LANES))

    def group0(i, carry):
        r0 = pl.multiple_of(i * blk, blk)
        sel = jnp.where(jnp.logical_and(first_window, i == 0), 1, 0)
        o, l = attend(q0_ref[pl.ds(r0, blk), :], kcat[pl.ds(r0, 2 * blk), :], vcat[pl.ds(r0, 2 * blk), :],
                      bias[sel])
        o0s[pl.ds(r0, blk), :] = o
        l0s[pl.ds(r0, blk), :] = l
        return carry

    lax.fori_loop(0, w // blk, group0, 0, unroll=TILE_UNROLL)

    def seg(ref, group, b):
        return ref[pl.ds(group * SLAB_PITCH + b, SEG, stride=SLAB_PITCH), :]

    def group1(idx, carry):
        r = idx // 4
        n = idx % 4
        g_cur = groups + SEG * n
        q = jnp.concatenate([seg(q1s, SEG * n, r + 4 * j) for j in range(4)], axis=0).astype(BF16)
        k = jnp.concatenate([seg(ks, g_cur - SEG, r + 4 * j) for j in range(4)]
                            + [seg(ks, g_cur, r + 4 * j) for j in range(4)], axis=0).astype(BF16)
        v = jnp.concatenate([seg(vs, g_cur - SEG, r + 4 * j) for j in range(4)]
                            + [seg(vs, g_cur, r + 4 * j) for j in range(4)], axis=0).astype(BF16)
        sel = jnp.where(jnp.logical_and(first_window, n == 0), 3, 2)
        o, l = attend(q, k, v, bias[sel])
        for j in range(4):
            dst = pl.ds(SEG * n * SLAB_PITCH + r + 4 * j, SEG, stride=SLAB_PITCH)
            o1s[dst, :] = o[j * SEG:(j + 1) * SEG]
            l1s[dst, :] = l[j * SEG:(j + 1) * SEG]
        return carry

    lax.fori_loop(0, 16, group1, 0, unroll=TILE_UNROLL)

    def group2(r, carry):
        prev = pl.ds(r, blk, stride=SLAB_PITCH)
        cur = pl.ds(groups * SLAB_PITCH + r, blk, stride=SLAB_PITCH)
        q = q2s[prev, :].astype(BF16)
        k = jnp.concatenate([ks[prev, :], ks[cur, :]], axis=0).astype(BF16)
        v = jnp.concatenate([vs[prev, :], vs[cur, :]], axis=0).astype(BF16)
        o, l = attend(q, k, v, bias[jnp.where(first_window, 1, 0)])
        o2s[prev, :] = o
        l2s[prev, :] = l
        return carry

    lax.fori_loop(0, SLAB_GROUP, group2, 0, unroll=TILE_UNROLL)

    def merge(i, carry):
        r0 = pl.multiple_of(i * blk, blk)
        rows = pl.ds(r0, blk)
        pieces = [pl.ds(pl.multiple_of((i * (blk // SLAB_GROUP) + j) * SLAB_PITCH, 8), SLAB_GROUP)
                  for j in range(blk // SLAB_GROUP)]
        gather = lambda ref: jnp.concatenate([ref[p, :] for p in pieces], axis=0)
        l0, l1, l2 = l0s[rows, :], gather(l1s), gather(l2s)
        m = jnp.maximum(jnp.maximum(l0, l1), l2)
        w0, w1, w2 = jnp.exp2(l0 - m), jnp.exp2(l1 - m), jnp.exp2(l2 - m)
        o = (w0 * o0s[rows, :] + w1 * gather(o1s) + w2 * gather(o2s)) * (1.0 / (w0 + w1 + w2))
        out_ref[rows, :] = (o * _silu(g_ref[rows, :].astype(F32))).astype(out_ref.dtype)
        return carry

    lax.fori_loop(0, w // blk, merge, 0)


def _dilated_attention(h_main, batch, seq):
    n = batch * seq
    w = DIL_WINDOW
    nt = seq // w
    slab_rows = (w // SLAB_GROUP) * SLAB_PITCH

    def cur(col):
        return pl.BlockSpec((w, HEAD_DIM), lambda b, t, h: (b * nt + t, col // HEAD_DIM + h))

    def prev(col):
        return pl.BlockSpec((w, HEAD_DIM), lambda b, t, h: (b * nt + jnp.maximum(t - 1, 0), col // HEAD_DIM + h))

    width = DIL_HEADS * HEAD_DIM
    slab = lambda rows: pltpu.VMEM((rows, LANES), F32)
    return pl.pallas_call(
        _dil_kernel,
        grid=(batch, nt, DIL_HEADS),
        in_specs=[cur(L0_DQ), cur(L0_DQ + width), cur(L0_DQ + 2 * width),
                  cur(L0_DK), prev(L0_DK), cur(L0_DV), prev(L0_DV), cur(L0_DG)],
        out_specs=pl.BlockSpec((w, HEAD_DIM), lambda b, t, h: (b * nt + t, h)),
        out_shape=jax.ShapeDtypeStruct((n, width), BF16),
        scratch_shapes=[
            pltpu.VMEM((w + DIL_BLOCK, HEAD_DIM), BF16), pltpu.VMEM((w + DIL_BLOCK, HEAD_DIM), BF16),
            slab(2 * slab_rows), slab(2 * slab_rows), slab(slab_rows), slab(slab_rows),
            slab(w), slab(w), slab(slab_rows), slab(slab_rows), slab(slab_rows), slab(slab_rows),
            pltpu.VMEM((4, DIL_BLOCK, 2 * DIL_BLOCK), F32),
        ],
        compiler_params=_cparams(("arbitrary", "arbitrary", "arbitrary")),
        name="dilated_attn",
    )(*([h_main] * 8))


def _mem_kv_kernel(m_ref, w_ref, o_ref):
    o_ref[...] = _nn(m_ref[...].astype(BF16), w_ref[...]).astype(o_ref.dtype)


def _mem_kv(mem2d, wk, wv):
    rows, kdim = mem2d.shape
    w = jnp.concatenate([wk, wv], axis=1).astype(BF16)
    tm = MEM_LEN
    return pl.pallas_call(
        _mem_kv_kernel,
        grid=(rows // tm,),
        in_specs=[pl.BlockSpec((tm, kdim), lambda i: (i, 0)), pl.BlockSpec(w.shape, lambda i: (0, 0))],
        out_specs=pl.BlockSpec((tm, w.shape[1]), lambda i: (i, 0)),
        out_shape=jax.ShapeDtypeStruct((rows, w.shape[1]), BF16),
        compiler_params=_cparams(("arbitrary",)),
        name="mem_kv",
    )(mem2d, w)


def _mem_attn_kernel(q_ref, g_ref, k_ref, v_ref, o_ref, *, tq):
    sub = 128
    for i in range(tq // sub):
        rows = slice(i * sub, (i + 1) * sub)
        for h in range(MEM_HEADS):
            hs = slice(h * HEAD_DIM, (h + 1) * HEAD_DIM)
            s = _nt(q_ref[rows, hs], k_ref[:, hs]) * (HEAD_DIM ** -0.5)
            m = jnp.max(s, -1, keepdims=True)
            p = jnp.exp(s - m)
            den = jnp.sum(p, -1, keepdims=True)
            o = _nn(p.astype(BF16), v_ref[:, hs]) * (1.0 / den)
            o_ref[rows, hs] = (o * _silu(g_ref[rows, hs].astype(F32))).astype(o_ref.dtype)


def _mem_attn(h_main, kv, batch, seq, q_col, g_col, tq=512):
    n = batch * seq
    width = MEM_HEADS * HEAD_DIM
    nt = seq // tq
    return pl.pallas_call(
        functools.partial(_mem_attn_kernel, tq=tq),
        grid=(batch, nt),
        in_specs=[
            pl.BlockSpec((tq, width), lambda b, t: (b * nt + t, q_col // width)),
            pl.BlockSpec((tq, width), lambda b, t: (b * nt + t, g_col // width)),
            pl.BlockSpec((MEM_LEN, width), lambda b, t: (b, 0)),
            pl.BlockSpec((MEM_LEN, width), lambda b, t: (b, 1)),
        ],
        out_specs=pl.BlockSpec((tq, width), lambda b, t: (b * nt + t, 0)),
        out_shape=jax.ShapeDtypeStruct((n, width), BF16),
        compiler_params=_cparams(("arbitrary", "arbitrary")),
        name="mem_attn",
    )(h_main, h_main, kv, kv)


def _out_kernel(*refs, n_parts):
    a_refs = refs[:n_parts]
    w_refs = refs[n_parts:2 * n_parts]
    x_ref, lw_ref, lb_ref, o_ref = refs[2 * n_parts:]
    acc = DEEPNORM_ALPHA * x_ref[...]
    for a_ref, w_ref in zip(a_refs, w_refs):
        acc = acc + _nn(a_ref[...], w_ref[...])
    mu = jnp.mean(acc, -1, keepdims=True)
    d = acc - mu
    var = jnp.mean(d * d, -1, keepdims=True)
    o_ref[...] = d * lax.rsqrt(var + LN_EPS) * lw_ref[...] + lb_ref[...]


def _out_proj_ln(parts, w_out, x, ln_w, ln_b, tm=512):
    n, dm = x.shape
    w_bf = w_out.astype(BF16)
    ws, off = [], 0
    for p in parts:
        ws.append(w_bf[off:off + p.shape[1]])
        off += p.shape[1]
    row = lambda width: pl.BlockSpec((tm, width), lambda i: (i, 0))
    const = lambda shape: pl.BlockSpec(shape, lambda i: (0, 0))
    return pl.pallas_call(
        functools.partial(_out_kernel, n_parts=len(parts)),
        grid=(n // tm,),
        in_specs=[row(p.shape[1]) for p in parts] + [const(w.shape) for w in ws]
        + [row(dm), const((1, dm)), const((1, dm))],
        out_specs=row(dm),
        out_shape=jax.ShapeDtypeStruct((n, dm), F32),
        compiler_params=_cparams(("arbitrary",)),
        name="out_proj_ln",
    )(*parts, *ws, x, ln_w.reshape(1, dm), ln_b.reshape(1, dm))


def _ssd_kernel(z_ref, xs_ref, bm_ref, cm_ref, dt_ref, cw_ref, cb_ref, dtb_ref, a_ref, dsk_ref, nw_ref,
                o_ref, u_ref, pair_ref, lag_ref, act_ref, st_ref, *, n_chunks):
    q_ = SSD_CHUNK
    halo = 8

    @pl.when(pl.program_id(1) == 0)
    def _():
        st_ref[...] = jnp.zeros_like(st_ref)
        u_ref[0:halo, :] = jnp.zeros((halo, SSD_CONV_DIM), F32)
        pair_ref[0:halo, :] = jnp.zeros((halo, SSD_CONV_DIM), F32)

    row = lax.broadcasted_iota(jnp.int32, (q_, q_), 0)
    col = lax.broadcasted_iota(jnp.int32, (q_, q_), 1)
    causal = col <= row
    tril = causal.astype(F32)
    lane = lax.broadcasted_iota(jnp.int32, (q_, LANES), 1)
    lo_half = lane < SSD_HEAD_DIM
    lane_row = lax.broadcasted_iota(jnp.int32, (1, LANES), 1)
    lo_half_row = lane_row < SSD_HEAD_DIM
    lo_half_f = lo_half.astype(F32)
    a_log2 = -jnp.exp(a_ref[...]) * LOG2_E

    def row_bcast(t, r):
        return jnp.tile(jnp.broadcast_to(t[r:r + 1, :], (8, q_)), (q_ // 8, 1))

    def chunk(ci, carry):
        r0 = pl.multiple_of(ci * q_, q_)
        rows = pl.ds(r0, q_)
        bc_w = SSD_GROUPS * SSD_STATE
        u_ref[halo:halo + q_, 0:SSD_INNER] = xs_ref[rows, :].astype(F32)
        u_ref[halo:halo + q_, SSD_INNER:SSD_INNER + bc_w] = bm_ref[rows, :].astype(F32)
        u_ref[halo:halo + q_, SSD_INNER + bc_w:SSD_CONV_DIM] = cm_ref[rows, :].astype(F32)
        for c0 in range(0, SSD_CONV_DIM, CONV_COLS):
            cs = slice(c0, c0 + CONV_COLS)
            u_cur = u_ref[halo:halo + q_, cs]
            lag_ref[...] = u_ref[halo - 1:halo - 1 + q_, cs]
            u_lag = lag_ref[...]
            pair_ref[halo:halo + q_, cs] = cw_ref[1:2, cs] * u_cur + cw_ref[0:1, cs] * u_lag
            conv = ((cb_ref[:, cs] + cw_ref[3:4, cs] * u_cur) + cw_ref[2:3, cs] * u_lag
                    + pair_ref[halo - 2:halo - 2 + q_, cs])
            act_ref[:, cs] = _silu(conv)
        u_ref[0:halo, :] = u_ref[q_:q_ + halo, :]
        pair_ref[0:halo, :] = pair_ref[q_:q_ + halo, :]
        dt_in = dt_ref[rows, :] + dtb_ref[...]
        dt = jnp.maximum(dt_in, 0.0) + jnp.log1p(jnp.exp(-jnp.abs(dt_in)))
        acum = jnp.dot(tril, dt * a_log2, preferred_element_type=F32, precision=lax.Precision.HIGHEST)
        acum_t = acum.T
        dt_t = dt.T
        ldt_t = jnp.log2(dt_t)
        rowp_t = acum_t - ldt_t
        w_t = dt_t * jnp.exp2(acum_t[:, q_ - 1:q_] - acum_t)
        c_decay = jnp.exp2(acum[q_ - 1:q_, :])
        for g in range(SSD_GROUPS):
            b_f = act_ref[:, SSD_INNER + g * SSD_STATE:SSD_INNER + (g + 1) * SSD_STATE]
            c_f = act_ref[:, SSD_INNER + (SSD_GROUPS + g) * SSD_STATE:SSD_INNER + (SSD_GROUPS + g + 1) * SSD_STATE]
            c_bf = c_f.astype(BF16)
            cbm = jnp.where(causal, _nt(c_bf, b_f.astype(BF16)), 0.0)
            b_t = b_f.T
            st = st_ref[g]
            y_off_all = _nn(c_bf, st.astype(BF16))
            ys, news, decs = [], [], []
            ss = jnp.zeros((q_, 1), F32)
            for j in range(SSD_GROUP_WIDTH // LANES):
                h1 = (g * SSD_GROUP_WIDTH + j * LANES) // SSD_HEAD_DIM
                ps = slice(g * SSD_GROUP_WIDTH + j * LANES, g * SSD_GROUP_WIDTH + (j + 1) * LANES)
                xs = act_ref[:, ps]
                intra, inter, e_l = [], [], []
                for hh in (h1, h1 + 1):
                    colb = jnp.broadcast_to(acum[:, hh:hh + 1], (q_, q_))
                    decay = jnp.exp2(jnp.minimum(colb - row_bcast(rowp_t, hh), row_bcast(ldt_t, hh)))
                    intra.append((cbm * decay).astype(BF16))
                    inter.append((b_t * row_bcast(w_t, hh)).astype(BF16))
                    e_l.append(jnp.exp2(colb))
                lhs = jnp.concatenate([jnp.concatenate(intra, axis=1), jnp.concatenate(inter, axis=1)], axis=0)
                xs_lo = xs * lo_half_f
                rhs = jnp.concatenate([xs_lo, xs - xs_lo], axis=0).astype(BF16)
                prod = _nn(lhs, rhs)
                y = prod[0:q_] + y_off_all[:, j * LANES:(j + 1) * LANES] * jnp.where(lo_half, e_l[0], e_l[1])
                y = y + dsk_ref[:, ps] * xs
                y = y * _silu(z_ref[rows, ps].astype(F32))
                ss = ss + jnp.sum(y * y, -1, keepdims=True)
                ys.append(y)
                news.append(prod[q_:2 * q_])
                decs.append(jnp.where(lo_half_row, c_decay[:, h1:h1 + 1], c_decay[:, h1 + 1:h1 + 2]))
            st_ref[g] = st * jnp.concatenate(decs, axis=1) + jnp.concatenate(news, axis=1)
            inv = lax.rsqrt(ss * (1.0 / SSD_GROUP_WIDTH) + RMS_EPS)
            for j, y in enumerate(ys):
                ps = slice(g * SSD_GROUP_WIDTH + j * LANES, g * SSD_GROUP_WIDTH + (j + 1) * LANES)
                o_ref[rows, ps] = (y * inv * nw_ref[:, ps]).astype(o_ref.dtype)
        return carry

    lax.fori_loop(0, n_chunks, chunk, 0)


def _ssd(h_main, h_small, conv_w, conv_b, dt_bias, a_log, d_skip, norm_w, batch, seq, tb=512):
    n = batch * seq
    nt = seq // tb
    bc_w = SSD_GROUPS * SSD_STATE
    pad = lambda v: jnp.zeros((1, LANES), F32).at[0, :SSD_HEADS].set(v.astype(F32))
    idx = lambda c: (lambda b, t: (b * nt + t, c))
    const = lambda b, t: (0, 0)
    return pl.pallas_call(
        functools.partial(_ssd_kernel, n_chunks=tb // SSD_CHUNK),
        grid=(batch, nt),
        in_specs=[
            pl.BlockSpec((tb, SSD_INNER), idx(L1_Z // SSD_INNER)),
            pl.BlockSpec((tb, SSD_INNER), idx(L1_XBC // SSD_INNER)),
            pl.BlockSpec((tb, bc_w), idx((L1_XBC + SSD_INNER) // bc_w)),
            pl.BlockSpec((tb, bc_w), idx((L1_XBC + SSD_INNER + bc_w) // bc_w)),
            pl.BlockSpec((tb, LANES), idx(0)),
            pl.BlockSpec((SSD_CONV, SSD_CONV_DIM), const),
            pl.BlockSpec((1, SSD_CONV_DIM), const),
            pl.BlockSpec((1, LANES), const),
            pl.BlockSpec((1, LANES), const),
            pl.BlockSpec((1, SSD_INNER), const),
            pl.BlockSpec((1, SSD_INNER), const),
        ],
        out_specs=pl.BlockSpec((tb, SSD_INNER), idx(0)),
        out_shape=jax.ShapeDtypeStruct((n, SSD_INNER), BF16),
        scratch_shapes=[
            pltpu.VMEM((SSD_CHUNK + 8, SSD_CONV_DIM), F32),
            pltpu.VMEM((SSD_CHUNK + 8, SSD_CONV_DIM), F32),
            pltpu.VMEM((SSD_CHUNK, CONV_COLS), F32),
            pltpu.VMEM((SSD_CHUNK, SSD_CONV_DIM), F32),
            pltpu.VMEM((SSD_GROUPS, SSD_STATE, SSD_GROUP_WIDTH), F32),
        ],
        compiler_params=_cparams(("arbitrary", "arbitrary")),
        name="ssd",
    )(h_main, h_main, h_main, h_main, h_small, conv_w, conv_b.reshape(1, -1),
      pad(dt_bias), pad(a_log), jnp.repeat(d_skip.astype(F32), SSD_HEAD_DIM).reshape(1, -1),
      norm_w.reshape(1, -1))


def _pad_cols(w, width=LANES):
    return jnp.zeros((w.shape[0], width), BF16).at[:, :w.shape[1]].set(w.astype(BF16))


def kernel(x, mem, positions, l0_w_in, l0_gla_w_gate, l0_gla_b_gate, l0_gla_norm_w, l0_mem_wk, l0_mem_wv, l0_w_out, l0_ln_w, l0_ln_b, l1_w_in, l1_conv_w, l1_conv_b, l1_dt_bias, l1_a_log, l1_d_skip, l1_ssd_norm_w, l1_mem_wk, l1_mem_wv, l1_w_out, l1_ln_w, l1_ln_b):
    batch, seq, dm = x.shape
    n = batch * seq
    x2 = x.reshape(n, dm)
    mem2 = mem.reshape(batch * MEM_LEN, dm)

    ga0 = L0_DQ + GLA_GATE_RANK
    w0_main = jnp.concatenate([l0_w_in[:, :L0_DQ], l0_w_in[:, ga0:]], axis=1).astype(BF16)
    w0_small = _pad_cols(l0_w_in[:, L0_DQ:ga0])
    cos, sin = _rope_tables(positions)
    col_scale = jnp.ones((1, L0_MAIN), F32).at[0, L0_DQ:L0_DK].set(HEAD_DIM ** -0.5 * LOG2_E)
    h0, h0s = _project(x2, w0_main, w0_small, rope=(cos, sin, col_scale, L0_DQ, L0_DV))
    o_a = _gla(h0, h0s, l0_gla_w_gate, l0_gla_b_gate, l0_gla_norm_w, batch, seq)
    o_b = _dilated_attention(h0, batch, seq)
    kv0 = _mem_kv(mem2, l0_mem_wk, l0_mem_wv)
    o_m = _mem_attn(h0, kv0, batch, seq, L0_MQ, L0_MG)
    x1 = _out_proj_ln([o_a, o_b, o_m], l0_w_out, x2, l0_ln_w, l0_ln_b)

    dt0 = SSD_INNER + SSD_CONV_DIM
    w1_main = jnp.concatenate([l1_w_in[:, :dt0], l1_w_in[:, dt0 + SSD_HEADS:]], axis=1).astype(BF16)
    w1_small = _pad_cols(l1_w_in[:, dt0:dt0 + SSD_HEADS])
    h1, h1s = _project(x1, w1_main, w1_small)
    y = _ssd(h1, h1s, l1_conv_w, l1_conv_b, l1_dt_bias, l1_a_log, l1_d_skip, l1_ssd_norm_w, batch, seq)
    kv1 = _mem_kv(mem2, l1_mem_wk, l1_mem_wv)
    o_m1 = _mem_attn(h1, kv1, batch, seq, L1_MQ, L1_MG)
    x2_out = _out_proj_ln([y, o_m1], l1_w_out, x1, l1_ln_w, l1_ln_b)
    return x2_out.reshape(batch, seq, dm)
```

```python
import functools

import jax
import jax.numpy as jnp
from jax import lax
from jax.experimental import pallas as pl
from jax.experimental.pallas import tpu as pltpu

F32 = jnp.float32
BF16 = jnp.bfloat16

DEPTH = 2
DEEPNORM_ALPHA = (2 * DEPTH) ** 0.25
LN_EPS = 1e-5
RMS_EPS = 1e-6
ROPE_THETA = 10000.0
MEM_LEN = 256

GLA_HEADS = 4
GLA_DK = 128
GLA_DV = 256
GLA_GATE_RANK = 16
GLA_GATE_NORMALIZER = 16.0
GLA_CHUNK = 64

DIL_PATTERNS = ((128, 1), (512, 4), (2048, 16))
DIL_HEADS = 4
HEAD_DIM = 128
DIL_BLOCK = 128

MEM_HEADS = 4

SSD_HEADS = 24
SSD_HEAD_DIM = 64
SSD_GROUPS = 4
SSD_STATE = 128
SSD_CONV = 4
SSD_CHUNK = 128
SSD_INNER = SSD_HEADS * SSD_HEAD_DIM
SSD_GROUP_WIDTH = SSD_INNER // SSD_GROUPS
SSD_CONV_DIM = SSD_INNER + 2 * SSD_GROUPS * SSD_STATE
CONV_COLS = 256

LANES = 128
LOG2_E = 1.4426950408889634
VMEM_LIMIT = 48 * 1024 * 1024

L0_GQ, L0_GK, L0_GV, L0_GG = 0, 512, 1024, 2048
L0_DQ, L0_DK, L0_DV, L0_DG = 3072, 4608, 5120, 5632
L0_MQ, L0_MG = 6144, 6656
L0_MAIN = 7168
L1_Z, L1_XBC, L1_MQ, L1_MG = 0, 1536, 4096, 4608
L1_MAIN = 5120


def _nt(a, b):
    return lax.dot_general(a, b, (((1,), (1,)), ((), ())), preferred_element_type=F32)


def _tn(a, b):
    return lax.dot_general(a, b, (((0,), (0,)), ((), ())), preferred_element_type=F32)


def _nn(a, b):
    return jnp.dot(a, b, preferred_element_type=F32)


def _cumsum_rows(tril, x):
    hi = x.astype(BF16)
    rest = x - hi.astype(F32)
    mid = rest.astype(BF16)
    lo = (rest - mid.astype(F32)).astype(BF16)
    return _nn(tril, hi) + _nn(tril, mid) + _nn(tril, lo)


def _silu(x):
    return 0.5 * x * (1.0 + jnp.tanh(0.5 * x))


def _cparams(sem):
    return pltpu.CompilerParams(dimension_semantics=sem, vmem_limit_bytes=VMEM_LIMIT)


def _rope_table_kernel(pos_ref, inv_ref, sign_ref, cos_ref, sin_ref):
    ang = pos_ref[...].astype(F32) * inv_ref[...]
    cos_ref[...] = jnp.cos(ang)
    sin_ref[...] = jnp.sin(ang) * sign_ref[...]


def _rope_tables(positions):
    n = positions.size
    half = HEAD_DIM // 2
    inv = 1.0 / (ROPE_THETA ** (jnp.arange(0, HEAD_DIM, 2, dtype=F32) / HEAD_DIM))
    inv2 = jnp.concatenate([inv, inv]).reshape(1, HEAD_DIM)
    sign = jnp.concatenate([-jnp.ones((half,), F32), jnp.ones((half,), F32)]).reshape(1, HEAD_DIM)
    pos_b = jnp.broadcast_to(positions.reshape(n, 1), (n, HEAD_DIM))
    tb = 2048
    row = pl.BlockSpec((tb, HEAD_DIM), lambda i: (i, 0))
    vec = pl.BlockSpec((1, HEAD_DIM), lambda i: (0, 0))
    return pl.pallas_call(
        _rope_table_kernel,
        grid=(n // tb,),
        in_specs=[row, vec, vec],
        out_specs=[row, row],
        out_shape=[jax.ShapeDtypeStruct((n, HEAD_DIM), F32)] * 2,
        compiler_params=_cparams(("arbitrary",)),
        name="rope_tables",
    )(pos_b, inv2, sign)


ROPE_SUB = 256


def _proj_kernel(*refs, split, rope_tiles, cast_x, tn):
    x_ref, wa_ref, wb_ref, ws_ref = refs[:4]
    if rope_tiles is None:
        o_ref, os_ref = refs[4:6]
        scratch = refs[6:]
    else:
        c_ref, s_ref, sc_ref, o_ref, os_ref = refs[4:9]
        scratch = refs[9:]
    xb_ref = scratch[0] if cast_x else x_ref
    j = pl.program_id(1)

    @pl.when(j == 0)
    def _():
        if cast_x:
            xb_ref[...] = x_ref[...].astype(BF16)
        os_ref[...] = _nn(xb_ref[...], ws_ref[...])

    def plain(w_ref):
        o_ref[...] = _nn(xb_ref[...], w_ref[...]).astype(o_ref.dtype)

    def rotary(w_ref):
        cos = c_ref[...]
        sin = s_ref[...]
        for c0 in range(0, tn, ROPE_SUB):
            acc = _nn(xb_ref[...], w_ref[:, c0:c0 + ROPE_SUB])
            for k in range(ROPE_SUB // HEAD_DIM):
                sl = slice(c0 + k * HEAD_DIM, c0 + (k + 1) * HEAD_DIM)
                a = acc[:, k * HEAD_DIM:(k + 1) * HEAD_DIM] * sc_ref[:, sl]
                o_ref[:, sl] = (a * cos + pltpu.roll(a, HEAD_DIM // 2, 1) * sin).astype(o_ref.dtype)

    in_first = j < split
    pl.when(in_first)(lambda: plain(wa_ref))
    if rope_tiles is None:
        pl.when(jnp.logical_not(in_first))(lambda: plain(wb_ref))
    else:
        lo, hi = rope_tiles
        is_rope = jnp.logical_and(j >= lo, j < hi)
        pl.when(is_rope)(lambda: rotary(wb_ref))
        pl.when(jnp.logical_and(jnp.logical_not(in_first), jnp.logical_not(is_rope)))(lambda: plain(wb_ref))


def _project(x, w_a, w_b, w_small, rope=None, tm=1024, tn=1024):
    n, kdim = x.shape
    m = w_a.shape[1] + w_b.shape[1]
    split = w_a.shape[1] // tn
    cast_x = x.dtype != BF16
    in_specs = [
        pl.BlockSpec((tm, kdim), lambda i, j: (i, 0)),
        pl.BlockSpec((kdim, tn), lambda i, j: (0, jnp.minimum(j, split - 1))),
        pl.BlockSpec((kdim, tn), lambda i, j: (0, jnp.maximum(j - split, 0))),
        pl.BlockSpec((kdim, LANES), lambda i, j: (0, 0)),
    ]
    args = [x, w_a, w_b, w_small]
    rope_tiles = None
    if rope is not None:
        cos, sin, col_scale, first_col, last_col = rope
        rope_tiles = (first_col // tn, last_col // tn)
        assert rope_tiles[0] >= split
        in_specs += [
            pl.BlockSpec((tm, HEAD_DIM), lambda i, j: (i, 0)),
            pl.BlockSpec((tm, HEAD_DIM), lambda i, j: (i, 0)),
            pl.BlockSpec((1, tn), lambda i, j: (0, j)),
        ]
        args += [cos, sin, col_scale]
    return pl.pallas_call(
        functools.partial(_proj_kernel, split=split, rope_tiles=rope_tiles, cast_x=cast_x, tn=tn),
        grid=(n // tm, m // tn),
        in_specs=in_specs,
        out_specs=[
            pl.BlockSpec((tm, tn), lambda i, j: (i, j)),
            pl.BlockSpec((tm, LANES), lambda i, j: (i, 0)),
        ],
        out_shape=[jax.ShapeDtypeStruct((n, m), BF16), jax.ShapeDtypeStruct((n, LANES), F32)],
        scratch_shapes=[pltpu.VMEM((tm, kdim), BF16)] if cast_x else [],
        compiler_params=_cparams(("arbitrary", "arbitrary")),
        name="in_proj",
    )(*args)


def _gla_kernel(q_ref, k_ref, v_ref, g_ref, ga_ref, wg_ref, bg_ref, nw_ref, o_ref, st_ref, *, n_chunks):
    c_ = GLA_CHUNK

    @pl.when(pl.program_id(1) == 0)
    def _():
        st_ref[...] = jnp.zeros_like(st_ref)

    row = lax.broadcasted_iota(jnp.int32, (c_, c_), 0)
    col = lax.broadcasted_iota(jnp.int32, (c_, c_), 1)
    causal = col <= row
    tril = jnp.where(causal, 1.0, 0.0).astype(BF16)
    wg = wg_ref[...]
    bg = bg_ref[...]
    nw = nw_ref[...]

    def chunk(ci, carry):
        r0 = pl.multiple_of(ci * c_, c_)
        rows = pl.ds(r0, c_)
        z = _nn(ga_ref[rows, :].astype(BF16), wg) + bg
        log_a = (jnp.minimum(z, 0.0) - jnp.log1p(jnp.exp(-jnp.abs(z)))) * (1.0 / GLA_GATE_NORMALIZER)
        bcum = _cumsum_rows(tril, log_a)
        blast = bcum[c_ - 1:c_, :]
        e_pos = jnp.exp(bcum)
        e_neg = jnp.exp(-bcum)
        e_tail = jnp.exp(blast - bcum)
        e_last = jnp.exp(blast)
        for h in range(GLA_HEADS):
            ks = slice(h * GLA_DK, (h + 1) * GLA_DK)
            vs = slice(h * GLA_DV, (h + 1) * GLA_DV)
            q = q_ref[rows, ks].astype(F32) * (GLA_DK ** -0.5)
            k = k_ref[rows, ks].astype(F32)
            q_dec = (q * e_pos[:, ks]).astype(BF16)
            k_inv = (k * e_neg[:, ks]).astype(BF16)
            k_tail = (k * e_tail[:, ks]).astype(BF16)
            v = v_ref[rows, vs]
            scores = jnp.where(causal, _nt(q_dec, k_inv), 0.0)
            st = st_ref[h]
            o = _nn(scores.astype(BF16), v) + _nt(q_dec, st.astype(BF16))
            st_ref[h] = st * e_last[:, ks] + _tn(v, k_tail)
            ms = jnp.mean(o * o, axis=-1, keepdims=True)
            o = o * lax.rsqrt(ms + RMS_EPS) * nw
            o_ref[rows, vs] = (o * _silu(g_ref[rows, vs].astype(F32))).astype(o_ref.dtype)
        return carry

    lax.fori_loop(0, n_chunks, chunk, 0, unroll=2)


def _gla(h_main, h_small, w_gate, b_gate, norm_w, batch, seq, tb=512):
    n = batch * seq
    nt = seq // tb
    wg = jnp.zeros((LANES, GLA_HEADS * GLA_DK), BF16).at[:GLA_GATE_RANK].set(w_gate.astype(BF16))
    idx = lambda c: (lambda b, t: (b * nt + t, c))
    const = lambda b, t: (0, 0)
    return pl.pallas_call(
        functools.partial(_gla_kernel, n_chunks=tb // GLA_CHUNK),
        grid=(batch, nt),
        in_specs=[
            pl.BlockSpec((tb, 512), idx(L0_GQ // 512)),
            pl.BlockSpec((tb, 512), idx(L0_GK // 512)),
            pl.BlockSpec((tb, 1024), idx(L0_GV // 1024)),
            pl.BlockSpec((tb, 1024), idx(L0_GG // 1024)),
            pl.BlockSpec((tb, LANES), idx(0)),
            pl.BlockSpec((LANES, 512), const),
            pl.BlockSpec((1, 512), const),
            pl.BlockSpec((1, GLA_DV), const),
        ],
        out_specs=pl.BlockSpec((tb, 1024), idx(0)),
        out_shape=jax.ShapeDtypeStruct((n, GLA_HEADS * GLA_DV), BF16),
        scratch_shapes=[pltpu.VMEM((GLA_HEADS, GLA_DV, GLA_DK), F32)],
        compiler_params=_cparams(("arbitrary", "arbitrary")),
        name="gla",
    )(h_main, h_main, h_main, h_main, h_small, wg, b_gate.reshape(1, -1), norm_w.reshape(1, -1))


DIL_WINDOW = 2048
SLAB_GROUP = 16
SLAB_PITCH = 24
SEG = 32
TILE_UNROLL = 8


def _dil_kernel(q0_ref, q1_ref, q2_ref, kc_ref, kp_ref, vc_ref, vp_ref, g_ref, out_ref,
                kcat, vcat, ks, vs, q1s, q2s, o0s, l0s, o1s, l1s, o2s, l2s, bias):
    blk = DIL_BLOCK
    w = DIL_WINDOW
    groups = w // SLAB_GROUP
    t = pl.program_id(1)
    first_window = t == 0

    rho = lax.broadcasted_iota(jnp.int32, (blk, 2 * blk), 0)
    kap = lax.broadcasted_iota(jnp.int32, (blk, 2 * blk), 1)
    in_prev = kap < blk
    kap_l = jnp.where(in_prev, kap, kap - blk)
    perm = lambda x: 4 * (x % SEG) + x // SEG
    neg = jnp.float32(-jnp.inf)
    for slot, (jq, jk) in enumerate(((rho, kap_l), (perm(rho), perm(kap_l)))):
        ok_prev = jnp.where(jk >= jq, 0.0, neg)
        ok_cur = jnp.where(jk <= jq, 0.0, neg)
        bias[2 * slot] = jnp.where(in_prev, ok_prev, ok_cur)
        bias[2 * slot + 1] = jnp.where(in_prev, neg, ok_cur)

    kcat[0:blk, :] = kp_ref[w - blk:w, :]
    kcat[blk:blk + w, :] = kc_ref[...]
    vcat[0:blk, :] = vp_ref[w - blk:w, :]
    vcat[blk:blk + w, :] = vc_ref[...]

    def to_slabs(i, carry):
        src = pl.ds(pl.multiple_of(i * SLAB_GROUP, SLAB_GROUP), SLAB_GROUP)
        prev = pl.ds(pl.multiple_of(i * SLAB_PITCH, 8), SLAB_GROUP)
        cur = pl.ds(pl.multiple_of((groups + i) * SLAB_PITCH, 8), SLAB_GROUP)
        ks[prev, :] = kp_ref[src, :].astype(F32)
        ks[cur, :] = kc_ref[src, :].astype(F32)
        vs[prev, :] = vp_ref[src, :].astype(F32)
        vs[cur, :] = vc_ref[src, :].astype(F32)
        q1s[prev, :] = q1_ref[src, :].astype(F32)
        q2s[prev, :] = q2_ref[src, :].astype(F32)
        return carry

    lax.fori_loop(0, groups, to_slabs, 0, unroll=4)

    def attend(q, k, v, b):
        s = _nt(q, k) + b
        m = jnp.max(s, -1, keepdims=True)
        p = jnp.exp2(s - m)
        den = jnp.sum(p, -1, keepdims=True)
        o = _nn(p.astype(BF16), v) * (1.0 / den)
        return o, jnp.broadcast_to(m + jnp.log2(den), (blk, LANES))

    def group0(i, carry):
        r0 = pl.multiple_of(i * blk, blk)
        sel = jnp.where(jnp.logical_and(first_window, i == 0), 1, 0)
        o, l = attend(q0_ref[pl.ds(r0, blk), :], kcat[pl.ds(r0, 2 * blk), :], vcat[pl.ds(r0, 2 * blk), :],
                      bias[sel])
        o0s[pl.ds(r0, blk), :] = o
        l0s[pl.ds(r0, blk), :] = l
        return carry

    lax.fori_loop(0, w // blk, group0, 0, unroll=TILE_UNROLL)

    def seg(ref, group, b):
        return ref[pl.ds(group * SLAB_PITCH + b, SEG, stride=SLAB_PITCH), :]

    def group1(idx, carry):
        r = idx // 4
        n = idx % 4
        g_cur = groups + SEG * n
        q = jnp.concatenate([seg(q1s, SEG * n, r + 4 * j) for j in range(4)], axis=0).astype(BF16)
        k = jnp.concatenate([seg(ks, g_cur - SEG, r + 4 * j) for j in range(4)]
                            + [seg(ks, g_cur, r + 4 * j) for j in range(4)], axis=0).astype(BF16)
        v = jnp.concatenate([seg(vs, g_cur - SEG, r + 4 * j) for j in range(4)]
                            + [seg(vs, g_cur, r + 4 * j) for j in range(4)], axis=0).astype(BF16)
        sel = jnp.where(jnp.logical_and(first_window, n == 0), 3, 2)
        o, l = attend(q, k, v, bias[sel])
        for j in range(4):
            dst = pl.ds(SEG * n * SLAB_PITCH + r + 4 * j, SEG, stride=SLAB_PITCH)
            o1s[dst, :] = o[j * SEG:(j + 1) * SEG]
            l1s[dst, :] = l[j * SEG:(j + 1) * SEG]
        return carry

    lax.fori_loop(0, 16, group1, 0, unroll=TILE_UNROLL)

    def group2(r, carry):
        prev = pl.ds(r, blk, stride=SLAB_PITCH)
        cur = pl.ds(groups * SLAB_PITCH + r, blk, stride=SLAB_PITCH)
        q = q2s[prev, :].astype(BF16)
        k = jnp.concatenate([ks[prev, :], ks[cur, :]], axis=0).astype(BF16)
        v = jnp.concatenate([vs[prev, :], vs[cur, :]], axis=0).astype(BF16)
        o, l = attend(q, k, v, bias[jnp.where(first_window, 1, 0)])
        o2s[prev, :] = o
        l2s[prev, :] = l
        return carry

    lax.fori_loop(0, SLAB_GROUP, group2, 0, unroll=TILE_UNROLL)

    def merge(i, carry):
        r0 = pl.multiple_of(i * blk, blk)
        rows = pl.ds(r0, blk)
        pieces = [pl.ds(pl.multiple_of((i * (blk // SLAB_GROUP) + j) * SLAB_PITCH, 8), SLAB_GROUP)
                  for j in range(blk // SLAB_GROUP)]
        gather = lambda ref: jnp.concatenate([ref[p, :] for p in pieces], axis=0)
        l0, l1, l2 = l0s[rows, :], gather(l1s), gather(l2s)
        m = jnp.maximum(jnp.maximum(l0, l1), l2)
        w0, w1, w2 = jnp.exp2(l0 - m), jnp.exp2(l1 - m), jnp.exp2(l2 - m)
        o = (w0 * o0s[rows, :] + w1 * gather(o1s) + w2 * gather(o2s)) * (1.0 / (w0 + w1 + w2))
        out_ref[rows, :] = (o * _silu(g_ref[rows, :].astype(F32))).astype(out_ref.dtype)
        return carry

    lax.fori_loop(0, w // blk, merge, 0)


def _dilated_attention(h_main, batch, seq):
    n = batch * seq
    w = DIL_WINDOW
    nt = seq // w
    slab_rows = (w // SLAB_GROUP) * SLAB_PITCH

    def cur(col):
        return pl.BlockSpec((w, HEAD_DIM), lambda b, t, h: (b * nt + t, col // HEAD_DIM + h))

    def prev(col):
        return pl.BlockSpec((w, HEAD_DIM), lambda b, t, h: (b * nt + jnp.maximum(t - 1, 0), col // HEAD_DIM + h))

    width = DIL_HEADS * HEAD_DIM
    slab = lambda rows: pltpu.VMEM((rows, LANES), F32)
    return pl.pallas_call(
        _dil_kernel,
        grid=(batch, nt, DIL_HEADS),
        in_specs=[cur(L0_DQ), cur(L0_DQ + width), cur(L0_DQ + 2 * width),
                  cur(L0_DK), prev(L0_DK), cur(L0_DV), prev(L0_DV), cur(L0_DG)],
        out_specs=pl.BlockSpec((w, HEAD_DIM), lambda b, t, h: (b * nt + t, h)),
        out_shape=jax.ShapeDtypeStruct((n, width), BF16),
        scratch_shapes=[
            pltpu.VMEM((w + DIL_BLOCK, HEAD_DIM), BF16), pltpu.VMEM((w + DIL_BLOCK, HEAD_DIM), BF16),
            slab(2 * slab_rows), slab(2 * slab_rows), slab(slab_rows), slab(slab_rows),
            slab(w), slab(w), slab(slab_rows), slab(slab_rows), slab(slab_rows), slab(slab_rows),
            pltpu.VMEM((4, DIL_BLOCK, 2 * DIL_BLOCK), F32),
        ],
        compiler_params=_cparams(("arbitrary", "arbitrary", "arbitrary")),
        name="dilated_attn",
    )(*([h_main] * 8))


def _mem_kv_kernel(m_ref, w_ref, o_ref):
    o_ref[...] = _nn(m_ref[...].astype(BF16), w_ref[...]).astype(o_ref.dtype)


def _mem_kv(mem2d, wk, wv):
    rows, kdim = mem2d.shape
    w = jnp.concatenate([wk, wv], axis=1).astype(BF16)
    tm = MEM_LEN
    return pl.pallas_call(
        _mem_kv_kernel,
        grid=(rows // tm,),
        in_specs=[pl.BlockSpec((tm, kdim), lambda i: (i, 0)), pl.BlockSpec(w.shape, lambda i: (0, 0))],
        out_specs=pl.BlockSpec((tm, w.shape[1]), lambda i: (i, 0)),
        out_shape=jax.ShapeDtypeStruct((rows, w.shape[1]), BF16),
        compiler_params=_cparams(("arbitrary",)),
        name="mem_kv",
    )(mem2d, w)


def _mem_attn_kernel(q_ref, g_ref, k_ref, v_ref, o_ref, *, tq):
    sub = 128
    for i in range(tq // sub):
        rows = slice(i * sub, (i + 1) * sub)
        for h in range(MEM_HEADS):
            hs = slice(h * HEAD_DIM, (h + 1) * HEAD_DIM)
            s = _nt(q_ref[rows, hs], k_ref[:, hs]) * (HEAD_DIM ** -0.5)
            m = jnp.max(s, -1, keepdims=True)
            p = jnp.exp(s - m)
            den = jnp.sum(p, -1, keepdims=True)
            o = _nn(p.astype(BF16), v_ref[:, hs]) * (1.0 / den)
            o_ref[rows, hs] = (o * _silu(g_ref[rows, hs].astype(F32))).astype(o_ref.dtype)


def _mem_attn(h_main, kv, batch, seq, q_col, g_col, tq=512):
    n = batch * seq
    width = MEM_HEADS * HEAD_DIM
    nt = seq // tq
    return pl.pallas_call(
        functools.partial(_mem_attn_kernel, tq=tq),
        grid=(batch, nt),
        in_specs=[
            pl.BlockSpec((tq, width), lambda b, t: (b * nt + t, q_col // width)),
            pl.BlockSpec((tq, width), lambda b, t: (b * nt + t, g_col // width)),
            pl.BlockSpec((MEM_LEN, width), lambda b, t: (b, 0)),
            pl.BlockSpec((MEM_LEN, width), lambda b, t: (b, 1)),
        ],
        out_specs=pl.BlockSpec((tq, width), lambda b, t: (b * nt + t, 0)),
        out_shape=jax.ShapeDtypeStruct((n, width), BF16),
        compiler_params=_cparams(("arbitrary", "arbitrary")),
        name="mem_attn",
    )(h_main, h_main, kv, kv)


def _out_kernel(*refs, n_parts, emit_bf16):
    a_refs = refs[:n_parts]
    w_refs = refs[n_parts:2 * n_parts]
    x_ref, lw_ref, lb_ref, o_ref = refs[2 * n_parts:2 * n_parts + 4]
    acc = DEEPNORM_ALPHA * x_ref[...]
    for a_ref, w_ref in zip(a_refs, w_refs):
        acc = acc + _nn(a_ref[...], w_ref[...])
    mu = jnp.mean(acc, -1, keepdims=True)
    d = acc - mu
    var = jnp.mean(d * d, -1, keepdims=True)
    y = d * lax.rsqrt(var + LN_EPS) * lw_ref[...] + lb_ref[...]
    o_ref[...] = y
    if emit_bf16:
        refs[2 * n_parts + 4][...] = y.astype(BF16)


def _out_proj_ln(parts, w_out, x, ln_w, ln_b, emit_bf16=False, tm=512):
    n, dm = x.shape
    w_bf = w_out.astype(BF16)
    w_specs, off = [], 0
    for p in parts:
        width = p.shape[1]
        w_specs.append(pl.BlockSpec((width, dm), functools.partial(lambda i, blk: (blk, 0), blk=off // width)))
        assert off % width == 0
        off += width
    row = lambda width: pl.BlockSpec((tm, width), lambda i: (i, 0))
    const = lambda shape: pl.BlockSpec(shape, lambda i: (0, 0))
    out_specs = [row(dm)] + ([row(dm)] if emit_bf16 else [])
    out_shape = [jax.ShapeDtypeStruct((n, dm), F32)] + ([jax.ShapeDtypeStruct((n, dm), BF16)] if emit_bf16 else [])
    return pl.pallas_call(
        functools.partial(_out_kernel, n_parts=len(parts), emit_bf16=emit_bf16),
        grid=(n // tm,),
        in_specs=[row(p.shape[1]) for p in parts] + w_specs + [row(dm), const((1, dm)), const((1, dm))],
        out_specs=out_specs,
        out_shape=out_shape,
        compiler_params=_cparams(("arbitrary",)),
        name="out_proj_ln",
    )(*parts, *([w_bf] * len(parts)), x, ln_w.reshape(1, dm), ln_b.reshape(1, dm))


def _ssd_kernel(z_ref, xs_ref, bm_ref, cm_ref, dt_ref, cw_ref, cb_ref, dtb_ref, a_ref, dsk_ref, nw_ref,
                o_ref, u_ref, pair_ref, lag_ref, act_ref, st_ref, *, n_chunks):
    q_ = SSD_CHUNK
    halo = 8

    @pl.when(pl.program_id(1) == 0)
    def _():
        st_ref[...] = jnp.zeros_like(st_ref)
        u_ref[0:halo, :] = jnp.zeros((halo, SSD_CONV_DIM), F32)
        pair_ref[0:halo, :] = jnp.zeros((halo, SSD_CONV_DIM), F32)

    row = lax.broadcasted_iota(jnp.int32, (q_, q_), 0)
    col = lax.broadcasted_iota(jnp.int32, (q_, q_), 1)
    causal = col <= row
    tril = jnp.where(causal, 1.0, 0.0).astype(BF16)
    lane = lax.broadcasted_iota(jnp.int32, (q_, LANES), 1)
    lo_half = lane < SSD_HEAD_DIM
    lane_row = lax.broadcasted_iota(jnp.int32, (1, LANES), 1)
    lo_half_row = lane_row < SSD_HEAD_DIM
    lo_half_f = lo_half.astype(F32)
    a_log2 = -jnp.exp(a_ref[...]) * LOG2_E

    def row_bcast(t, r):
        return jnp.tile(jnp.broadcast_to(t[r:r + 1, :], (8, q_)), (q_ // 8, 1))

    def chunk(ci, carry):
        r0 = pl.multiple_of(ci * q_, q_)
        rows = pl.ds(r0, q_)
        bc_w = SSD_GROUPS * SSD_STATE
        u_ref[halo:halo + q_, 0:SSD_INNER] = xs_ref[rows, :].astype(F32)
        u_ref[halo:halo + q_, SSD_INNER:SSD_INNER + bc_w] = bm_ref[rows, :].astype(F32)
        u_ref[halo:halo + q_, SSD_INNER + bc_w:SSD_CONV_DIM] = cm_ref[rows, :].astype(F32)
        for c0 in range(0, SSD_CONV_DIM, CONV_COLS):
            cs = slice(c0, c0 + CONV_COLS)
            u_cur = u_ref[halo:halo + q_, cs]
            lag_ref[...] = u_ref[halo - 1:halo - 1 + q_, cs]
            u_lag = lag_ref[...]
            pair_ref[halo:halo + q_, cs] = cw_ref[1:2, cs] * u_cur + cw_ref[0:1, cs] * u_lag
            conv = ((cb_ref[:, cs] + cw_ref[3:4, cs] * u_cur) + cw_ref[2:3, cs] * u_lag
                    + pair_ref[halo - 2:halo - 2 + q_, cs])
            act_ref[:, cs] = _silu(conv)
        u_ref[0:halo, :] = u_ref[q_:q_ + halo, :]
        pair_ref[0:halo, :] = pair_ref[q_:q_ + halo, :]
        dt_in = dt_ref[rows, :] + dtb_ref[...]
        dt = jnp.maximum(dt_in, 0.0) + jnp.log1p(jnp.exp(-jnp.abs(dt_in)))
        acum = _cumsum_rows(tril, dt * a_log2)
        acum_t = acum.T
        dt_t = dt.T
        ldt_t = jnp.log2(dt_t)
        rowp_t = acum_t - ldt_t
        w_t = dt_t * jnp.exp2(acum_t[:, q_ - 1:q_] - acum_t)
        c_decay = jnp.exp2(acum[q_ - 1:q_, :])
        for g in range(SSD_GROUPS):
            b_f = act_ref[:, SSD_INNER + g * SSD_STATE:SSD_INNER + (g + 1) * SSD_STATE]
            c_f = act_ref[:, SSD_INNER + (SSD_GROUPS + g) * SSD_STATE:SSD_INNER + (SSD_GROUPS + g + 1) * SSD_STATE]
            c_bf = c_f.astype(BF16)
            cbm = jnp.where(causal, _nt(c_bf, b_f.astype(BF16)), 0.0)
            b_t = b_f.T
            st = st_ref[g]
            y_off_all = _nn(c_bf, st.astype(BF16))
            ys, news, decs = [], [], []
            ss = jnp.zeros((q_, 1), F32)
            for j in range(SSD_GROUP_WIDTH // LANES):
                h1 = (g * SSD_GROUP_WIDTH + j * LANES) // SSD_HEAD_DIM
                ps = slice(g * SSD_GROUP_WIDTH + j * LANES, g * SSD_GROUP_WIDTH + (j + 1) * LANES)
                xs = act_ref[:, ps]
                intra, inter, e_l = [], [], []
                for hh in (h1, h1 + 1):
                    colb = jnp.broadcast_to(acum[:, hh:hh + 1], (q_, q_))
                    decay = jnp.exp2(jnp.minimum(colb - row_bcast(rowp_t, hh), row_bcast(ldt_t, hh)))
                    intra.append((cbm * decay).astype(BF16))
                    inter.append((b_t * row_bcast(w_t, hh)).astype(BF16))
                    e_l.append(jnp.exp2(colb))
                lhs = jnp.concatenate([jnp.concatenate(intra, axis=1), jnp.concatenate(inter, axis=1)], axis=0)
                xs_lo = xs * lo_half_f
                rhs = jnp.concatenate([xs_lo, xs - xs_lo], axis=0).astype(BF16)
                prod = _nn(lhs, rhs)
                y = prod[0:q_] + y_off_all[:, j * LANES:(j + 1) * LANES] * jnp.where(lo_half, e_l[0], e_l[1])
                y = y + dsk_ref[:, ps] * xs
                y = y * _silu(z_ref[rows, ps].astype(F32))
                ss = ss + jnp.sum(y * y, -1, keepdims=True)
                ys.append(y)
                news.append(prod[q_:2 * q_])
                decs.append(jnp.where(lo_half_row, c_decay[:, h1:h1 + 1], c_decay[:, h1 + 1:h1 + 2]))
            st_ref[g] = st * jnp.concatenate(decs, axis=1) + jnp.concatenate(news, axis=1)
            inv = lax.rsqrt(ss * (1.0 / SSD_GROUP_WIDTH) + RMS_EPS)
            for j, y in enumerate(ys):
                ps = slice(g * SSD_GROUP_WIDTH + j * LANES, g * SSD_GROUP_WIDTH + (j + 1) * LANES)
                o_ref[rows, ps] = (y * inv * nw_ref[:, ps]).astype(o_ref.dtype)
        return carry

    lax.fori_loop(0, n_chunks, chunk, 0)


def _ssd(h_main, h_small, conv_w, conv_b, dt_bias, a_log, d_skip, norm_w, batch, seq, tb=512):
    n = batch * seq
    nt = seq // tb
    bc_w = SSD_GROUPS * SSD_STATE
    pad = lambda v: jnp.zeros((1, LANES), F32).at[0, :SSD_HEADS].set(v.astype(F32))
    idx = lambda c: (lambda b, t: (b * nt + t, c))
    const = lambda b, t: (0, 0)
    return pl.pallas_call(
        functools.partial(_ssd_kernel, n_chunks=tb // SSD_CHUNK),
        grid=(batch, nt),
        in_specs=[
            pl.BlockSpec((tb, SSD_INNER), idx(L1_Z // SSD_INNER)),
            pl.BlockSpec((tb, SSD_INNER), idx(L1_XBC // SSD_INNER)),
            pl.BlockSpec((tb, bc_w), idx((L1_XBC + SSD_INNER) // bc_w)),
            pl.BlockSpec((tb, bc_w), idx((L1_XBC + SSD_INNER + bc_w) // bc_w)),
            pl.BlockSpec((tb, LANES), idx(0)),
            pl.BlockSpec((SSD_CONV, SSD_CONV_DIM), const),
            pl.BlockSpec((1, SSD_CONV_DIM), const),
            pl.BlockSpec((1, LANES), const),
            pl.BlockSpec((1, LANES), const),
            pl.BlockSpec((1, SSD_INNER), const),
            pl.BlockSpec((1, SSD_INNER), const),
        ],
        out_specs=pl.BlockSpec((tb, SSD_INNER), idx(0)),
        out_shape=jax.ShapeDtypeStruct((n, SSD_INNER), BF16),
        scratch_shapes=[
            pltpu.VMEM((SSD_CHUNK + 8, SSD_CONV_DIM), F32),
            pltpu.VMEM((SSD_CHUNK + 8, SSD_CONV_DIM), F32),
            pltpu.VMEM((SSD_CHUNK, CONV_COLS), F32),
            pltpu.VMEM((SSD_CHUNK, SSD_CONV_DIM), F32),
            pltpu.VMEM((SSD_GROUPS, SSD_STATE, SSD_GROUP_WIDTH), F32),
        ],
        compiler_params=_cparams(("arbitrary", "arbitrary")),
        name="ssd",
    )(h_main, h_main, h_main, h_main, h_small, conv_w, conv_b.reshape(1, -1),
      pad(dt_bias), pad(a_log), jnp.repeat(d_skip.astype(F32), SSD_HEAD_DIM).reshape(1, -1),
      norm_w.reshape(1, -1))


def _pad_cols(w, width=LANES):
    return jnp.zeros((w.shape[0], width), BF16).at[:, :w.shape[1]].set(w.astype(BF16))


def kernel(x, mem, positions, l0_w_in, l0_gla_w_gate, l0_gla_b_gate, l0_gla_norm_w, l0_mem_wk, l0_mem_wv, l0_w_out, l0_ln_w, l0_ln_b, l1_w_in, l1_conv_w, l1_conv_b, l1_dt_bias, l1_a_log, l1_d_skip, l1_ssd_norm_w, l1_mem_wk, l1_mem_wv, l1_w_out, l1_ln_w, l1_ln_b):
    batch, seq, dm = x.shape
    n = batch * seq
    x2 = x.reshape(n, dm)
    mem2 = mem.reshape(batch * MEM_LEN, dm)

    ga0 = L0_DQ + GLA_GATE_RANK
    w0_a = l0_w_in[:, :L0_DQ].astype(BF16)
    w0_b = l0_w_in[:, ga0:].astype(BF16)
    w0_small = _pad_cols(l0_w_in[:, L0_DQ:ga0])
    cos, sin = _rope_tables(positions)
    col_scale = jnp.ones((1, L0_MAIN), F32).at[0, L0_DQ:L0_DK].set(HEAD_DIM ** -0.5 * LOG2_E)
    h0, h0s = _project(x2, w0_a, w0_b, w0_small, rope=(cos, sin, col_scale, L0_DQ, L0_DV))
    o_a = _gla(h0, h0s, l0_gla_w_gate, l0_gla_b_gate, l0_gla_norm_w, batch, seq)
    o_b = _dilated_attention(h0, batch, seq)
    kv0 = _mem_kv(mem2, l0_mem_wk, l0_mem_wv)
    o_m = _mem_attn(h0, kv0, batch, seq, L0_MQ, L0_MG)
    x1, x1_bf = _out_proj_ln([o_a, o_b, o_m], l0_w_out, x2, l0_ln_w, l0_ln_b, emit_bf16=True)

    dt0 = SSD_INNER + SSD_CONV_DIM
    w1_a = l1_w_in[:, :dt0].astype(BF16)
    w1_b = l1_w_in[:, dt0 + SSD_HEADS:].astype(BF16)
    w1_small = _pad_cols(l1_w_in[:, dt0:dt0 + SSD_HEADS])
    h1, h1s = _project(x1_bf, w1_a, w1_b, w1_small)
    y = _ssd(h1, h1s, l1_conv_w, l1_conv_b, l1_dt_bias, l1_a_log, l1_d_skip, l1_ssd_norm_w, batch, seq)
    kv1 = _mem_kv(mem2, l1_mem_wk, l1_mem_wv)
    o_m1 = _mem_attn(h1, kv1, batch, seq, L1_MQ, L1_MG)
    (x2_out,) = _out_proj_ln([y, o_m1], l1_w_out, x1, l1_ln_w, l1_ln_b)
    return x2_out.reshape(batch, seq, dm)
```

```python
import functools

import jax
import jax.numpy as jnp
from jax import lax
from jax.experimental import pallas as pl
from jax.experimental.pallas import tpu as pltpu

F32 = jnp.float32
BF16 = jnp.bfloat16

DEPTH = 2
DEEPNORM_ALPHA = (2 * DEPTH) ** 0.25
LN_EPS = 1e-5
RMS_EPS = 1e-6
ROPE_THETA = 10000.0
MEM_LEN = 256

GLA_HEADS = 4
GLA_DK = 128
GLA_DV = 256
GLA_GATE_RANK = 16
GLA_GATE_NORMALIZER = 16.0
GLA_CHUNK = 64

DIL_PATTERNS = ((128, 1), (512, 4), (2048, 16))
DIL_HEADS = 4
HEAD_DIM = 128
DIL_BLOCK = 128

MEM_HEADS = 4

SSD_HEADS = 24
SSD_HEAD_DIM = 64
SSD_GROUPS = 4
SSD_STATE = 128
SSD_CONV = 4
SSD_CHUNK = 128
SSD_INNER = SSD_HEADS * SSD_HEAD_DIM
SSD_GROUP_WIDTH = SSD_INNER // SSD_GROUPS
SSD_CONV_DIM = SSD_INNER + 2 * SSD_GROUPS * SSD_STATE
CONV_COLS = 256

LANES = 128
LOG2_E = 1.4426950408889634
VMEM_LIMIT = 48 * 1024 * 1024

L0_GQ, L0_GK, L0_GV, L0_GG = 0, 512, 1024, 2048
L0_DQ, L0_DK, L0_DV, L0_DG = 3072, 4608, 5120, 5632
L0_MQ, L0_MG = 6144, 6656
L0_MAIN = 7168
L1_Z, L1_XBC, L1_MQ, L1_MG = 0, 1536, 4096, 4608
L1_MAIN = 5120


def _nt(a, b):
    return lax.dot_general(a, b, (((1,), (1,)), ((), ())), preferred_element_type=F32)


def _tn(a, b):
    return lax.dot_general(a, b, (((0,), (0,)), ((), ())), preferred_element_type=F32)


def _nn(a, b):
    return jnp.dot(a, b, preferred_element_type=F32)


def _cumsum_rows(tril, x):
    hi = x.astype(BF16)
    rest = x - hi.astype(F32)
    mid = rest.astype(BF16)
    lo = (rest - mid.astype(F32)).astype(BF16)
    return _nn(tril, hi) + _nn(tril, mid) + _nn(tril, lo)


def _silu(x):
    return 0.5 * x * (1.0 + jnp.tanh(0.5 * x))


def _cparams(sem):
    return pltpu.CompilerParams(dimension_semantics=sem, vmem_limit_bytes=VMEM_LIMIT)


def _rope_table_kernel(pos_ref, inv_ref, sign_ref, cos_ref, sin_ref):
    rows = pos_ref.shape[0]
    half = HEAD_DIM // 2
    ang = pos_ref[...].astype(F32) * inv_ref[...]
    cos = jnp.cos(ang)
    sin = jnp.sin(ang)
    lo = lax.broadcasted_iota(jnp.int32, (rows, HEAD_DIM), 1) < half
    sign = sign_ref[...]
    for table, ref, scale in ((cos, cos_ref, None), (sin, sin_ref, sign)):
        swapped = pltpu.roll(table, half, 1)
        first = jnp.where(lo, table, swapped)
        second = jnp.where(lo, swapped, table)
        if scale is not None:
            first, second = first * scale, second * scale
        ref[0:rows, :] = first
        ref[rows:2 * rows, :] = second


def _rope_tables(positions):
    n = positions.size
    half = HEAD_DIM // 2
    inv = 1.0 / (ROPE_THETA ** (jnp.arange(0, HEAD_DIM, 2, dtype=F32) / HEAD_DIM))
    inv2 = jnp.concatenate([inv, inv]).reshape(1, HEAD_DIM)
    sign = jnp.concatenate([-jnp.ones((half,), F32), jnp.ones((half,), F32)]).reshape(1, HEAD_DIM)
    tb = 2048
    pos_pair = positions.reshape(n // tb, 2, tb // 2).transpose(0, 2, 1)
    pos_pair = jnp.repeat(pos_pair, half, axis=2).reshape(n // 2, HEAD_DIM)
    row = pl.BlockSpec((tb, HEAD_DIM), lambda i: (i, 0))
    vec = pl.BlockSpec((1, HEAD_DIM), lambda i: (0, 0))
    return pl.pallas_call(
        _rope_table_kernel,
        grid=(n // tb,),
        in_specs=[pl.BlockSpec((tb // 2, HEAD_DIM), lambda i: (i, 0)), vec, vec],
        out_specs=[row, row],
        out_shape=[jax.ShapeDtypeStruct((n, HEAD_DIM), F32)] * 2,
        compiler_params=_cparams(("arbitrary",)),
        name="rope_tables",
    )(pos_pair, inv2, sign)


ROPE_SUB = 256


def _proj_kernel(*refs, split, rope_tiles, cast_x, tn):
    x_ref, wa_ref, wb_ref, ws_ref = refs[:4]
    if rope_tiles is None:
        o_ref, os_ref = refs[4:6]
        scratch = refs[6:]
    else:
        c_ref, s_ref, sc_ref, o_ref, os_ref = refs[4:9]
        scratch = refs[9:]
    xb_ref = scratch[0] if cast_x else x_ref
    j = pl.program_id(1)

    @pl.when(j == 0)
    def _():
        if cast_x:
            xb_ref[...] = x_ref[...].astype(BF16)
        os_ref[...] = _nn(xb_ref[...], ws_ref[...])

    def plain(w_ref):
        o_ref[...] = _nn(xb_ref[...], w_ref[...]).astype(o_ref.dtype)

    def rotary(w_ref):
        cos = c_ref[...]
        sin = s_ref[...]
        for c0 in range(0, tn, ROPE_SUB):
            acc = _nn(xb_ref[...], w_ref[:, c0:c0 + ROPE_SUB])
            for k in range(ROPE_SUB // HEAD_DIM):
                sl = slice(c0 + k * HEAD_DIM, c0 + (k + 1) * HEAD_DIM)
                a = acc[:, k * HEAD_DIM:(k + 1) * HEAD_DIM] * sc_ref[:, sl]
                o_ref[:, sl] = (a * cos + pltpu.roll(a, HEAD_DIM // 2, 1) * sin).astype(o_ref.dtype)

    in_first = j < split
    pl.when(in_first)(lambda: plain(wa_ref))
    if rope_tiles is None:
        pl.when(jnp.logical_not(in_first))(lambda: plain(wb_ref))
    else:
        lo, hi = rope_tiles
        is_rope = jnp.logical_and(j >= lo, j < hi)
        pl.when(is_rope)(lambda: rotary(wb_ref))
        pl.when(jnp.logical_and(jnp.logical_not(in_first), jnp.logical_not(is_rope)))(lambda: plain(wb_ref))


def _project(x, w_a, w_b, w_small, rope=None, tm=2048, tn=1024):
    n, kdim = x.shape
    m = w_a.shape[1] + w_b.shape[1]
    split = w_a.shape[1] // tn
    cast_x = x.dtype != BF16
    in_specs = [
        pl.BlockSpec((tm, kdim), lambda i, j: (i, 0)),
        pl.BlockSpec((kdim, tn), lambda i, j: (0, jnp.minimum(j, split - 1))),
        pl.BlockSpec((kdim, tn), lambda i, j: (0, jnp.maximum(j - split, 0))),
        pl.BlockSpec((kdim, LANES), lambda i, j: (0, 0)),
    ]
    args = [x, w_a, w_b, w_small]
    rope_tiles = None
    if rope is not None:
        cos, sin, col_scale, first_col, last_col = rope
        rope_tiles = (first_col // tn, last_col // tn)
        assert rope_tiles[0] >= split
        in_specs += [
            pl.BlockSpec((tm, HEAD_DIM), lambda i, j: (i, 0)),
            pl.BlockSpec((tm, HEAD_DIM), lambda i, j: (i, 0)),
            pl.BlockSpec((1, tn), lambda i, j: (0, j)),
        ]
        args += [cos, sin, col_scale]
    return pl.pallas_call(
        functools.partial(_proj_kernel, split=split, rope_tiles=rope_tiles, cast_x=cast_x, tn=tn),
        grid=(n // tm, m // tn),
        in_specs=in_specs,
        out_specs=[
            pl.BlockSpec((tm, tn), lambda i, j: (i, j)),
            pl.BlockSpec((tm, LANES), lambda i, j: (i, 0)),
        ],
        out_shape=[jax.ShapeDtypeStruct((n, m), BF16), jax.ShapeDtypeStruct((n, LANES), F32)],
        scratch_shapes=[pltpu.VMEM((tm, kdim), BF16)] if cast_x else [],
        compiler_params=_cparams(("arbitrary", "arbitrary")),
        name="in_proj",
    )(*args)


def _gla_kernel(q_ref, k_ref, v_ref, g_ref, ga_ref, wg_ref, bg_ref, nw_ref, o_ref, st_ref, *, n_chunks):
    c_ = GLA_CHUNK

    @pl.when(pl.program_id(1) == 0)
    def _():
        st_ref[...] = jnp.zeros_like(st_ref)

    row = lax.broadcasted_iota(jnp.int32, (c_, c_), 0)
    col = lax.broadcasted_iota(jnp.int32, (c_, c_), 1)
    causal = col <= row
    tril = jnp.where(causal, 1.0, 0.0).astype(BF16)
    wg = wg_ref[...]
    bg = bg_ref[...]
    nw = nw_ref[...]

    def chunk(ci, carry):
        r0 = pl.multiple_of(ci * c_, c_)
        rows = pl.ds(r0, c_)
        z = _nn(ga_ref[rows, :].astype(BF16), wg) + bg
        log_a = (jnp.minimum(z, 0.0) - jnp.log1p(jnp.exp(-jnp.abs(z)))) * (1.0 / GLA_GATE_NORMALIZER)
        bcum = _cumsum_rows(tril, log_a)
        blast = bcum[c_ - 1:c_, :]
        e_pos = jnp.exp(bcum)
        e_neg = jnp.exp(-bcum)
        e_tail = jnp.exp(blast - bcum)
        e_last = jnp.exp(blast)
        for h in range(GLA_HEADS):
            ks = slice(h * GLA_DK, (h + 1) * GLA_DK)
            vs = slice(h * GLA_DV, (h + 1) * GLA_DV)
            q = q_ref[rows, ks].astype(F32) * (GLA_DK ** -0.5)
            k = k_ref[rows, ks].astype(F32)
            q_dec = (q * e_pos[:, ks]).astype(BF16)
            k_inv = (k * e_neg[:, ks]).astype(BF16)
            k_tail = (k * e_tail[:, ks]).astype(BF16)
            v = v_ref[rows, vs]
            scores = jnp.where(causal, _nt(q_dec, k_inv), 0.0)
            st = st_ref[h]
            o = _nn(scores.astype(BF16), v) + _nt(q_dec, st.astype(BF16))
            st_ref[h] = st * e_last[:, ks] + _tn(v, k_tail)
            ms = jnp.mean(o * o, axis=-1, keepdims=True)
            o = o * lax.rsqrt(ms + RMS_EPS) * nw
            o_ref[rows, vs] = (o * _silu(g_ref[rows, vs].astype(F32))).astype(o_ref.dtype)
        return carry

    lax.fori_loop(0, n_chunks, chunk, 0, unroll=2)


def _gla(h_main, h_small, w_gate, b_gate, norm_w, batch, seq, tb=512):
    n = batch * seq
    nt = seq // tb
    wg = jnp.zeros((LANES, GLA_HEADS * GLA_DK), BF16).at[:GLA_GATE_RANK].set(w_gate.astype(BF16))
    idx = lambda c: (lambda b, t: (b * nt + t, c))
    const = lambda b, t: (0, 0)
    return pl.pallas_call(
        functools.partial(_gla_kernel, n_chunks=tb // GLA_CHUNK),
        grid=(batch, nt),
        in_specs=[
            pl.BlockSpec((tb, 512), idx(L0_GQ // 512)),
            pl.BlockSpec((tb, 512), idx(L0_GK // 512)),
            pl.BlockSpec((tb, 1024), idx(L0_GV // 1024)),
            pl.BlockSpec((tb, 1024), idx(L0_GG // 1024)),
            pl.BlockSpec((tb, LANES), idx(0)),
            pl.BlockSpec((LANES, 512), const),
            pl.BlockSpec((1, 512), const),
            pl.BlockSpec((1, GLA_DV), const),
        ],
        out_specs=pl.BlockSpec((tb, 1024), idx(0)),
        out_shape=jax.ShapeDtypeStruct((n, GLA_HEADS * GLA_DV), BF16),
        scratch_shapes=[pltpu.VMEM((GLA_HEADS, GLA_DV, GLA_DK), F32)],
        compiler_params=_cparams(("arbitrary", "arbitrary")),
        name="gla",
    )(h_main, h_main, h_main, h_main, h_small, wg, b_gate.reshape(1, -1), norm_w.reshape(1, -1))


DIL_WINDOW = 2048
SLAB_GROUP = 16
SLAB_PITCH = 24
SEG = 32
TILE_UNROLL = 8


def _dil_kernel(q0_ref, q1_ref, q2_ref, kc_ref, kp_ref, vc_ref, vp_ref, g_ref, out_ref,
                kcat, vcat, ks, vs, q1s, q2s, o0s, l0s, o1s, l1s, o2s, l2s, bias):
    blk = DIL_BLOCK
    w = DIL_WINDOW
    groups = w // SLAB_GROUP
    t = pl.program_id(1)
    first_window = t == 0

    rho = lax.broadcasted_iota(jnp.int32, (blk, 2 * blk), 0)
    kap = lax.broadcasted_iota(jnp.int32, (blk, 2 * blk), 1)
    in_prev = kap < blk
    kap_l = jnp.where(in_prev, kap, kap - blk)
    perm = lambda x: 4 * (x % SEG) + x // SEG
    neg = jnp.float32(-jnp.inf)
    for slot, (jq, jk) in enumerate(((rho, kap_l), (perm(rho), perm(kap_l)))):
        ok_prev = jnp.where(jk >= jq, 0.0, neg)
        ok_cur = jnp.where(jk <= jq, 0.0, neg)
        bias[2 * slot] = jnp.where(in_prev, ok_prev, ok_cur)
        bias[2 * slot + 1] = jnp.where(in_prev, neg, ok_cur)

    kcat[0:blk, :] = kp_ref[w - blk:w, :]
    kcat[blk:blk + w, :] = kc_ref[...]
    vcat[0:blk, :] = vp_ref[w - blk:w, :]
    vcat[blk:blk + w, :] = vc_ref[...]

    def to_slabs(i, carry):
        src = pl.ds(pl.multiple_of(i * SLAB_GROUP, SLAB_GROUP), SLAB_GROUP)
        prev = pl.ds(pl.multiple_of(i * SLAB_PITCH, 8), SLAB_GROUP)
        cur = pl.ds(pl.multiple_of((groups + i) * SLAB_PITCH, 8), SLAB_GROUP)
        ks[prev, :] = kp_ref[src, :].astype(F32)
        ks[cur, :] = kc_ref[src, :].astype(F32)
        vs[prev, :] = vp_ref[src, :].astype(F32)
        vs[cur, :] = vc_ref[src, :].astype(F32)
        q1s[prev, :] = q1_ref[src, :].astype(F32)
        q2s[prev, :] = q2_ref[src, :].astype(F32)
        return carry

    lax.fori_loop(0, groups, to_slabs, 0, unroll=4)

    def attend(q, k, v, b):
        s = _nt(q, k) + b
        m = jnp.max(s, -1, keepdims=True)
        p = jnp.exp2(s - m)
        den = jnp.sum(p, -1, keepdims=True)
        o = _nn(p.astype(BF16), v) * (1.0 / den)
        return o, jnp.broadcast_to(m + jnp.log2(den), (blk, LANES))

    def group0(i, carry):
        r0 = pl.multiple_of(i * blk, blk)
        sel = jnp.where(jnp.logical_and(first_window, i == 0), 1, 0)
        o, l = attend(q0_ref[pl.ds(r0, blk), :], kcat[pl.ds(r0, 2 * blk), :], vcat[pl.ds(r0, 2 * blk), :],
                      bias[sel])
        o0s[pl.ds(r0, blk), :] = o
        l0s[pl.ds(r0, blk), :] = l
        return carry

    lax.fori_loop(0, w // blk, group0, 0, unroll=TILE_UNROLL)

    def seg(ref, group, b):
        return ref[pl.ds(group * SLAB_PITCH + b, SEG, stride=SLAB_PITCH), :]

    def group1(idx, carry):
        r = idx // 4
        n = idx % 4
        g_cur = groups + SEG * n
        q = jnp.concatenate([seg(q1s, SEG * n, r + 4 * j) for j in range(4)], axis=0).astype(BF16)
        k = jnp.concatenate([seg(ks, g_cur - SEG, r + 4 * j) for j in range(4)]
                            + [seg(ks, g_cur, r + 4 * j) for j in range(4)], axis=0).astype(BF16)
        v = jnp.concatenate([seg(vs, g_cur - SEG, r + 4 * j) for j in range(4)]
                            + [seg(vs, g_cur, r + 4 * j) for j in range(4)], axis=0).astype(BF16)
        sel = jnp.where(jnp.logical_and(first_window, n == 0), 3, 2)
        o, l = attend(q, k, v, bias[sel])
        for j in range(4):
            dst = pl.ds(SEG * n * SLAB_PITCH + r + 4 * j, SEG, stride=SLAB_PITCH)
            o1s[dst, :] = o[j * SEG:(j + 1) * SEG]
            l1s[dst, :] = l[j * SEG:(j + 1) * SEG]
        return carry

    lax.fori_loop(0, 16, group1, 0, unroll=TILE_UNROLL)

    def group2(r, carry):
        prev = pl.ds(r, blk, stride=SLAB_PITCH)
        cur = pl.ds(groups * SLAB_PITCH + r, blk, stride=SLAB_PITCH)
        q = q2s[prev, :].astype(BF16)
        k = jnp.concatenate([ks[prev, :], ks[cur, :]], axis=0).astype(BF16)
        v = jnp.concatenate([vs[prev, :], vs[cur, :]], axis=0).astype(BF16)
        o, l = attend(q, k, v, bias[jnp.where(first_window, 1, 0)])
        o2s[prev, :] = o
        l2s[prev, :] = l
        return carry

    lax.fori_loop(0, SLAB_GROUP, group2, 0, unroll=TILE_UNROLL)

    def merge(i, carry):
        r0 = pl.multiple_of(i * blk, blk)
        rows = pl.ds(r0, blk)
        pieces = [pl.ds(pl.multiple_of((i * (blk // SLAB_GROUP) + j) * SLAB_PITCH, 8), SLAB_GROUP)
                  for j in range(blk // SLAB_GROUP)]
        gather = lambda ref: jnp.concatenate([ref[p, :] for p in pieces], axis=0)
        l0, l1, l2 = l0s[rows, :], gather(l1s), gather(l2s)
        m = jnp.maximum(jnp.maximum(l0, l1), l2)
        w0, w1, w2 = jnp.exp2(l0 - m), jnp.exp2(l1 - m), jnp.exp2(l2 - m)
        o = (w0 * o0s[rows, :] + w1 * gather(o1s) + w2 * gather(o2s)) * (1.0 / (w0 + w1 + w2))
        out_ref[rows, :] = (o * _silu(g_ref[rows, :].astype(F32))).astype(out_ref.dtype)
        return carry

    lax.fori_loop(0, w // blk, merge, 0)


def _dilated_attention(h_main, batch, seq):
    n = batch * seq
    w = DIL_WINDOW
    nt = seq // w
    slab_rows = (w // SLAB_GROUP) * SLAB_PITCH

    def cur(col):
        return pl.BlockSpec((w, HEAD_DIM), lambda b, t, h: (b * nt + t, col // HEAD_DIM + h))

    def prev(col):
        return pl.BlockSpec((w, HEAD_DIM), lambda b, t, h: (b * nt + jnp.maximum(t - 1, 0), col // HEAD_DIM + h))

    width = DIL_HEADS * HEAD_DIM
    slab = lambda rows: pltpu.VMEM((rows, LANES), F32)
    return pl.pallas_call(
        _dil_kernel,
        grid=(batch, nt, DIL_HEADS),
        in_specs=[cur(L0_DQ), cur(L0_DQ + width), cur(L0_DQ + 2 * width),
                  cur(L0_DK), prev(L0_DK), cur(L0_DV), prev(L0_DV), cur(L0_DG)],
        out_specs=pl.BlockSpec((w, HEAD_DIM), lambda b, t, h: (b * nt + t, h)),
        out_shape=jax.ShapeDtypeStruct((n, width), BF16),
        scratch_shapes=[
            pltpu.VMEM((w + DIL_BLOCK, HEAD_DIM), BF16), pltpu.VMEM((w + DIL_BLOCK, HEAD_DIM), BF16),
            slab(2 * slab_rows), slab(2 * slab_rows), slab(slab_rows), slab(slab_rows),
            slab(w), slab(w), slab(slab_rows), slab(slab_rows), slab(slab_rows), slab(slab_rows),
            pltpu.VMEM((4, DIL_BLOCK, 2 * DIL_BLOCK), F32),
        ],
        compiler_params=_cparams(("arbitrary", "arbitrary", "arbitrary")),
        name="dilated_attn",
    )(*([h_main] * 8))


def _mem_kv_kernel(m_ref, w_ref, o_ref):
    o_ref[...] = _nn(m_ref[...].astype(BF16), w_ref[...]).astype(o_ref.dtype)


def _mem_kv(mem2d, wk, wv):
    rows, kdim = mem2d.shape
    w = jnp.concatenate([wk, wv], axis=1).astype(BF16)
    tm = MEM_LEN
    return pl.pallas_call(
        _mem_kv_kernel,
        grid=(rows // tm,),
        in_specs=[pl.BlockSpec((tm, kdim), lambda i: (i, 0)), pl.BlockSpec(w.shape, lambda i: (0, 0))],
        out_specs=pl.BlockSpec((tm, w.shape[1]), lambda i: (i, 0)),
        out_shape=jax.ShapeDtypeStruct((rows, w.shape[1]), BF16),
        compiler_params=_cparams(("arbitrary",)),
        name="mem_kv",
    )(mem2d, w)


def _mem_attn_kernel(q_ref, g_ref, k_ref, v_ref, o_ref, *, tq):
    sub = 128
    for i in range(tq // sub):
        rows = slice(i * sub, (i + 1) * sub)
        for h in range(MEM_HEADS):
            hs = slice(h * HEAD_DIM, (h + 1) * HEAD_DIM)
            s = _nt(q_ref[rows, hs], k_ref[:, hs]) * (HEAD_DIM ** -0.5 * LOG2_E)
            m = jnp.max(s, -1, keepdims=True)
            p = jnp.exp2(s - m)
            den = jnp.sum(p, -1, keepdims=True)
            o = _nn(p.astype(BF16), v_ref[:, hs]) * (1.0 / den)
            o_ref[rows, hs] = (o * _silu(g_ref[rows, hs].astype(F32))).astype(o_ref.dtype)


def _mem_attn(h_main, kv, batch, seq, q_col, g_col, tq=512):
    n = batch * seq
    width = MEM_HEADS * HEAD_DIM
    nt = seq // tq
    return pl.pallas_call(
        functools.partial(_mem_attn_kernel, tq=tq),
        grid=(batch, nt),
        in_specs=[
            pl.BlockSpec((tq, width), lambda b, t: (b * nt + t, q_col // width)),
            pl.BlockSpec((tq, width), lambda b, t: (b * nt + t, g_col // width)),
            pl.BlockSpec((MEM_LEN, width), lambda b, t: (b, 0)),
            pl.BlockSpec((MEM_LEN, width), lambda b, t: (b, 1)),
        ],
        out_specs=pl.BlockSpec((tq, width), lambda b, t: (b * nt + t, 0)),
        out_shape=jax.ShapeDtypeStruct((n, width), BF16),
        compiler_params=_cparams(("arbitrary", "arbitrary")),
        name="mem_attn",
    )(h_main, h_main, kv, kv)


OUT_SUB_ROWS = 256


def _out_kernel(*refs, n_parts, emit_bf16):
    a_refs = refs[:n_parts]
    w_refs = refs[n_parts:2 * n_parts]
    x_ref, lw_ref, lb_ref, o_ref = refs[2 * n_parts:2 * n_parts + 4]
    for r0 in range(0, x_ref.shape[0], OUT_SUB_ROWS):
        rows = slice(r0, r0 + OUT_SUB_ROWS)
        acc = DEEPNORM_ALPHA * x_ref[rows, :]
        for a_ref, w_ref in zip(a_refs, w_refs):
            acc = acc + _nn(a_ref[rows, :], w_ref[...])
        mu = jnp.mean(acc, -1, keepdims=True)
        d = acc - mu
        var = jnp.mean(d * d, -1, keepdims=True)
        y = d * lax.rsqrt(var + LN_EPS) * lw_ref[...] + lb_ref[...]
        o_ref[rows, :] = y
        if emit_bf16:
            refs[2 * n_parts + 4][rows, :] = y.astype(BF16)


def _out_proj_ln(parts, w_out, x, ln_w, ln_b, emit_bf16=False, tm=1024):
    n, dm = x.shape
    w_bf = w_out.astype(BF16)
    w_specs, off = [], 0
    for p in parts:
        width = p.shape[1]
        w_specs.append(pl.BlockSpec((width, dm), functools.partial(lambda i, blk: (blk, 0), blk=off // width)))
        assert off % width == 0
        off += width
    row = lambda width: pl.BlockSpec((tm, width), lambda i: (i, 0))
    const = lambda shape: pl.BlockSpec(shape, lambda i: (0, 0))
    out_specs = [row(dm)] + ([row(dm)] if emit_bf16 else [])
    out_shape = [jax.ShapeDtypeStruct((n, dm), F32)] + ([jax.ShapeDtypeStruct((n, dm), BF16)] if emit_bf16 else [])
    return pl.pallas_call(
        functools.partial(_out_kernel, n_parts=len(parts), emit_bf16=emit_bf16),
        grid=(n // tm,),
        in_specs=[row(p.shape[1]) for p in parts] + w_specs + [row(dm), const((1, dm)), const((1, dm))],
        out_specs=out_specs,
        out_shape=out_shape,
        compiler_params=_cparams(("arbitrary",)),
        name="out_proj_ln",
    )(*parts, *([w_bf] * len(parts)), x, ln_w.reshape(1, dm), ln_b.reshape(1, dm))


def _ssd_kernel(z_ref, xs_ref, bm_ref, cm_ref, dt_ref, cw_ref, cb_ref, dtb_ref, a_ref, dsk_ref, nw_ref,
                o_ref, u_ref, pair_ref, lag_ref, act_ref, st_ref, *, n_chunks):
    q_ = SSD_CHUNK
    halo = 8

    @pl.when(pl.program_id(1) == 0)
    def _():
        st_ref[...] = jnp.zeros_like(st_ref)
        u_ref[0:halo, :] = jnp.zeros((halo, SSD_CONV_DIM), F32)
        pair_ref[0:halo, :] = jnp.zeros((halo, SSD_CONV_DIM), F32)

    row = lax.broadcasted_iota(jnp.int32, (q_, q_), 0)
    col = lax.broadcasted_iota(jnp.int32, (q_, q_), 1)
    causal = col <= row
    tril = jnp.where(causal, 1.0, 0.0).astype(BF16)
    lane = lax.broadcasted_iota(jnp.int32, (q_, LANES), 1)
    lo_half = lane < SSD_HEAD_DIM
    lane_row = lax.broadcasted_iota(jnp.int32, (1, LANES), 1)
    lo_half_row = lane_row < SSD_HEAD_DIM
    lo_half_f = lo_half.astype(F32)
    a_log2 = -jnp.exp(a_ref[...]) * LOG2_E

    def row_bcast(t, r):
        return jnp.tile(jnp.broadcast_to(t[r:r + 1, :], (8, q_)), (q_ // 8, 1))

    def chunk(ci, carry):
        r0 = pl.multiple_of(ci * q_, q_)
        rows = pl.ds(r0, q_)
        bc_w = SSD_GROUPS * SSD_STATE
        u_ref[halo:halo + q_, 0:SSD_INNER] = xs_ref[rows, :].astype(F32)
        u_ref[halo:halo + q_, SSD_INNER:SSD_INNER + bc_w] = bm_ref[rows, :].astype(F32)
        u_ref[halo:halo + q_, SSD_INNER + bc_w:SSD_CONV_DIM] = cm_ref[rows, :].astype(F32)
        for c0 in range(0, SSD_CONV_DIM, CONV_COLS):
            cs = slice(c0, c0 + CONV_COLS)
            u_cur = u_ref[halo:halo + q_, cs]
            lag_ref[...] = u_ref[halo - 1:halo - 1 + q_, cs]
            u_lag = lag_ref[...]
            pair_ref[halo:halo + q_, cs] = cw_ref[1:2, cs] * u_cur + cw_ref[0:1, cs] * u_lag
            conv = ((cb_ref[:, cs] + cw_ref[3:4, cs] * u_cur) + cw_ref[2:3, cs] * u_lag
                    + pair_ref[halo - 2:halo - 2 + q_, cs])
            act_ref[:, cs] = _silu(conv)
        u_ref[0:halo, :] = u_ref[q_:q_ + halo, :]
        pair_ref[0:halo, :] = pair_ref[q_:q_ + halo, :]
        dt_in = dt_ref[rows, :] + dtb_ref[...]
        dt = jnp.maximum(dt_in, 0.0) + jnp.log1p(jnp.exp(-jnp.abs(dt_in)))
        acum = _cumsum_rows(tril, dt * a_log2)
        acum_t = acum.T
        dt_t = dt.T
        ldt_t = jnp.log2(dt_t)
        rowp_t = acum_t - ldt_t
        w_t = dt_t * jnp.exp2(acum_t[:, q_ - 1:q_] - acum_t)
        c_decay = jnp.exp2(acum[q_ - 1:q_, :])
        for g in range(SSD_GROUPS):
            b_f = act_ref[:, SSD_INNER + g * SSD_STATE:SSD_INNER + (g + 1) * SSD_STATE]
            c_f = act_ref[:, SSD_INNER + (SSD_GROUPS + g) * SSD_STATE:SSD_INNER + (SSD_GROUPS + g + 1) * SSD_STATE]
            c_bf = c_f.astype(BF16)
            cbm = jnp.where(causal, _nt(c_bf, b_f.astype(BF16)), 0.0)
            b_t = b_f.T
            st = st_ref[g]
            y_off_all = _nn(c_bf, st.astype(BF16))
            ys, news, decs = [], [], []
            ss = jnp.zeros((q_, 1), F32)
            for j in range(SSD_GROUP_WIDTH // LANES):
                h1 = (g * SSD_GROUP_WIDTH + j * LANES) // SSD_HEAD_DIM
                ps = slice(g * SSD_GROUP_WIDTH + j * LANES, g * SSD_GROUP_WIDTH + (j + 1) * LANES)
                xs = act_ref[:, ps]
                intra, inter, e_l = [], [], []
                for hh in (h1, h1 + 1):
                    colb = jnp.broadcast_to(acum[:, hh:hh + 1], (q_, q_))
                    decay = jnp.exp2(jnp.minimum(colb - row_bcast(rowp_t, hh), row_bcast(ldt_t, hh)))
                    intra.append((cbm * decay).astype(BF16))
                    inter.append((b_t * row_bcast(w_t, hh)).astype(BF16))
                    e_l.append(jnp.exp2(colb))
                lhs = jnp.concatenate([jnp.concatenate(intra, axis=1), jnp.concatenate(inter, axis=1)], axis=0)
                xs_lo = xs * lo_half_f
                rhs = jnp.concatenate([xs_lo, xs - xs_lo], axis=0).astype(BF16)
                prod = _nn(lhs, rhs)
                y = prod[0:q_] + y_off_all[:, j * LANES:(j + 1) * LANES] * jnp.where(lo_half, e_l[0], e_l[1])
                y = y + dsk_ref[:, ps] * xs
                y = y * _silu(z_ref[rows, ps].astype(F32))
                ss = ss + jnp.sum(y * y, -1, keepdims=True)
                ys.append(y)
                news.append(prod[q_:2 * q_])
                decs.append(jnp.where(lo_half_row, c_decay[:, h1:h1 + 1], c_decay[:, h1 + 1:h1 + 2]))
            st_ref[g] = st * jnp.concatenate(decs, axis=1) + jnp.concatenate(news, axis=1)
            inv = lax.rsqrt(ss * (1.0 / SSD_GROUP_WIDTH) + RMS_EPS)
            for j, y in enumerate(ys):
                ps = slice(g * SSD_GROUP_WIDTH + j * LANES, g * SSD_GROUP_WIDTH + (j + 1) * LANES)
                o_ref[rows, ps] = (y * inv * nw_ref[:, ps]).astype(o_ref.dtype)
        return carry

    lax.fori_loop(0, n_chunks, chunk, 0)


def _ssd(h_main, h_small, conv_w, conv_b, dt_bias, a_log, d_skip, norm_w, batch, seq, tb=512):
    n = batch * seq
    nt = seq // tb
    bc_w = SSD_GROUPS * SSD_STATE
    pad = lambda v: jnp.zeros((1, LANES), F32).at[0, :SSD_HEADS].set(v.astype(F32))
    idx = lambda c: (lambda b, t: (b * nt + t, c))
    const = lambda b, t: (0, 0)
    return pl.pallas_call(
        functools.partial(_ssd_kernel, n_chunks=tb // SSD_CHUNK),
        grid=(batch, nt),
        in_specs=[
            pl.BlockSpec((tb, SSD_INNER), idx(L1_Z // SSD_INNER)),
            pl.BlockSpec((tb, SSD_INNER), idx(L1_XBC // SSD_INNER)),
            pl.BlockSpec((tb, bc_w), idx((L1_XBC + SSD_INNER) // bc_w)),
            pl.BlockSpec((tb, bc_w), idx((L1_XBC + SSD_INNER + bc_w) // bc_w)),
            pl.BlockSpec((tb, LANES), idx(0)),
            pl.BlockSpec((SSD_CONV, SSD_CONV_DIM), const),
            pl.BlockSpec((1, SSD_CONV_DIM), const),
            pl.BlockSpec((1, LANES), const),
            pl.BlockSpec((1, LANES), const),
            pl.BlockSpec((1, SSD_INNER), const),
            pl.BlockSpec((1, SSD_INNER), const),
        ],
        out_specs=pl.BlockSpec((tb, SSD_INNER), idx(0)),
        out_shape=jax.ShapeDtypeStruct((n, SSD_INNER), BF16),
        scratch_shapes=[
            pltpu.VMEM((SSD_CHUNK + 8, SSD_CONV_DIM), F32),
            pltpu.VMEM((SSD_CHUNK + 8, SSD_CONV_DIM), F32),
            pltpu.VMEM((SSD_CHUNK, CONV_COLS), F32),
            pltpu.VMEM((SSD_CHUNK, SSD_CONV_DIM), F32),
            pltpu.VMEM((SSD_GROUPS, SSD_STATE, SSD_GROUP_WIDTH), F32),
        ],
        compiler_params=_cparams(("arbitrary", "arbitrary")),
        name="ssd",
    )(h_main, h_main, h_main, h_main, h_small, conv_w, conv_b.reshape(1, -1),
      pad(dt_bias), pad(a_log), jnp.repeat(d_skip.astype(F32), SSD_HEAD_DIM).reshape(1, -1),
      norm_w.reshape(1, -1))


def _pad_cols(w, width=LANES):
    return jnp.zeros((w.shape[0], width), BF16).at[:, :w.shape[1]].set(w.astype(BF16))


def kernel(x, mem, positions, l0_w_in, l0_gla_w_gate, l0_gla_b_gate, l0_gla_norm_w, l0_mem_wk, l0_mem_wv, l0_w_out, l0_ln_w, l0_ln_b, l1_w_in, l1_conv_w, l1_conv_b, l1_dt_bias, l1_a_log, l1_d_skip, l1_ssd_norm_w, l1_mem_wk, l1_mem_wv, l1_w_out, l1_ln_w, l1_ln_b):
    batch, seq, dm = x.shape
    n = batch * seq
    x2 = x.reshape(n, dm)
    mem2 = mem.reshape(batch * MEM_LEN, dm)

    ga0 = L0_DQ + GLA_GATE_RANK
    w0_a = l0_w_in[:, :L0_DQ].astype(BF16)
    w0_b = l0_w_in[:, ga0:].astype(BF16)
    w0_small = _pad_cols(l0_w_in[:, L0_DQ:ga0])
    cos, sin = _rope_tables(positions)
    col_scale = jnp.ones((1, L0_MAIN), F32).at[0, L0_DQ:L0_DK].set(HEAD_DIM ** -0.5 * LOG2_E)
    h0, h0s = _project(x2, w0_a, w0_b, w0_small, rope=(cos, sin, col_scale, L0_DQ, L0_DV), tm=1024)
    o_a = _gla(h0, h0s, l0_gla_w_gate, l0_gla_b_gate, l0_gla_norm_w, batch, seq)
    o_b = _dilated_attention(h0, batch, seq)
    kv0 = _mem_kv(mem2, l0_mem_wk, l0_mem_wv)
    o_m = _mem_attn(h0, kv0, batch, seq, L0_MQ, L0_MG)
    x1, x1_bf = _out_proj_ln([o_a, o_b, o_m], l0_w_out, x2, l0_ln_w, l0_ln_b, emit_bf16=True)

    dt0 = SSD_INNER + SSD_CONV_DIM
    w1_a = l1_w_in[:, :dt0].astype(BF16)
    w1_b = l1_w_in[:, dt0 + SSD_HEADS:].astype(BF16)
    w1_small = _pad_cols(l1_w_in[:, dt0:dt0 + SSD_HEADS])
    h1, h1s = _project(x1_bf, w1_a, w1_b, w1_small)
    y = _ssd(h1, h1s, l1_conv_w, l1_conv_b, l1_dt_bias, l1_a_log, l1_d_skip, l1_ssd_norm_w, batch, seq)
    kv1 = _mem_kv(mem2, l1_mem_wk, l1_mem_wv)
    o_m1 = _mem_attn(h1, kv1, batch, seq, L1_MQ, L1_MG)
    (x2_out,) = _out_proj_ln([y, o_m1], l1_w_out, x1, l1_ln_w, l1_ln_b)
    return x2_out.reshape(batch, seq, dm)
```

```python
import functools

import jax
import jax.numpy as jnp
from jax import lax
from jax.experimental import pallas as pl
from jax.experimental.pallas import tpu as pltpu

F32 = jnp.float32
BF16 = jnp.bfloat16

DEPTH = 2
DEEPNORM_ALPHA = (2 * DEPTH) ** 0.25
LN_EPS = 1e-5
RMS_EPS = 1e-6
ROPE_THETA = 10000.0
MEM_LEN = 256

GLA_HEADS = 4
GLA_DK = 128
GLA_DV = 256
GLA_GATE_RANK = 16
GLA_GATE_NORMALIZER = 16.0
GLA_CHUNK = 64

DIL_PATTERNS = ((128, 1), (512, 4), (2048, 16))
DIL_HEADS = 4
HEAD_DIM = 128
DIL_BLOCK = 128

MEM_HEADS = 4

SSD_HEADS = 24
SSD_HEAD_DIM = 64
SSD_GROUPS = 4
SSD_STATE = 128
SSD_CONV = 4
SSD_CHUNK = 128
SSD_INNER = SSD_HEADS * SSD_HEAD_DIM
SSD_GROUP_WIDTH = SSD_INNER // SSD_GROUPS
SSD_CONV_DIM = SSD_INNER + 2 * SSD_GROUPS * SSD_STATE
CONV_COLS = 256

LANES = 128
LOG2_E = 1.4426950408889634
VMEM_LIMIT = 48 * 1024 * 1024

L0_GQ, L0_GK, L0_GV, L0_GG = 0, 512, 1024, 2048
L0_DQ, L0_DK, L0_DV, L0_DG = 3072, 4608, 5120, 5632
L0_MQ, L0_MG = 6144, 6656
L0_MAIN = 7168
L1_Z, L1_XBC, L1_MQ, L1_MG = 0, 1536, 4096, 4608
L1_MAIN = 5120


def _nt(a, b):
    return lax.dot_general(a, b, (((1,), (1,)), ((), ())), preferred_element_type=F32)


def _tn(a, b):
    return lax.dot_general(a, b, (((0,), (0,)), ((), ())), preferred_element_type=F32)


def _nn(a, b):
    return jnp.dot(a, b, preferred_element_type=F32)


def _cumsum_rows(tril, x):
    hi = x.astype(BF16)
    rest = x - hi.astype(F32)
    mid = rest.astype(BF16)
    lo = (rest - mid.astype(F32)).astype(BF16)
    return _nn(tril, hi) + _nn(tril, mid) + _nn(tril, lo)


def _silu(x):
    return 0.5 * x * (1.0 + jnp.tanh(0.5 * x))


def _cparams(sem):
    return pltpu.CompilerParams(dimension_semantics=sem, vmem_limit_bytes=VMEM_LIMIT)


def _rope_table_kernel(pos_ref, inv_ref, sign_ref, cos_ref, sin_ref):
    rows = pos_ref.shape[0]
    half = HEAD_DIM // 2
    ang = pos_ref[...].astype(F32) * inv_ref[...]
    cos = jnp.cos(ang)
    sin = jnp.sin(ang)
    lo = lax.broadcasted_iota(jnp.int32, (rows, HEAD_DIM), 1) < half
    sign = sign_ref[...]
    for table, ref, scale in ((cos, cos_ref, None), (sin, sin_ref, sign)):
        swapped = pltpu.roll(table, half, 1)
        first = jnp.where(lo, table, swapped)
        second = jnp.where(lo, swapped, table)
        if scale is not None:
            first, second = first * scale, second * scale
        ref[0:rows, :] = first
        ref[rows:2 * rows, :] = second


def _rope_tables(positions):
    n = positions.size
    half = HEAD_DIM // 2
    inv = 1.0 / (ROPE_THETA ** (jnp.arange(0, HEAD_DIM, 2, dtype=F32) / HEAD_DIM))
    inv2 = jnp.concatenate([inv, inv]).reshape(1, HEAD_DIM)
    sign = jnp.concatenate([-jnp.ones((half,), F32), jnp.ones((half,), F32)]).reshape(1, HEAD_DIM)
    tb = 2048
    pos_pair = positions.reshape(n // tb, 2, tb // 2).transpose(0, 2, 1)
    pos_pair = jnp.repeat(pos_pair, half, axis=2).reshape(n // 2, HEAD_DIM)
    row = pl.BlockSpec((tb, HEAD_DIM), lambda i: (i, 0))
    vec = pl.BlockSpec((1, HEAD_DIM), lambda i: (0, 0))
    return pl.pallas_call(
        _rope_table_kernel,
        grid=(n // tb,),
        in_specs=[pl.BlockSpec((tb // 2, HEAD_DIM), lambda i: (i, 0)), vec, vec],
        out_specs=[row, row],
        out_shape=[jax.ShapeDtypeStruct((n, HEAD_DIM), F32)] * 2,
        compiler_params=_cparams(("arbitrary",)),
        name="rope_tables",
    )(pos_pair, inv2, sign)


SUB_COLS = 256


def _proj_kernel(*refs, groups, cast_x, has_rope):
    it = iter(refs)
    x_ref, wa_ref, wb_ref, ws_ref = next(it), next(it), next(it), next(it)
    if has_rope:
        c_ref, s_ref, sc_ref = next(it), next(it), next(it)
    o_ref, os_ref = next(it), next(it)
    xb_ref = next(it) if cast_x else x_ref
    j = pl.program_id(1)

    @pl.when(j == 0)
    def _():
        if cast_x:
            xb_ref[...] = x_ref[...].astype(BF16)
        os_ref[...] = _nn(xb_ref[...], ws_ref[...])

    def rope_block(acc, cols):
        cos, sin = c_ref[...], s_ref[...]
        for k in range(SUB_COLS // HEAD_DIM):
            sl = slice(cols.start + k * HEAD_DIM, cols.start + (k + 1) * HEAD_DIM)
            a = acc[:, k * HEAD_DIM:(k + 1) * HEAD_DIM] * sc_ref[:, sl]
            o_ref[:, sl] = (a * cos + pltpu.roll(a, HEAD_DIM // 2, 1) * sin).astype(o_ref.dtype)

    def emit(w_ref, kinds):
        if kinds is None:
            o_ref[...] = _nn(xb_ref[...], w_ref[...]).astype(o_ref.dtype)
            return
        for sb, kind in enumerate(kinds):
            cols = slice(sb * SUB_COLS, (sb + 1) * SUB_COLS)
            acc = _nn(xb_ref[...], w_ref[:, cols])
            if kind == "rope":
                rope_block(acc, cols)
            else:
                o_ref[:, cols] = acc.astype(o_ref.dtype)

    for lo, hi, src, kinds in groups:
        pl.when(jnp.logical_and(j >= lo, j < hi))(functools.partial(emit, wb_ref if src else wa_ref, kinds))


def _project(x, w_a, w_b, w_small, rope=None, tm=2048, tn=1024):
    n, kdim = x.shape
    m = w_a.shape[1] + w_b.shape[1]
    split = w_a.shape[1] // tn
    n_tiles = m // tn
    cast_x = x.dtype != BF16
    per_tile = tn // SUB_COLS
    kinds = [["plain"] * per_tile for _ in range(n_tiles)]
    in_specs = [
        pl.BlockSpec((tm, kdim), lambda i, j: (i, 0)),
        pl.BlockSpec((kdim, tn), lambda i, j: (0, jnp.minimum(j, split - 1))),
        pl.BlockSpec((kdim, tn), lambda i, j: (0, jnp.maximum(j - split, 0))),
        pl.BlockSpec((kdim, LANES), lambda i, j: (0, 0)),
    ]
    args = [x, w_a, w_b, w_small]
    scratch = [pltpu.VMEM((tm, kdim), BF16)] if cast_x else []
    if rope is not None:
        cos, sin, col_scale, first_col, last_col = rope
        for c in range(first_col // SUB_COLS, last_col // SUB_COLS):
            kinds[c // per_tile][c % per_tile] = "rope"
        in_specs += [
            pl.BlockSpec((tm, HEAD_DIM), lambda i, j: (i, 0)),
            pl.BlockSpec((tm, HEAD_DIM), lambda i, j: (i, 0)),
            pl.BlockSpec((1, tn), lambda i, j: (0, j)),
        ]
        args += [cos, sin, col_scale]
    groups = []
    for t in range(n_tiles):
        plan = None if all(k == "plain" for k in kinds[t]) else tuple(kinds[t])
        src = int(t >= split)
        if groups and groups[-1][2] == src and groups[-1][3] == plan:
            groups[-1] = (groups[-1][0], t + 1, src, plan)
        else:
            groups.append((t, t + 1, src, plan))
    return pl.pallas_call(
        functools.partial(_proj_kernel, groups=tuple(groups), cast_x=cast_x, has_rope=rope is not None),
        grid=(n // tm, n_tiles),
        in_specs=in_specs,
        out_specs=[
            pl.BlockSpec((tm, tn), lambda i, j: (i, j)),
            pl.BlockSpec((tm, LANES), lambda i, j: (i, 0)),
        ],
        out_shape=[jax.ShapeDtypeStruct((n, m), BF16), jax.ShapeDtypeStruct((n, LANES), F32)],
        scratch_shapes=scratch,
        compiler_params=_cparams(("arbitrary", "arbitrary")),
        name="in_proj",
    )(*args)


def _gla_kernel(q_ref, k_ref, v_ref, g_ref, ga_ref, wg_ref, bg_ref, nw_ref, o_ref, st_ref, la_ref, *, n_chunks):
    c_ = GLA_CHUNK

    @pl.when(pl.program_id(1) == 0)
    def _():
        st_ref[...] = jnp.zeros_like(st_ref)

    row = lax.broadcasted_iota(jnp.int32, (c_, c_), 0)
    col = lax.broadcasted_iota(jnp.int32, (c_, c_), 1)
    causal = col <= row
    tril = jnp.where(causal, 1.0, 0.0).astype(BF16)
    nw = nw_ref[...]
    z = _nn(ga_ref[...].astype(BF16), wg_ref[...]) + bg_ref[...]
    la_ref[...] = (jnp.minimum(z, 0.0) - jnp.log1p(jnp.exp(-jnp.abs(z)))) * (1.0 / GLA_GATE_NORMALIZER)

    def chunk(ci, carry):
        r0 = pl.multiple_of(ci * c_, c_)
        rows = pl.ds(r0, c_)
        bcum = _cumsum_rows(tril, la_ref[rows, :])
        blast = bcum[c_ - 1:c_, :]
        e_pos = jnp.exp(bcum)
        e_neg = jnp.exp(-bcum)
        e_tail = jnp.exp(blast - bcum)
        e_last = jnp.exp(blast)
        for h in range(GLA_HEADS):
            ks = slice(h * GLA_DK, (h + 1) * GLA_DK)
            vs = slice(h * GLA_DV, (h + 1) * GLA_DV)
            q = q_ref[rows, ks].astype(F32) * (GLA_DK ** -0.5)
            k = k_ref[rows, ks].astype(F32)
            q_dec = (q * e_pos[:, ks]).astype(BF16)
            k_inv = (k * e_neg[:, ks]).astype(BF16)
            k_tail = (k * e_tail[:, ks]).astype(BF16)
            v = v_ref[rows, vs]
            scores = jnp.where(causal, _nt(q_dec, k_inv), 0.0)
            st = st_ref[h]
            o = _nn(scores.astype(BF16), v) + _nt(q_dec, st.astype(BF16))
            st_ref[h] = st * e_last[:, ks] + _tn(v, k_tail)
            ms = jnp.mean(o * o, axis=-1, keepdims=True)
            o = o * lax.rsqrt(ms + RMS_EPS) * nw
            o_ref[rows, vs] = (o * _silu(g_ref[rows, vs].astype(F32))).astype(o_ref.dtype)
        return carry

    lax.fori_loop(0, n_chunks, chunk, 0, unroll=2)


def _gla(h_main, h_small, w_gate, b_gate, norm_w, batch, seq, tb=512):
    n = batch * seq
    nt = seq // tb
    wg = jnp.zeros((LANES, GLA_HEADS * GLA_DK), BF16).at[:GLA_GATE_RANK].set(w_gate.astype(BF16))
    idx = lambda c: (lambda b, t: (b * nt + t, c))
    const = lambda b, t: (0, 0)
    return pl.pallas_call(
        functools.partial(_gla_kernel, n_chunks=tb // GLA_CHUNK),
        grid=(batch, nt),
        in_specs=[
            pl.BlockSpec((tb, 512), idx(L0_GQ // 512)),
            pl.BlockSpec((tb, 512), idx(L0_GK // 512)),
            pl.BlockSpec((tb, 1024), idx(L0_GV // 1024)),
            pl.BlockSpec((tb, 1024), idx(L0_GG // 1024)),
            pl.BlockSpec((tb, LANES), idx(0)),
            pl.BlockSpec((LANES, 512), const),
            pl.BlockSpec((1, 512), const),
            pl.BlockSpec((1, GLA_DV), const),
        ],
        out_specs=pl.BlockSpec((tb, 1024), idx(0)),
        out_shape=jax.ShapeDtypeStruct((n, GLA_HEADS * GLA_DV), BF16),
        scratch_shapes=[pltpu.VMEM((GLA_HEADS, GLA_DV, GLA_DK), F32),
                        pltpu.VMEM((tb, GLA_HEADS * GLA_DK), F32)],
        compiler_params=_cparams(("arbitrary", "arbitrary")),
        name="gla",
    )(h_main, h_main, h_main, h_main, h_small, wg, b_gate.reshape(1, -1), norm_w.reshape(1, -1))


DIL_WINDOW = 2048
SLAB_GROUP = 16
SLAB_PITCH = 24
SEG = 32
TILE_UNROLL = 8


def _dil_kernel(q0_ref, q1_ref, q2_ref, kc_ref, kp_ref, vc_ref, vp_ref, g_ref, out_ref,
                kcat, vcat, ks, vs, q1s, q2s, o0s, l0s, o1s, l1s, o2s, l2s, bias):
    blk = DIL_BLOCK
    w = DIL_WINDOW
    groups = w // SLAB_GROUP
    t = pl.program_id(1)
    first_window = t == 0

    @pl.when(jnp.logical_and(jnp.logical_and(pl.program_id(0) == 0, t == 0), pl.program_id(2) == 0))
    def _():
        rho = lax.broadcasted_iota(jnp.int32, (blk, 2 * blk), 0)
        kap = lax.broadcasted_iota(jnp.int32, (blk, 2 * blk), 1)
        in_prev = kap < blk
        kap_l = jnp.where(in_prev, kap, kap - blk)
        perm = lambda x: 4 * (x % SEG) + x // SEG
        neg = jnp.float32(-jnp.inf)
        for slot, (jq, jk) in enumerate(((rho, kap_l), (perm(rho), perm(kap_l)))):
            ok_prev = jnp.where(jk >= jq, 0.0, neg)
            ok_cur = jnp.where(jk <= jq, 0.0, neg)
            bias[2 * slot] = jnp.where(in_prev, ok_prev, ok_cur)
            bias[2 * slot + 1] = jnp.where(in_prev, neg, ok_cur)

    kcat[0:blk, :] = kp_ref[w - blk:w, :]
    kcat[blk:blk + w, :] = kc_ref[...]
    vcat[0:blk, :] = vp_ref[w - blk:w, :]
    vcat[blk:blk + w, :] = vc_ref[...]

    def to_slabs(i, carry):
        src = pl.ds(pl.multiple_of(i * SLAB_GROUP, SLAB_GROUP), SLAB_GROUP)
        prev = pl.ds(pl.multiple_of(i * SLAB_PITCH, 8), SLAB_GROUP)
        cur = pl.ds(pl.multiple_of((groups + i) * SLAB_PITCH, 8), SLAB_GROUP)
        ks[prev, :] = kp_ref[src, :].astype(F32)
        ks[cur, :] = kc_ref[src, :].astype(F32)
        vs[prev, :] = vp_ref[src, :].astype(F32)
        vs[cur, :] = vc_ref[src, :].astype(F32)
        q1s[prev, :] = q1_ref[src, :].astype(F32)
        q2s[prev, :] = q2_ref[src, :].astype(F32)
        return carry

    lax.fori_loop(0, groups, to_slabs, 0, unroll=4)

    def attend(q, k, v, b):
        s = _nt(q, k) + b
        m = jnp.max(s, -1, keepdims=True)
        p = jnp.exp2(s - m)
        den = jnp.sum(p, -1, keepdims=True)
        o = _nn(p.astype(BF16), v) * (1.0 / den)
        return o, jnp.broadcast_to(m + jnp.log2(den), (blk, LANES))

    def group0(i, carry):
        r0 = pl.multiple_of(i * blk, blk)
        sel = jnp.where(jnp.logical_and(first_window, i == 0), 1, 0)
        o, l = attend(q0_ref[pl.ds(r0, blk), :], kcat[pl.ds(r0, 2 * blk), :], vcat[pl.ds(r0, 2 * blk), :],
                      bias[sel])
        o0s[pl.ds(r0, blk), :] = o
        l0s[pl.ds(r0, blk), :] = l
        return carry

    lax.fori_loop(0, w // blk, group0, 0, unroll=TILE_UNROLL)

    def seg(ref, group, b):
        return ref[pl.ds(group * SLAB_PITCH + b, SEG, stride=SLAB_PITCH), :]

    def group1(idx, carry):
        r = idx // 4
        n = idx % 4
        g_cur = groups + SEG * n
        q = jnp.concatenate([seg(q1s, SEG * n, r + 4 * j) for j in range(4)], axis=0).astype(BF16)
        k = jnp.concatenate([seg(ks, g_cur - SEG, r + 4 * j) for j in range(4)]
                            + [seg(ks, g_cur, r + 4 * j) for j in range(4)], axis=0).astype(BF16)
        v = jnp.concatenate([seg(vs, g_cur - SEG, r + 4 * j) for j in range(4)]
                            + [seg(vs, g_cur, r + 4 * j) for j in range(4)], axis=0).astype(BF16)
        sel = jnp.where(jnp.logical_and(first_window, n == 0), 3, 2)
        o, l = attend(q, k, v, bias[sel])
        for j in range(4):
            dst = pl.ds(SEG * n * SLAB_PITCH + r + 4 * j, SEG, stride=SLAB_PITCH)
            o1s[dst, :] = o[j * SEG:(j + 1) * SEG]
            l1s[dst, :] = l[j * SEG:(j + 1) * SEG]
        return carry

    lax.fori_loop(0, 16, group1, 0, unroll=TILE_UNROLL)

    def group2(r, carry):
        prev = pl.ds(r, blk, stride=SLAB_PITCH)
        cur = pl.ds(groups * SLAB_PITCH + r, blk, stride=SLAB_PITCH)
        q = q2s[prev, :].astype(BF16)
        k = jnp.concatenate([ks[prev, :], ks[cur, :]], axis=0).astype(BF16)
        v = jnp.concatenate([vs[prev, :], vs[cur, :]], axis=0).astype(BF16)
        o, l = attend(q, k, v, bias[jnp.where(first_window, 1, 0)])
        o2s[prev, :] = o
        l2s[prev, :] = l
        return carry

    lax.fori_loop(0, SLAB_GROUP, group2, 0, unroll=TILE_UNROLL)

    def merge(i, carry):
        r0 = pl.multiple_of(i * blk, blk)
        rows = pl.ds(r0, blk)
        pieces = [pl.ds(pl.multiple_of((i * (blk // SLAB_GROUP) + j) * SLAB_PITCH, 8), SLAB_GROUP)
                  for j in range(blk // SLAB_GROUP)]
        gather = lambda ref: jnp.concatenate([ref[p, :] for p in pieces], axis=0)
        l0, l1, l2 = l0s[rows, :], gather(l1s), gather(l2s)
        m = jnp.maximum(jnp.maximum(l0, l1), l2)
        w0, w1, w2 = jnp.exp2(l0 - m), jnp.exp2(l1 - m), jnp.exp2(l2 - m)
        o = (w0 * o0s[rows, :] + w1 * gather(o1s) + w2 * gather(o2s)) * (1.0 / (w0 + w1 + w2))
        out_ref[rows, :] = (o * _silu(g_ref[rows, :].astype(F32))).astype(out_ref.dtype)
        return carry

    lax.fori_loop(0, w // blk, merge, 0)


def _dilated_attention(h_main, batch, seq):
    n = batch * seq
    w = DIL_WINDOW
    nt = seq // w
    slab_rows = (w // SLAB_GROUP) * SLAB_PITCH

    def cur(col):
        return pl.BlockSpec((w, HEAD_DIM), lambda b, t, h: (b * nt + t, col // HEAD_DIM + h))

    def prev(col):
        return pl.BlockSpec((w, HEAD_DIM), lambda b, t, h: (b * nt + jnp.maximum(t - 1, 0), col // HEAD_DIM + h))

    width = DIL_HEADS * HEAD_DIM
    slab = lambda rows: pltpu.VMEM((rows, LANES), F32)
    return pl.pallas_call(
        _dil_kernel,
        grid=(batch, nt, DIL_HEADS),
        in_specs=[cur(L0_DQ), cur(L0_DQ + width), cur(L0_DQ + 2 * width),
                  cur(L0_DK), prev(L0_DK), cur(L0_DV), prev(L0_DV), cur(L0_DG)],
        out_specs=pl.BlockSpec((w, HEAD_DIM), lambda b, t, h: (b * nt + t, h)),
        out_shape=jax.ShapeDtypeStruct((n, width), BF16),
        scratch_shapes=[
            pltpu.VMEM((w + DIL_BLOCK, HEAD_DIM), BF16), pltpu.VMEM((w + DIL_BLOCK, HEAD_DIM), BF16),
            slab(2 * slab_rows), slab(2 * slab_rows), slab(slab_rows), slab(slab_rows),
            slab(w), slab(w), slab(slab_rows), slab(slab_rows), slab(slab_rows), slab(slab_rows),
            pltpu.VMEM((4, DIL_BLOCK, 2 * DIL_BLOCK), F32),
        ],
        compiler_params=_cparams(("arbitrary", "arbitrary", "arbitrary")),
        name="dilated_attn",
    )(*([h_main] * 8))


def _mem_kv_kernel(m_ref, w_ref, o_ref):
    o_ref[...] = _nn(m_ref[...].astype(BF16), w_ref[...]).astype(o_ref.dtype)


def _mem_kv(mem2d, wk, wv):
    rows, kdim = mem2d.shape
    w = jnp.concatenate([wk, wv], axis=1).astype(BF16)
    tm = MEM_LEN
    return pl.pallas_call(
        _mem_kv_kernel,
        grid=(rows // tm,),
        in_specs=[pl.BlockSpec((tm, kdim), lambda i: (i, 0)), pl.BlockSpec(w.shape, lambda i: (0, 0))],
        out_specs=pl.BlockSpec((tm, w.shape[1]), lambda i: (i, 0)),
        out_shape=jax.ShapeDtypeStruct((rows, w.shape[1]), BF16),
        compiler_params=_cparams(("arbitrary",)),
        name="mem_kv",
    )(mem2d, w)


def _mem_attn_kernel(q_ref, g_ref, k_ref, v_ref, o_ref, *, tq):
    for h in range(MEM_HEADS):
        hs = slice(h * HEAD_DIM, (h + 1) * HEAD_DIM)
        s = _nt(q_ref[:, hs], k_ref[:, hs]) * (HEAD_DIM ** -0.5 * LOG2_E)
        m = jnp.max(s, -1, keepdims=True)
        p = jnp.exp2(s - m)
        den = jnp.sum(p, -1, keepdims=True)
        o = _nn(p.astype(BF16), v_ref[:, hs]) * (1.0 / den)
        o_ref[:, hs] = (o * _silu(g_ref[:, hs].astype(F32))).astype(o_ref.dtype)


def _mem_attn(h_main, kv, batch, seq, q_col, g_col, tq=1024):
    n = batch * seq
    width = MEM_HEADS * HEAD_DIM
    nt = seq // tq
    return pl.pallas_call(
        functools.partial(_mem_attn_kernel, tq=tq),
        grid=(batch, nt),
        in_specs=[
            pl.BlockSpec((tq, width), lambda b, t: (b * nt + t, q_col // width)),
            pl.BlockSpec((tq, width), lambda b, t: (b * nt + t, g_col // width)),
            pl.BlockSpec((MEM_LEN, width), lambda b, t: (b, 0)),
            pl.BlockSpec((MEM_LEN, width), lambda b, t: (b, 1)),
        ],
        out_specs=pl.BlockSpec((tq, width), lambda b, t: (b * nt + t, 0)),
        out_shape=jax.ShapeDtypeStruct((n, width), BF16),
        compiler_params=_cparams(("arbitrary", "arbitrary")),
        name="mem_attn",
    )(h_main, h_main, kv, kv)


OUT_SUB_ROWS = 256


def _out_kernel(*refs, n_parts, emit_bf16):
    a_refs = refs[:n_parts]
    w_refs = refs[n_parts:2 * n_parts]
    x_ref, lw_ref, lb_ref, o_ref = refs[2 * n_parts:2 * n_parts + 4]
    for r0 in range(0, x_ref.shape[0], OUT_SUB_ROWS):
        rows = slice(r0, r0 + OUT_SUB_ROWS)
        acc = DEEPNORM_ALPHA * x_ref[rows, :]
        for a_ref, w_ref in zip(a_refs, w_refs):
            acc = acc + _nn(a_ref[rows, :], w_ref[...])
        mu = jnp.mean(acc, -1, keepdims=True)
        d = acc - mu
        var = jnp.mean(d * d, -1, keepdims=True)
        y = d * lax.rsqrt(var + LN_EPS) * lw_ref[...] + lb_ref[...]
        o_ref[rows, :] = y
        if emit_bf16:
            refs[2 * n_parts + 4][rows, :] = y.astype(BF16)


def _out_proj_ln(parts, w_out, x, ln_w, ln_b, emit_bf16=False, tm=1024):
    n, dm = x.shape
    w_bf = w_out.astype(BF16)
    w_specs, off = [], 0
    for p in parts:
        width = p.shape[1]
        w_specs.append(pl.BlockSpec((width, dm), functools.partial(lambda i, blk: (blk, 0), blk=off // width)))
        assert off % width == 0
        off += width
    row = lambda width: pl.BlockSpec((tm, width), lambda i: (i, 0))
    const = lambda shape: pl.BlockSpec(shape, lambda i: (0, 0))
    out_specs = [row(dm)] + ([row(dm)] if emit_bf16 else [])
    out_shape = [jax.ShapeDtypeStruct((n, dm), F32)] + ([jax.ShapeDtypeStruct((n, dm), BF16)] if emit_bf16 else [])
    return pl.pallas_call(
        functools.partial(_out_kernel, n_parts=len(parts), emit_bf16=emit_bf16),
        grid=(n // tm,),
        in_specs=[row(p.shape[1]) for p in parts] + w_specs + [row(dm), const((1, dm)), const((1, dm))],
        out_specs=out_specs,
        out_shape=out_shape,
        compiler_params=_cparams(("arbitrary",)),
        name="out_proj_ln",
    )(*parts, *([w_bf] * len(parts)), x, ln_w.reshape(1, dm), ln_b.reshape(1, dm))


def _ssd_kernel(z_ref, xs_ref, bm_ref, cm_ref, dt_ref, cw_ref, cb_ref, dtb_ref, a_ref, dsk_ref, nw_ref,
                o_ref, u_ref, pair_ref, lag_ref, act_ref, st_ref, *, n_chunks):
    q_ = SSD_CHUNK
    halo = 8

    @pl.when(pl.program_id(1) == 0)
    def _():
        st_ref[...] = jnp.zeros_like(st_ref)
        u_ref[0:halo, :] = jnp.zeros((halo, SSD_CONV_DIM), F32)
        pair_ref[0:halo, :] = jnp.zeros((halo, SSD_CONV_DIM), F32)

    row = lax.broadcasted_iota(jnp.int32, (q_, q_), 0)
    col = lax.broadcasted_iota(jnp.int32, (q_, q_), 1)
    causal = col <= row
    tril = jnp.where(causal, 1.0, 0.0).astype(BF16)
    lane = lax.broadcasted_iota(jnp.int32, (q_, LANES), 1)
    lo_half = lane < SSD_HEAD_DIM
    lane_row = lax.broadcasted_iota(jnp.int32, (1, LANES), 1)
    lo_half_row = lane_row < SSD_HEAD_DIM
    lo_half_f = lo_half.astype(F32)
    a_log2 = -jnp.exp(a_ref[...]) * LOG2_E

    def row_bcast(t, r):
        return jnp.tile(jnp.broadcast_to(t[r:r + 1, :], (8, q_)), (q_ // 8, 1))

    def chunk(ci, carry):
        r0 = pl.multiple_of(ci * q_, q_)
        rows = pl.ds(r0, q_)
        bc_w = SSD_GROUPS * SSD_STATE
        u_ref[halo:halo + q_, 0:SSD_INNER] = xs_ref[rows, :].astype(F32)
        u_ref[halo:halo + q_, SSD_INNER:SSD_INNER + bc_w] = bm_ref[rows, :].astype(F32)
        u_ref[halo:halo + q_, SSD_INNER + bc_w:SSD_CONV_DIM] = cm_ref[rows, :].astype(F32)
        for c0 in range(0, SSD_CONV_DIM, CONV_COLS):
            cs = slice(c0, c0 + CONV_COLS)
            u_cur = u_ref[halo:halo + q_, cs]
            lag_ref[...] = u_ref[halo - 1:halo - 1 + q_, cs]
            u_lag = lag_ref[...]
            pair_ref[halo:halo + q_, cs] = cw_ref[1:2, cs] * u_cur + cw_ref[0:1, cs] * u_lag
            conv = ((cb_ref[:, cs] + cw_ref[3:4, cs] * u_cur) + cw_ref[2:3, cs] * u_lag
                    + pair_ref[halo - 2:halo - 2 + q_, cs])
            act_ref[:, cs] = _silu(conv)
        u_ref[0:halo, :] = u_ref[q_:q_ + halo, :]
        pair_ref[0:halo, :] = pair_ref[q_:q_ + halo, :]
        dt_in = dt_ref[rows, :] + dtb_ref[...]
        dt = jnp.maximum(dt_in, 0.0) + jnp.log1p(jnp.exp(-jnp.abs(dt_in)))
        acum = _cumsum_rows(tril, dt * a_log2)
        acum_t = acum.T
        dt_t = dt.T
        ldt_t = jnp.log2(dt_t)
        rowp_t = acum_t - ldt_t
        w_t = dt_t * jnp.exp2(acum_t[:, q_ - 1:q_] - acum_t)
        c_decay = jnp.exp2(acum[q_ - 1:q_, :])
        for g in range(SSD_GROUPS):
            b_f = act_ref[:, SSD_INNER + g * SSD_STATE:SSD_INNER + (g + 1) * SSD_STATE]
            c_f = act_ref[:, SSD_INNER + (SSD_GROUPS + g) * SSD_STATE:SSD_INNER + (SSD_GROUPS + g + 1) * SSD_STATE]
            c_bf = c_f.astype(BF16)
            cbm = jnp.where(causal, _nt(c_bf, b_f.astype(BF16)), 0.0)
            b_t = b_f.T
            st = st_ref[g]
            y_off_all = _nn(c_bf, st.astype(BF16))
            ys, news, decs = [], [], []
            ss = jnp.zeros((q_, 1), F32)
            for j in range(SSD_GROUP_WIDTH // LANES):
                h1 = (g * SSD_GROUP_WIDTH + j * LANES) // SSD_HEAD_DIM
                ps = slice(g * SSD_GROUP_WIDTH + j * LANES, g * SSD_GROUP_WIDTH + (j + 1) * LANES)
                xs = act_ref[:, ps]
                intra, inter, e_l = [], [], []
                for hh in (h1, h1 + 1):
                    colb = jnp.broadcast_to(acum[:, hh:hh + 1], (q_, q_))
                    decay = jnp.exp2(jnp.minimum(colb - row_bcast(rowp_t, hh), row_bcast(ldt_t, hh)))
                    intra.append((cbm * decay).astype(BF16))
                    inter.append((b_t * row_bcast(w_t, hh)).astype(BF16))
                    e_l.append(jnp.exp2(colb))
                lhs = jnp.concatenate([jnp.concatenate(intra, axis=1), jnp.concatenate(inter, axis=1)], axis=0)
                xs_lo = xs * lo_half_f
                rhs = jnp.concatenate([xs_lo, xs - xs_lo], axis=0).astype(BF16)
                prod = _nn(lhs, rhs)
                y = prod[0:q_] + y_off_all[:, j * LANES:(j + 1) * LANES] * jnp.where(lo_half, e_l[0], e_l[1])
                y = y + dsk_ref[:, ps] * xs
                y = y * _silu(z_ref[rows, ps].astype(F32))
                ss = ss + jnp.sum(y * y, -1, keepdims=True)
                ys.append(y)
                news.append(prod[q_:2 * q_])
                decs.append(jnp.where(lo_half_row, c_decay[:, h1:h1 + 1], c_decay[:, h1 + 1:h1 + 2]))
            st_ref[g] = st * jnp.concatenate(decs, axis=1) + jnp.concatenate(news, axis=1)
            inv = lax.rsqrt(ss * (1.0 / SSD_GROUP_WIDTH) + RMS_EPS)
            for j, y in enumerate(ys):
                ps = slice(g * SSD_GROUP_WIDTH + j * LANES, g * SSD_GROUP_WIDTH + (j + 1) * LANES)
                o_ref[rows, ps] = (y * inv * nw_ref[:, ps]).astype(o_ref.dtype)
        return carry

    lax.fori_loop(0, n_chunks, chunk, 0)


def _ssd(h_main, h_small, conv_w, conv_b, dt_bias, a_log, d_skip, norm_w, batch, seq, tb=512):
    n = batch * seq
    nt = seq // tb
    bc_w = SSD_GROUPS * SSD_STATE
    pad = lambda v: jnp.zeros((1, LANES), F32).at[0, :SSD_HEADS].set(v.astype(F32))
    idx = lambda c: (lambda b, t: (b * nt + t, c))
    const = lambda b, t: (0, 0)
    return pl.pallas_call(
        functools.partial(_ssd_kernel, n_chunks=tb // SSD_CHUNK),
        grid=(batch, nt),
        in_specs=[
            pl.BlockSpec((tb, SSD_INNER), idx(L1_Z // SSD_INNER)),
            pl.BlockSpec((tb, SSD_INNER), idx(L1_XBC // SSD_INNER)),
            pl.BlockSpec((tb, bc_w), idx((L1_XBC + SSD_INNER) // bc_w)),
            pl.BlockSpec((tb, bc_w), idx((L1_XBC + SSD_INNER + bc_w) // bc_w)),
            pl.BlockSpec((tb, LANES), idx(0)),
            pl.BlockSpec((SSD_CONV, SSD_CONV_DIM), const),
            pl.BlockSpec((1, SSD_CONV_DIM), const),
            pl.BlockSpec((1, LANES), const),
            pl.BlockSpec((1, LANES), const),
            pl.BlockSpec((1, SSD_INNER), const),
            pl.BlockSpec((1, SSD_INNER), const),
        ],
        out_specs=pl.BlockSpec((tb, SSD_INNER), idx(0)),
        out_shape=jax.ShapeDtypeStruct((n, SSD_INNER), BF16),
        scratch_shapes=[
            pltpu.VMEM((SSD_CHUNK + 8, SSD_CONV_DIM), F32),
            pltpu.VMEM((SSD_CHUNK + 8, SSD_CONV_DIM), F32),
            pltpu.VMEM((SSD_CHUNK, CONV_COLS), F32),
            pltpu.VMEM((SSD_CHUNK, SSD_CONV_DIM), F32),
            pltpu.VMEM((SSD_GROUPS, SSD_STATE, SSD_GROUP_WIDTH), F32),
        ],
        compiler_params=_cparams(("arbitrary", "arbitrary")),
        name="ssd",
    )(h_main, h_main, h_main, h_main, h_small, conv_w, conv_b.reshape(1, -1),
      pad(dt_bias), pad(a_log), jnp.repeat(d_skip.astype(F32), SSD_HEAD_DIM).reshape(1, -1),
      norm_w.reshape(1, -1))


def _pad_cols(w, width=LANES):
    return jnp.zeros((w.shape[0], width), BF16).at[:, :w.shape[1]].set(w.astype(BF16))


def kernel(x, mem, positions, l0_w_in, l0_gla_w_gate, l0_gla_b_gate, l0_gla_norm_w, l0_mem_wk, l0_mem_wv, l0_w_out, l0_ln_w, l0_ln_b, l1_w_in, l1_conv_w, l1_conv_b, l1_dt_bias, l1_a_log, l1_d_skip, l1_ssd_norm_w, l1_mem_wk, l1_mem_wv, l1_w_out, l1_ln_w, l1_ln_b):
    batch, seq, dm = x.shape
    n = batch * seq
    x2 = x.reshape(n, dm)
    mem2 = mem.reshape(batch * MEM_LEN, dm)

    ga0 = L0_DQ + GLA_GATE_RANK
    w0_a = l0_w_in[:, :L0_DQ].astype(BF16)
    w0_b = l0_w_in[:, ga0:].astype(BF16)
    w0_small = _pad_cols(l0_w_in[:, L0_DQ:ga0])
    cos, sin = _rope_tables(positions)
    col_scale = jnp.ones((1, L0_MAIN), F32).at[0, L0_DQ:L0_DK].set(HEAD_DIM ** -0.5 * LOG2_E)
    h0, h0s = _project(x2, w0_a, w0_b, w0_small, rope=(cos, sin, col_scale, L0_DQ, L0_DV), tm=1024)
    o_a = _gla(h0, h0s, l0_gla_w_gate, l0_gla_b_gate, l0_gla_norm_w, batch, seq)
    o_b = _dilated_attention(h0, batch, seq)
    kv0 = _mem_kv(mem2, l0_mem_wk, l0_mem_wv)
    o_m = _mem_attn(h0, kv0, batch, seq, L0_MQ, L0_MG)
    x1, x1_bf = _out_proj_ln([o_a, o_b, o_m], l0_w_out, x2, l0_ln_w, l0_ln_b, emit_bf16=True)

    dt0 = SSD_INNER + SSD_CONV_DIM
    w1_a = l1_w_in[:, :dt0].astype(BF16)
    w1_b = l1_w_in[:, dt0 + SSD_HEADS:].astype(BF16)
    w1_small = _pad_cols(l1_w_in[:, dt0:dt0 + SSD_HEADS])
    h1, h1s = _project(x1_bf, w1_a, w1_b, w1_small)
    y = _ssd(h1, h1s, l1_conv_w, l1_conv_b, l1_dt_bias, l1_a_log, l1_d_skip, l1_ssd_norm_w, batch, seq)
    kv1 = _mem_kv(mem2, l1_mem_wk, l1_mem_wv)
    o_m1 = _mem_attn(h1, kv1, batch, seq, L1_MQ, L1_MG)
    (x2_out,) = _out_proj_ln([y, o_m1], l1_w_out, x1, l1_ln_w, l1_ln_b)
    return x2_out.reshape(batch, seq, dm)
```

```python
import functools

import jax
import jax.numpy as jnp
from jax import lax
from jax.experimental import pallas as pl
from jax.experimental.pallas import tpu as pltpu

F32 = jnp.float32
BF16 = jnp.bfloat16

DEPTH = 2
DEEPNORM_ALPHA = (2 * DEPTH) ** 0.25
LN_EPS = 1e-5
RMS_EPS = 1e-6
ROPE_THETA = 10000.0
MEM_LEN = 256

GLA_HEADS = 4
GLA_DK = 128
GLA_DV = 256
GLA_GATE_RANK = 16
GLA_GATE_NORMALIZER = 16.0
GLA_CHUNK = 64

DIL_PATTERNS = ((128, 1), (512, 4), (2048, 16))
DIL_HEADS = 4
HEAD_DIM = 128
DIL_BLOCK = 128

MEM_HEADS = 4

SSD_HEADS = 24
SSD_HEAD_DIM = 64
SSD_GROUPS = 4
SSD_STATE = 128
SSD_CONV = 4
SSD_CHUNK = 128
SSD_INNER = SSD_HEADS * SSD_HEAD_DIM
SSD_GROUP_WIDTH = SSD_INNER // SSD_GROUPS
SSD_CONV_DIM = SSD_INNER + 2 * SSD_GROUPS * SSD_STATE
CONV_COLS = 256

LANES = 128
LOG2_E = 1.4426950408889634
VMEM_LIMIT = 48 * 1024 * 1024

L0_GQ, L0_GK, L0_GV, L0_GG = 0, 512, 1024, 2048
L0_DQ, L0_DK, L0_DV, L0_DG = 3072, 4608, 5120, 5632
L0_MQ, L0_MG = 6144, 6656
L0_MAIN = 7168
L1_Z, L1_XBC, L1_MQ, L1_MG = 0, 1536, 4096, 4608
L1_MAIN = 5120


def _nt(a, b):
    return lax.dot_general(a, b, (((1,), (1,)), ((), ())), preferred_element_type=F32)


def _tn(a, b):
    return lax.dot_general(a, b, (((0,), (0,)), ((), ())), preferred_element_type=F32)


def _nn(a, b):
    return jnp.dot(a, b, preferred_element_type=F32)


def _cumsum_rows(tril, x):
    hi = x.astype(BF16)
    rest = x - hi.astype(F32)
    mid = rest.astype(BF16)
    lo = (rest - mid.astype(F32)).astype(BF16)
    return _nn(tril, hi) + _nn(tril, mid) + _nn(tril, lo)


def _silu(x):
    return 0.5 * x * (1.0 + jnp.tanh(0.5 * x))


def _cparams(sem):
    return pltpu.CompilerParams(dimension_semantics=sem, vmem_limit_bytes=VMEM_LIMIT)


def _rope_table_kernel(pos_ref, inv_ref, sign_ref, cos_ref, sin_ref):
    rows = pos_ref.shape[0]
    half = HEAD_DIM // 2
    ang = pos_ref[...].astype(F32) * inv_ref[...]
    cos = jnp.cos(ang)
    sin = jnp.sin(ang)
    lo = lax.broadcasted_iota(jnp.int32, (rows, HEAD_DIM), 1) < half
    sign = sign_ref[...]
    for table, ref, scale in ((cos, cos_ref, None), (sin, sin_ref, sign)):
        swapped = pltpu.roll(table, half, 1)
        first = jnp.where(lo, table, swapped)
        second = jnp.where(lo, swapped, table)
        if scale is not None:
            first, second = first * scale, second * scale
        ref[0:rows, :] = first
        ref[rows:2 * rows, :] = second


def _rope_tables(positions):
    n = positions.size
    half = HEAD_DIM // 2
    inv = 1.0 / (ROPE_THETA ** (jnp.arange(0, HEAD_DIM, 2, dtype=F32) / HEAD_DIM))
    inv2 = jnp.concatenate([inv, inv]).reshape(1, HEAD_DIM)
    sign = jnp.concatenate([-jnp.ones((half,), F32), jnp.ones((half,), F32)]).reshape(1, HEAD_DIM)
    tb = 2048
    pos_pair = positions.reshape(n // tb, 2, tb // 2).transpose(0, 2, 1)
    pos_pair = jnp.repeat(pos_pair, half, axis=2).reshape(n // 2, HEAD_DIM)
    row = pl.BlockSpec((tb, HEAD_DIM), lambda i: (i, 0))
    vec = pl.BlockSpec((1, HEAD_DIM), lambda i: (0, 0))
    return pl.pallas_call(
        _rope_table_kernel,
        grid=(n // tb,),
        in_specs=[pl.BlockSpec((tb // 2, HEAD_DIM), lambda i: (i, 0)), vec, vec],
        out_specs=[row, row],
        out_shape=[jax.ShapeDtypeStruct((n, HEAD_DIM), F32)] * 2,
        compiler_params=_cparams(("arbitrary",)),
        name="rope_tables",
    )(pos_pair, inv2, sign)


SUB_COLS = 256


def _proj_kernel(*refs, groups, cast_x, has_rope):
    it = iter(refs)
    x_ref, wa_ref, wb_ref, ws_ref = next(it), next(it), next(it), next(it)
    if has_rope:
        c_ref, s_ref, sc_ref = next(it), next(it), next(it)
    o_ref, os_ref = next(it), next(it)
    xb_ref = next(it) if cast_x else x_ref
    j = pl.program_id(1)

    @pl.when(j == 0)
    def _():
        if cast_x:
            xb_ref[...] = x_ref[...].astype(BF16)
        os_ref[...] = _nn(xb_ref[...], ws_ref[...])

    def rope_block(acc, cols):
        cos, sin = c_ref[...], s_ref[...]
        for k in range(SUB_COLS // HEAD_DIM):
            sl = slice(cols.start + k * HEAD_DIM, cols.start + (k + 1) * HEAD_DIM)
            a = acc[:, k * HEAD_DIM:(k + 1) * HEAD_DIM] * sc_ref[:, sl]
            o_ref[:, sl] = (a * cos + pltpu.roll(a, HEAD_DIM // 2, 1) * sin).astype(o_ref.dtype)

    def emit(w_ref, kinds):
        if kinds is None:
            o_ref[...] = _nn(xb_ref[...], w_ref[...]).astype(o_ref.dtype)
            return
        for sb, kind in enumerate(kinds):
            cols = slice(sb * SUB_COLS, (sb + 1) * SUB_COLS)
            acc = _nn(xb_ref[...], w_ref[:, cols])
            if kind == "rope":
                rope_block(acc, cols)
            else:
                o_ref[:, cols] = acc.astype(o_ref.dtype)

    for lo, hi, src, kinds in groups:
        pl.when(jnp.logical_and(j >= lo, j < hi))(functools.partial(emit, wb_ref if src else wa_ref, kinds))


def _project(x, w_a, w_b, w_small, rope=None, tm=2048, tn=1024):
    n, kdim = x.shape
    m = w_a.shape[1] + w_b.shape[1]
    split = w_a.shape[1] // tn
    n_tiles = m // tn
    cast_x = x.dtype != BF16
    per_tile = tn // SUB_COLS
    kinds = [["plain"] * per_tile for _ in range(n_tiles)]
    in_specs = [
        pl.BlockSpec((tm, kdim), lambda i, j: (i, 0)),
        pl.BlockSpec((kdim, tn), lambda i, j: (0, jnp.minimum(j, split - 1))),
        pl.BlockSpec((kdim, tn), lambda i, j: (0, jnp.maximum(j - split, 0))),
        pl.BlockSpec((kdim, LANES), lambda i, j: (0, 0)),
    ]
    args = [x, w_a, w_b, w_small]
    scratch = [pltpu.VMEM((tm, kdim), BF16)] if cast_x else []
    if rope is not None:
        cos, sin, col_scale, first_col, last_col = rope
        for c in range(first_col // SUB_COLS, last_col // SUB_COLS):
            kinds[c // per_tile][c % per_tile] = "rope"
        in_specs += [
            pl.BlockSpec((tm, HEAD_DIM), lambda i, j: (i, 0)),
            pl.BlockSpec((tm, HEAD_DIM), lambda i, j: (i, 0)),
            pl.BlockSpec((1, tn), lambda i, j: (0, j)),
        ]
        args += [cos, sin, col_scale]
    groups = []
    for t in range(n_tiles):
        plan = None if all(k == "plain" for k in kinds[t]) else tuple(kinds[t])
        src = int(t >= split)
        if groups and groups[-1][2] == src and groups[-1][3] == plan:
            groups[-1] = (groups[-1][0], t + 1, src, plan)
        else:
            groups.append((t, t + 1, src, plan))
    return pl.pallas_call(
        functools.partial(_proj_kernel, groups=tuple(groups), cast_x=cast_x, has_rope=rope is not None),
        grid=(n // tm, n_tiles),
        in_specs=in_specs,
        out_specs=[
            pl.BlockSpec((tm, tn), lambda i, j: (i, j)),
            pl.BlockSpec((tm, LANES), lambda i, j: (i, 0)),
        ],
        out_shape=[jax.ShapeDtypeStruct((n, m), BF16), jax.ShapeDtypeStruct((n, LANES), F32)],
        scratch_shapes=scratch,
        compiler_params=_cparams(("arbitrary", "arbitrary")),
        name="in_proj",
    )(*args)


GLA_BATCH_GROUP = 2


def _gla_kernel(q_ref, k_ref, v_ref, g_ref, ga_ref, wg_ref, bg_ref, nw_ref, o_ref, st_ref, la_ref, *, n_chunks):
    c_ = GLA_CHUNK
    group = q_ref.shape[0]

    @pl.when(pl.program_id(1) == 0)
    def _():
        st_ref[...] = jnp.zeros_like(st_ref)

    p_ = 2 * c_
    row = lax.broadcasted_iota(jnp.int32, (p_, p_), 0)
    col = lax.broadcasted_iota(jnp.int32, (p_, p_), 1)
    same_chunk = (row < c_) == (col < c_)
    tril = jnp.where(jnp.logical_and(same_chunk, col <= row), 1.0, 0.0).astype(BF16)
    row_a = lax.broadcasted_iota(jnp.int32, (c_, c_), 0)
    col_a = lax.broadcasted_iota(jnp.int32, (c_, c_), 1)
    causal_a = col_a <= row_a
    row_b = lax.broadcasted_iota(jnp.int32, (c_, p_), 0)
    col_b = lax.broadcasted_iota(jnp.int32, (c_, p_), 1)
    causal_b = col_b - c_ <= row_b
    in_a = lax.broadcasted_iota(jnp.int32, (p_, 1), 0) < c_
    nw = nw_ref[...]
    for b in range(group):
        z = _nn(ga_ref[b].astype(BF16), wg_ref[...]) + bg_ref[...]
        la_ref[b] = (jnp.minimum(z, 0.0) - jnp.log1p(jnp.exp(-jnp.abs(z)))) * (1.0 / GLA_GATE_NORMALIZER)

    def pair(pi, carry):
        r0 = pl.multiple_of(pi * p_, p_)
        rows = pl.ds(r0, p_)
        gates = []
        for b in range(group):
            bcum = _cumsum_rows(tril, la_ref[b, rows, :])
            blast_a = bcum[c_ - 1:c_, :]
            blast_b = bcum[p_ - 1:p_, :]
            blast = jnp.where(in_a, blast_a, blast_b)
            gates.append((jnp.exp(bcum), jnp.exp(-bcum), jnp.exp(blast - bcum), jnp.exp(blast_a), jnp.exp(blast_b)))
        for h in range(GLA_HEADS):
            for b in range(group):
                e_pos, e_neg, e_tail, d_a, d_b = gates[b]
                ks = slice(h * GLA_DK, (h + 1) * GLA_DK)
                vs = slice(h * GLA_DV, (h + 1) * GLA_DV)
                q = q_ref[b, rows, ks].astype(F32) * (GLA_DK ** -0.5)
                k = k_ref[b, rows, ks].astype(F32)
                q_dec = q * e_pos[:, ks]
                k_tail = k * e_tail[:, ks]
                q_dec_bf = q_dec.astype(BF16)
                k_inv_bf = (k * e_neg[:, ks]).astype(BF16)
                k_tail_bf = k_tail.astype(BF16)
                v = v_ref[b, rows, vs]
                s_a = jnp.where(causal_a, _nt(q_dec_bf[0:c_], k_inv_bf[0:c_]), 0.0)
                keys_b = jnp.concatenate([k_tail_bf[0:c_], k_inv_bf[c_:p_]], axis=0)
                s_b = jnp.where(causal_b, _nt(q_dec_bf[c_:p_], keys_b), 0.0)
                o_intra = jnp.concatenate([_nn(s_a.astype(BF16), v[0:c_]), _nn(s_b.astype(BF16), v)], axis=0)
                q_state = jnp.concatenate([q_dec_bf[0:c_], (q_dec[c_:p_] * d_a[:, ks]).astype(BF16)], axis=0)
                st = st_ref[b, h]
                o = o_intra + _nt(q_state, st.astype(BF16))
                k_end = jnp.concatenate([(k_tail[0:c_] * d_b[:, ks]).astype(BF16), k_tail_bf[c_:p_]], axis=0)
                st_ref[b, h] = st * (d_a[:, ks] * d_b[:, ks]) + _tn(v, k_end)
                ms = jnp.mean(o * o, axis=-1, keepdims=True)
                o = o * lax.rsqrt(ms + RMS_EPS) * nw
                o_ref[b, rows, vs] = (o * _silu(g_ref[b, rows, vs].astype(F32))).astype(o_ref.dtype)
        return carry

    lax.fori_loop(0, n_chunks // 2, pair, 0)


def _gla(h_main, h_small, w_gate, b_gate, norm_w, batch, seq, tb=512):
    n = batch * seq
    nt = seq // tb
    group = GLA_BATCH_GROUP if batch % GLA_BATCH_GROUP == 0 else 1
    wg = jnp.zeros((LANES, GLA_HEADS * GLA_DK), BF16).at[:GLA_GATE_RANK].set(w_gate.astype(BF16))
    h4 = h_main.reshape(batch // group, group, seq, h_main.shape[1])
    s4 = h_small.reshape(batch // group, group, seq, LANES)
    idx = lambda c: (lambda b, t: (b, 0, t, c))
    const = lambda b, t: (0, 0)
    width = GLA_HEADS * GLA_DV
    out = pl.pallas_call(
        functools.partial(_gla_kernel, n_chunks=tb // GLA_CHUNK),
        grid=(batch // group, nt),
        in_specs=[
            pl.BlockSpec((None, group, tb, 512), idx(L0_GQ // 512)),
            pl.BlockSpec((None, group, tb, 512), idx(L0_GK // 512)),
            pl.BlockSpec((None, group, tb, 1024), idx(L0_GV // 1024)),
            pl.BlockSpec((None, group, tb, 1024), idx(L0_GG // 1024)),
            pl.BlockSpec((None, group, tb, LANES), idx(0)),
            pl.BlockSpec((LANES, 512), const),
            pl.BlockSpec((1, 512), const),
            pl.BlockSpec((1, GLA_DV), const),
        ],
        out_specs=pl.BlockSpec((None, group, tb, width), idx(0)),
        out_shape=jax.ShapeDtypeStruct((batch // group, group, seq, width), BF16),
        scratch_shapes=[pltpu.VMEM((group, GLA_HEADS, GLA_DV, GLA_DK), F32),
                        pltpu.VMEM((group, tb, GLA_HEADS * GLA_DK), F32)],
        compiler_params=_cparams(("arbitrary", "arbitrary")),
        name="gla",
    )(h4, h4, h4, h4, s4, wg, b_gate.reshape(1, -1), norm_w.reshape(1, -1))
    return out.reshape(n, width)


DIL_WINDOW = 2048
SLAB_GROUP = 16
SLAB_PITCH = 24
SEG = 32
TILE_UNROLL = 8


def _dil_kernel(q0_ref, q1_ref, q2_ref, kc_ref, kp_ref, vc_ref, vp_ref, g_ref, out_ref,
                kcat, vcat, ks, vs, q1s, q2s, o0s, l0s, o1s, l1s, o2s, l2s, bias):
    blk = DIL_BLOCK
    w = DIL_WINDOW
    groups = w // SLAB_GROUP
    t = pl.program_id(1)
    first_window = t == 0

    @pl.when(jnp.logical_and(jnp.logical_and(pl.program_id(0) == 0, t == 0), pl.program_id(2) == 0))
    def _():
        rho = lax.broadcasted_iota(jnp.int32, (blk, 2 * blk), 0)
        kap = lax.broadcasted_iota(jnp.int32, (blk, 2 * blk), 1)
        in_prev = kap < blk
        kap_l = jnp.where(in_prev, kap, kap - blk)
        perm = lambda x: 4 * (x % SEG) + x // SEG
        neg = jnp.float32(-jnp.inf)
        for slot, (jq, jk) in enumerate(((rho, kap_l), (perm(rho), perm(kap_l)))):
            ok_prev = jnp.where(jk >= jq, 0.0, neg)
            ok_cur = jnp.where(jk <= jq, 0.0, neg)
            bias[2 * slot] = jnp.where(in_prev, ok_prev, ok_cur)
            bias[2 * slot + 1] = jnp.where(in_prev, neg, ok_cur)

    kcat[0:blk, :] = kp_ref[w - blk:w, :]
    kcat[blk:blk + w, :] = kc_ref[...]
    vcat[0:blk, :] = vp_ref[w - blk:w, :]
    vcat[blk:blk + w, :] = vc_ref[...]

    def to_slabs(i, carry):
        src = pl.ds(pl.multiple_of(i * SLAB_GROUP, SLAB_GROUP), SLAB_GROUP)
        prev = pl.ds(pl.multiple_of(i * SLAB_PITCH, 8), SLAB_GROUP)
        cur = pl.ds(pl.multiple_of((groups + i) * SLAB_PITCH, 8), SLAB_GROUP)
        ks[prev, :] = kp_ref[src, :].astype(F32)
        ks[cur, :] = kc_ref[src, :].astype(F32)
        vs[prev, :] = vp_ref[src, :].astype(F32)
        vs[cur, :] = vc_ref[src, :].astype(F32)
        q1s[prev, :] = q1_ref[src, :].astype(F32)
        q2s[prev, :] = q2_ref[src, :].astype(F32)
        return carry

    lax.fori_loop(0, groups, to_slabs, 0, unroll=4)

    def attend(q, k, v, b):
        s = _nt(q, k) + b
        m = jnp.max(s, -1, keepdims=True)
        p = jnp.exp2(s - m)
        den = jnp.sum(p, -1, keepdims=True)
        o = _nn(p.astype(BF16), v) * (1.0 / den)
        return o, jnp.broadcast_to(m + jnp.log2(den), (blk, LANES))

    def group0(i, carry):
        r0 = pl.multiple_of(i * blk, blk)
        sel = jnp.where(jnp.logical_and(first_window, i == 0), 1, 0)
        o, l = attend(q0_ref[pl.ds(r0, blk), :], kcat[pl.ds(r0, 2 * blk), :], vcat[pl.ds(r0, 2 * blk), :],
                      bias[sel])
        o0s[pl.ds(r0, blk), :] = o
        l0s[pl.ds(r0, blk), :] = l
        return carry

    lax.fori_loop(0, w // blk, group0, 0, unroll=TILE_UNROLL)

    def seg(ref, group, b):
        return ref[pl.ds(group * SLAB_PITCH + b, SEG, stride=SLAB_PITCH), :]

    def group1(idx, carry):
        r = idx // 4
        n = idx % 4
        g_cur = groups + SEG * n
        q = jnp.concatenate([seg(q1s, SEG * n, r + 4 * j) for j in range(4)], axis=0).astype(BF16)
        k = jnp.concatenate([seg(ks, g_cur - SEG, r + 4 * j) for j in range(4)]
                            + [seg(ks, g_cur, r + 4 * j) for j in range(4)], axis=0).astype(BF16)
        v = jnp.concatenate([seg(vs, g_cur - SEG, r + 4 * j) for j in range(4)]
                            + [seg(vs, g_cur, r + 4 * j) for j in range(4)], axis=0).astype(BF16)
        sel = jnp.where(jnp.logical_and(first_window, n == 0), 3, 2)
        o, l = attend(q, k, v, bias[sel])
        for j in range(4):
            dst = pl.ds(SEG * n * SLAB_PITCH + r + 4 * j, SEG, stride=SLAB_PITCH)
            o1s[dst, :] = o[j * SEG:(j + 1) * SEG]
            l1s[dst, :] = l[j * SEG:(j + 1) * SEG]
        return carry

    lax.fori_loop(0, 16, group1, 0, unroll=TILE_UNROLL)

    def group2(r, carry):
        prev = pl.ds(r, blk, stride=SLAB_PITCH)
        cur = pl.ds(groups * SLAB_PITCH + r, blk, stride=SLAB_PITCH)
        q = q2s[prev, :].astype(BF16)
        k = jnp.concatenate([ks[prev, :], ks[cur, :]], axis=0).astype(BF16)
        v = jnp.concatenate([vs[prev, :], vs[cur, :]], axis=0).astype(BF16)
        o, l = attend(q, k, v, bias[jnp.where(first_window, 1, 0)])
        o2s[prev, :] = o
        l2s[prev, :] = l
        return carry

    lax.fori_loop(0, SLAB_GROUP, group2, 0, unroll=TILE_UNROLL)

    def merge(i, carry):
        r0 = pl.multiple_of(i * blk, blk)
        rows = pl.ds(r0, blk)
        pieces = [pl.ds(pl.multiple_of((i * (blk // SLAB_GROUP) + j) * SLAB_PITCH, 8), SLAB_GROUP)
                  for j in range(blk // SLAB_GROUP)]
        gather = lambda ref: jnp.concatenate([ref[p, :] for p in pieces], axis=0)
        l0, l1, l2 = l0s[rows, :], gather(l1s), gather(l2s)
        m = jnp.maximum(jnp.maximum(l0, l1), l2)
        w0, w1, w2 = jnp.exp2(l0 - m), jnp.exp2(l1 - m), jnp.exp2(l2 - m)
        o = (w0 * o0s[rows, :] + w1 * gather(o1s) + w2 * gather(o2s)) * (1.0 / (w0 + w1 + w2))
        out_ref[rows, :] = (o * _silu(g_ref[rows, :].astype(F32))).astype(out_ref.dtype)
        return carry

    lax.fori_loop(0, w // blk, merge, 0)


def _dilated_attention(h_main, batch, seq):
    n = batch * seq
    w = DIL_WINDOW
    nt = seq // w
    slab_rows = (w // SLAB_GROUP) * SLAB_PITCH

    def cur(col):
        return pl.BlockSpec((w, HEAD_DIM), lambda b, t, h: (b * nt + t, col // HEAD_DIM + h))

    def prev(col):
        return pl.BlockSpec((w, HEAD_DIM), lambda b, t, h: (b * nt + jnp.maximum(t - 1, 0), col // HEAD_DIM + h))

    width = DIL_HEADS * HEAD_DIM
    slab = lambda rows: pltpu.VMEM((rows, LANES), F32)
    return pl.pallas_call(
        _dil_kernel,
        grid=(batch, nt, DIL_HEADS),
        in_specs=[cur(L0_DQ), cur(L0_DQ + width), cur(L0_DQ + 2 * width),
                  cur(L0_DK), prev(L0_DK), cur(L0_DV), prev(L0_DV), cur(L0_DG)],
        out_specs=pl.BlockSpec((w, HEAD_DIM), lambda b, t, h: (b * nt + t, h)),
        out_shape=jax.ShapeDtypeStruct((n, width), BF16),
        scratch_shapes=[
            pltpu.VMEM((w + DIL_BLOCK, HEAD_DIM), BF16), pltpu.VMEM((w + DIL_BLOCK, HEAD_DIM), BF16),
            slab(2 * slab_rows), slab(2 * slab_rows), slab(slab_rows), slab(slab_rows),
            slab(w), slab(w), slab(slab_rows), slab(slab_rows), slab(slab_rows), slab(slab_rows),
            pltpu.VMEM((4, DIL_BLOCK, 2 * DIL_BLOCK), F32),
        ],
        compiler_params=_cparams(("arbitrary", "arbitrary", "arbitrary")),
        name="dilated_attn",
    )(*([h_main] * 8))


def _mem_kv_kernel(m_ref, w_ref, o_ref):
    o_ref[...] = _nn(m_ref[...].astype(BF16), w_ref[...]).astype(o_ref.dtype)


def _mem_kv(mem2d, wk, wv):
    rows, kdim = mem2d.shape
    w = jnp.concatenate([wk, wv], axis=1).astype(BF16)
    tm = MEM_LEN
    return pl.pallas_call(
        _mem_kv_kernel,
        grid=(rows // tm,),
        in_specs=[pl.BlockSpec((tm, kdim), lambda i: (i, 0)), pl.BlockSpec(w.shape, lambda i: (0, 0))],
        out_specs=pl.BlockSpec((tm, w.shape[1]), lambda i: (i, 0)),
        out_shape=jax.ShapeDtypeStruct((rows, w.shape[1]), BF16),
        compiler_params=_cparams(("arbitrary",)),
        name="mem_kv",
    )(mem2d, w)


def _mem_attn_kernel(q_ref, g_ref, k_ref, v_ref, o_ref, *, tq):
    for h in range(MEM_HEADS):
        hs = slice(h * HEAD_DIM, (h + 1) * HEAD_DIM)
        s = _nt(q_ref[:, hs], k_ref[:, hs]) * (HEAD_DIM ** -0.5 * LOG2_E)
        m = jnp.max(s, -1, keepdims=True)
        p = jnp.exp2(s - m)
        den = jnp.sum(p, -1, keepdims=True)
        o = _nn(p.astype(BF16), v_ref[:, hs]) * (1.0 / den)
        o_ref[:, hs] = (o * _silu(g_ref[:, hs].astype(F32))).astype(o_ref.dtype)


def _mem_attn(h_main, kv, batch, seq, q_col, g_col, tq=1024):
    n = batch * seq
    width = MEM_HEADS * HEAD_DIM
    nt = seq // tq
    return pl.pallas_call(
        functools.partial(_mem_attn_kernel, tq=tq),
        grid=(batch, nt),
        in_specs=[
            pl.BlockSpec((tq, width), lambda b, t: (b * nt + t, q_col // width)),
            pl.BlockSpec((tq, width), lambda b, t: (b * nt + t, g_col // width)),
            pl.BlockSpec((MEM_LEN, width), lambda b, t: (b, 0)),
            pl.BlockSpec((MEM_LEN, width), lambda b, t: (b, 1)),
        ],
        out_specs=pl.BlockSpec((tq, width), lambda b, t: (b * nt + t, 0)),
        out_shape=jax.ShapeDtypeStruct((n, width), BF16),
        compiler_params=_cparams(("arbitrary", "arbitrary")),
        name="mem_attn",
    )(h_main, h_main, kv, kv)


OUT_SUB_ROWS = 256


def _out_kernel(*refs, n_parts, emit_bf16):
    a_refs = refs[:n_parts]
    w_refs = refs[n_parts:2 * n_parts]
    x_ref, lw_ref, lb_ref, o_ref = refs[2 * n_parts:2 * n_parts + 4]
    for r0 in range(0, x_ref.shape[0], OUT_SUB_ROWS):
        rows = slice(r0, r0 + OUT_SUB_ROWS)
        acc = DEEPNORM_ALPHA * x_ref[rows, :]
        for a_ref, w_ref in zip(a_refs, w_refs):
            acc = acc + _nn(a_ref[rows, :], w_ref[...])
        mu = jnp.mean(acc, -1, keepdims=True)
        d = acc - mu
        var = jnp.mean(d * d, -1, keepdims=True)
        y = d * lax.rsqrt(var + LN_EPS) * lw_ref[...] + lb_ref[...]
        o_ref[rows, :] = y
        if emit_bf16:
            refs[2 * n_parts + 4][rows, :] = y.astype(BF16)


def _out_proj_ln(parts, w_out, x, ln_w, ln_b, emit_bf16=False, tm=1024):
    n, dm = x.shape
    w_bf = w_out.astype(BF16)
    w_specs, off = [], 0
    for p in parts:
        width = p.shape[1]
        w_specs.append(pl.BlockSpec((width, dm), functools.partial(lambda i, blk: (blk, 0), blk=off // width)))
        assert off % width == 0
        off += width
    row = lambda width: pl.BlockSpec((tm, width), lambda i: (i, 0))
    const = lambda shape: pl.BlockSpec(shape, lambda i: (0, 0))
    out_specs = [row(dm)] + ([row(dm)] if emit_bf16 else [])
    out_shape = [jax.ShapeDtypeStruct((n, dm), F32)] + ([jax.ShapeDtypeStruct((n, dm), BF16)] if emit_bf16 else [])
    return pl.pallas_call(
        functools.partial(_out_kernel, n_parts=len(parts), emit_bf16=emit_bf16),
        grid=(n // tm,),
        in_specs=[row(p.shape[1]) for p in parts] + w_specs + [row(dm), const((1, dm)), const((1, dm))],
        out_specs=out_specs,
        out_shape=out_shape,
        compiler_params=_cparams(("arbitrary",)),
        name="out_proj_ln",
    )(*parts, *([w_bf] * len(parts)), x, ln_w.reshape(1, dm), ln_b.reshape(1, dm))


def _ssd_kernel(z_ref, xs_ref, bm_ref, cm_ref, dt_ref, cw_ref, cb_ref, dtb_ref, a_ref, dsk_ref, nw_ref,
                o_ref, u_ref, pair_ref, lag_ref, act_ref, st_ref, *, n_chunks):
    q_ = SSD_CHUNK
    halo = 8

    @pl.when(pl.program_id(1) == 0)
    def _():
        st_ref[...] = jnp.zeros_like(st_ref)
        u_ref[0:halo, :] = jnp.zeros((halo, SSD_CONV_DIM), F32)
        pair_ref[0:halo, :] = jnp.zeros((halo, SSD_CONV_DIM), F32)

    row = lax.broadcasted_iota(jnp.int32, (q_, q_), 0)
    col = lax.broadcasted_iota(jnp.int32, (q_, q_), 1)
    causal = col <= row
    tril = jnp.where(causal, 1.0, 0.0).astype(BF16)
    lane = lax.broadcasted_iota(jnp.int32, (q_, LANES), 1)
    lo_half = lane < SSD_HEAD_DIM
    lane_row = lax.broadcasted_iota(jnp.int32, (1, LANES), 1)
    lo_half_row = lane_row < SSD_HEAD_DIM
    lo_half_f = lo_half.astype(F32)
    a_log2 = -jnp.exp(a_ref[...]) * LOG2_E

    def row_bcast(t, r):
        return jnp.tile(jnp.broadcast_to(t[r:r + 1, :], (8, q_)), (q_ // 8, 1))

    def chunk(ci, carry):
        r0 = pl.multiple_of(ci * q_, q_)
        rows = pl.ds(r0, q_)
        bc_w = SSD_GROUPS * SSD_STATE
        u_ref[halo:halo + q_, 0:SSD_INNER] = xs_ref[rows, :].astype(F32)
        u_ref[halo:halo + q_, SSD_INNER:SSD_INNER + bc_w] = bm_ref[rows, :].astype(F32)
        u_ref[halo:halo + q_, SSD_INNER + bc_w:SSD_CONV_DIM] = cm_ref[rows, :].astype(F32)
        for c0 in range(0, SSD_CONV_DIM, CONV_COLS):
            cs = slice(c0, c0 + CONV_COLS)
            u_cur = u_ref[halo:halo + q_, cs]
            lag_ref[...] = u_ref[halo - 1:halo - 1 + q_, cs]
            u_lag = lag_ref[...]
            pair_ref[halo:halo + q_, cs] = cw_ref[1:2, cs] * u_cur + cw_ref[0:1, cs] * u_lag
            conv = ((cb_ref[:, cs] + cw_ref[3:4, cs] * u_cur) + cw_ref[2:3, cs] * u_lag
                    + pair_ref[halo - 2:halo - 2 + q_, cs])
            act_ref[:, cs] = _silu(conv)
        u_ref[0:halo, :] = u_ref[q_:q_ + halo, :]
        pair_ref[0:halo, :] = pair_ref[q_:q_ + halo, :]
        dt_in = dt_ref[rows, :] + dtb_ref[...]
        dt = jnp.maximum(dt_in, 0.0) + jnp.log1p(jnp.exp(-jnp.abs(dt_in)))
        acum = _cumsum_rows(tril, dt * a_log2)
        acum_t = acum.T
        dt_t = dt.T
        ldt_t = jnp.log2(dt_t)
        rowp_t = acum_t - ldt_t
        w_t = dt_t * jnp.exp2(acum_t[:, q_ - 1:q_] - acum_t)
        c_decay = jnp.exp2(acum[q_ - 1:q_, :])
        for g in range(SSD_GROUPS):
            b_f = act_ref[:, SSD_INNER + g * SSD_STATE:SSD_INNER + (g + 1) * SSD_STATE]
            c_f = act_ref[:, SSD_INNER + (SSD_GROUPS + g) * SSD_STATE:SSD_INNER + (SSD_GROUPS + g + 1) * SSD_STATE]
            c_bf = c_f.astype(BF16)
            cbm = jnp.where(causal, _nt(c_bf, b_f.astype(BF16)), 0.0)
            b_t = b_f.T
            st = st_ref[g]
            y_off_all = _nn(c_bf, st.astype(BF16))
            ys, news, decs = [], [], []
            ss = jnp.zeros((q_, 1), F32)
            for j in range(SSD_GROUP_WIDTH // LANES):
                h1 = (g * SSD_GROUP_WIDTH + j * LANES) // SSD_HEAD_DIM
                ps = slice(g * SSD_GROUP_WIDTH + j * LANES, g * SSD_GROUP_WIDTH + (j + 1) * LANES)
                xs = act_ref[:, ps]
                intra, inter, e_l = [], [], []
                for hh in (h1, h1 + 1):
                    colb = jnp.broadcast_to(acum[:, hh:hh + 1], (q_, q_))
                    decay = jnp.exp2(jnp.minimum(colb - row_bcast(rowp_t, hh), row_bcast(ldt_t, hh)))
                    intra.append((cbm * decay).astype(BF16))
                    inter.append((b_t * row_bcast(w_t, hh)).astype(BF16))
                    e_l.append(jnp.exp2(colb))
                lhs = jnp.concatenate([jnp.concatenate(intra, axis=1), jnp.concatenate(inter, axis=1)], axis=0)
                xs_lo = xs * lo_half_f
                rhs = jnp.concatenate([xs_lo, xs - xs_lo], axis=0).astype(BF16)
                prod = _nn(lhs, rhs)
                y = prod[0:q_] + y_off_all[:, j * LANES:(j + 1) * LANES] * jnp.where(lo_half, e_l[0], e_l[1])
                y = y + dsk_ref[:, ps] * xs
                y = y * _silu(z_ref[rows, ps].astype(F32))
                ss = ss + jnp.sum(y * y, -1, keepdims=True)
                ys.append(y)
                news.append(prod[q_:2 * q_])
                decs.append(jnp.where(lo_half_row, c_decay[:, h1:h1 + 1], c_decay[:, h1 + 1:h1 + 2]))
            st_ref[g] = st * jnp.concatenate(decs, axis=1) + jnp.concatenate(news, axis=1)
            inv = lax.rsqrt(ss * (1.0 / SSD_GROUP_WIDTH) + RMS_EPS)
            for j, y in enumerate(ys):
                ps = slice(g * SSD_GROUP_WIDTH + j * LANES, g * SSD_GROUP_WIDTH + (j + 1) * LANES)
                o_ref[rows, ps] = (y * inv * nw_ref[:, ps]).astype(o_ref.dtype)
        return carry

    lax.fori_loop(0, n_chunks, chunk, 0)


def _ssd(h_main, h_small, conv_w, conv_b, dt_bias, a_log, d_skip, norm_w, batch, seq, tb=512):
    n = batch * seq
    nt = seq // tb
    bc_w = SSD_GROUPS * SSD_STATE
    pad = lambda v: jnp.zeros((1, LANES), F32).at[0, :SSD_HEADS].set(v.astype(F32))
    idx = lambda c: (lambda b, t: (b * nt + t, c))
    const = lambda b, t: (0, 0)
    return pl.pallas_call(
        functools.partial(_ssd_kernel, n_chunks=tb // SSD_CHUNK),
        grid=(batch, nt),
        in_specs=[
            pl.BlockSpec((tb, SSD_INNER), idx(L1_Z // SSD_INNER)),
            pl.BlockSpec((tb, SSD_INNER), idx(L1_XBC // SSD_INNER)),
            pl.BlockSpec((tb, bc_w), idx((L1_XBC + SSD_INNER) // bc_w)),
            pl.BlockSpec((tb, bc_w), idx((L1_XBC + SSD_INNER + bc_w) // bc_w)),
            pl.BlockSpec((tb, LANES), idx(0)),
            pl.BlockSpec((SSD_CONV, SSD_CONV_DIM), const),
            pl.BlockSpec((1, SSD_CONV_DIM), const),
            pl.BlockSpec((1, LANES), const),
            pl.BlockSpec((1, LANES), const),
            pl.BlockSpec((1, SSD_INNER), const),
            pl.BlockSpec((1, SSD_INNER), const),
        ],
        out_specs=pl.BlockSpec((tb, SSD_INNER), idx(0)),
        out_shape=jax.ShapeDtypeStruct((n, SSD_INNER), BF16),
        scratch_shapes=[
            pltpu.VMEM((SSD_CHUNK + 8, SSD_CONV_DIM), F32),
            pltpu.VMEM((SSD_CHUNK + 8, SSD_CONV_DIM), F32),
            pltpu.VMEM((SSD_CHUNK, CONV_COLS), F32),
            pltpu.VMEM((SSD_CHUNK, SSD_CONV_DIM), F32),
            pltpu.VMEM((SSD_GROUPS, SSD_STATE, SSD_GROUP_WIDTH), F32),
        ],
        compiler_params=_cparams(("arbitrary", "arbitrary")),
        name="ssd",
    )(h_main, h_main, h_main, h_main, h_small, conv_w, conv_b.reshape(1, -1),
      pad(dt_bias), pad(a_log), jnp.repeat(d_skip.astype(F32), SSD_HEAD_DIM).reshape(1, -1),
      norm_w.reshape(1, -1))


def _pad_cols(w, width=LANES):
    return jnp.zeros((w.shape[0], width), BF16).at[:, :w.shape[1]].set(w.astype(BF16))


def kernel(x, mem, positions, l0_w_in, l0_gla_w_gate, l0_gla_b_gate, l0_gla_norm_w, l0_mem_wk, l0_mem_wv, l0_w_out, l0_ln_w, l0_ln_b, l1_w_in, l1_conv_w, l1_conv_b, l1_dt_bias, l1_a_log, l1_d_skip, l1_ssd_norm_w, l1_mem_wk, l1_mem_wv, l1_w_out, l1_ln_w, l1_ln_b):
    batch, seq, dm = x.shape
    n = batch * seq
    x2 = x.reshape(n, dm)
    mem2 = mem.reshape(batch * MEM_LEN, dm)

    ga0 = L0_DQ + GLA_GATE_RANK
    w0_a = l0_w_in[:, :L0_DQ].astype(BF16)
    w0_b = l0_w_in[:, ga0:].astype(BF16)
    w0_small = _pad_cols(l0_w_in[:, L0_DQ:ga0])
    cos, sin = _rope_tables(positions)
    col_scale = jnp.ones((1, L0_MAIN), F32).at[0, L0_DQ:L0_DK].set(HEAD_DIM ** -0.5 * LOG2_E)
    h0, h0s = _project(x2, w0_a, w0_b, w0_small, rope=(cos, sin, col_scale, L0_DQ, L0_DV), tm=1024)
    o_a = _gla(h0, h0s, l0_gla_w_gate, l0_gla_b_gate, l0_gla_norm_w, batch, seq)
    o_b = _dilated_attention(h0, batch, seq)
    kv0 = _mem_kv(mem2, l0_mem_wk, l0_mem_wv)
    o_m = _mem_attn(h0, kv0, batch, seq, L0_MQ, L0_MG)
    x1, x1_bf = _out_proj_ln([o_a, o_b, o_m], l0_w_out, x2, l0_ln_w, l0_ln_b, emit_bf16=True)

    dt0 = SSD_INNER + SSD_CONV_DIM
    w1_a = l1_w_in[:, :dt0].astype(BF16)
    w1_b = l1_w_in[:, dt0 + SSD_HEADS:].astype(BF16)
    w1_small = _pad_cols(l1_w_in[:, dt0:dt0 + SSD_HEADS])
    h1, h1s = _project(x1_bf, w1_a, w1_b, w1_small)
    y = _ssd(h1, h1s, l1_conv_w, l1_conv_b, l1_dt_bias, l1_a_log, l1_d_skip, l1_ssd_norm_w, batch, seq)
    kv1 = _mem_kv(mem2, l1_mem_wk, l1_mem_wv)
    o_m1 = _mem_attn(h1, kv1, batch, seq, L1_MQ, L1_MG)
    (x2_out,) = _out_proj_ln([y, o_m1], l1_w_out, x1, l1_ln_w, l1_ln_b)
    return x2_out.reshape(batch, seq, dm)
```

```python
import functools

import jax
import jax.numpy as jnp
from jax import lax
from jax.experimental import pallas as pl
from jax.experimental.pallas import tpu as pltpu

F32 = jnp.float32
BF16 = jnp.bfloat16

DEPTH = 2
DEEPNORM_ALPHA = (2 * DEPTH) ** 0.25
LN_EPS = 1e-5
RMS_EPS = 1e-6
ROPE_THETA = 10000.0
MEM_LEN = 256

GLA_HEADS = 4
GLA_DK = 128
GLA_DV = 256
GLA_GATE_RANK = 16
GLA_GATE_NORMALIZER = 16.0
GLA_CHUNK = 64

DIL_PATTERNS = ((128, 1), (512, 4), (2048, 16))
DIL_HEADS = 4
HEAD_DIM = 128
DIL_BLOCK = 128

MEM_HEADS = 4

SSD_HEADS = 24
SSD_HEAD_DIM = 64
SSD_GROUPS = 4
SSD_STATE = 128
SSD_CONV = 4
SSD_CHUNK = 128
SSD_INNER = SSD_HEADS * SSD_HEAD_DIM
SSD_GROUP_WIDTH = SSD_INNER // SSD_GROUPS
SSD_CONV_DIM = SSD_INNER + 2 * SSD_GROUPS * SSD_STATE
CONV_COLS = 256

LANES = 128
LOG2_E = 1.4426950408889634
VMEM_LIMIT = 48 * 1024 * 1024

L0_GQ, L0_GK, L0_GV, L0_GG = 0, 512, 1024, 2048
L0_DQ, L0_DK, L0_DV, L0_DG = 3072, 4608, 5120, 5632
L0_MQ, L0_MG = 6144, 6656
L0_MAIN = 7168
L1_Z, L1_XBC, L1_MQ, L1_MG = 0, 1536, 4096, 4608
L1_MAIN = 5120


def _nt(a, b):
    return lax.dot_general(a, b, (((1,), (1,)), ((), ())), preferred_element_type=F32)


def _tn(a, b):
    return lax.dot_general(a, b, (((0,), (0,)), ((), ())), preferred_element_type=F32)


def _nn(a, b):
    return jnp.dot(a, b, preferred_element_type=F32)


def _cumsum_rows(tril, x):
    hi = x.astype(BF16)
    rest = x - hi.astype(F32)
    mid = rest.astype(BF16)
    lo = (rest - mid.astype(F32)).astype(BF16)
    return _nn(tril, hi) + _nn(tril, mid) + _nn(tril, lo)


def _silu(x):
    return 0.5 * x * (1.0 + jnp.tanh(0.5 * x))


def _cparams(sem):
    return pltpu.CompilerParams(dimension_semantics=sem, vmem_limit_bytes=VMEM_LIMIT)


def _rope_table_kernel(pos_ref, inv_ref, sign_ref, x_ref, cos_ref, sin_ref, xb_ref):
    xb_ref[...] = x_ref[...].astype(BF16)
    rows = pos_ref.shape[0]
    half = HEAD_DIM // 2
    ang = pos_ref[...].astype(F32) * inv_ref[...]
    cos = jnp.cos(ang)
    sin = jnp.sin(ang)
    lo = lax.broadcasted_iota(jnp.int32, (rows, HEAD_DIM), 1) < half
    sign = sign_ref[...]
    for table, ref, scale in ((cos, cos_ref, None), (sin, sin_ref, sign)):
        swapped = pltpu.roll(table, half, 1)
        first = jnp.where(lo, table, swapped)
        second = jnp.where(lo, swapped, table)
        if scale is not None:
            first, second = first * scale, second * scale
        ref[0:rows, :] = first
        ref[rows:2 * rows, :] = second


def _rope_tables(positions, x):
    n, dm = x.shape
    half = HEAD_DIM // 2
    inv = 1.0 / (ROPE_THETA ** (jnp.arange(0, HEAD_DIM, 2, dtype=F32) / HEAD_DIM))
    inv2 = jnp.concatenate([inv, inv]).reshape(1, HEAD_DIM)
    sign = jnp.concatenate([-jnp.ones((half,), F32), jnp.ones((half,), F32)]).reshape(1, HEAD_DIM)
    tb = 2048
    pos_pair = positions.reshape(n // tb, 2, tb // 2).transpose(0, 2, 1)
    pos_pair = jnp.repeat(pos_pair, half, axis=2).reshape(n // 2, HEAD_DIM)
    row = pl.BlockSpec((tb, HEAD_DIM), lambda i: (i, 0))
    vec = pl.BlockSpec((1, HEAD_DIM), lambda i: (0, 0))
    return pl.pallas_call(
        _rope_table_kernel,
        grid=(n // tb,),
        in_specs=[pl.BlockSpec((tb // 2, HEAD_DIM), lambda i: (i, 0)), vec, vec,
                  pl.BlockSpec((tb, dm), lambda i: (i, 0))],
        out_specs=[row, row, pl.BlockSpec((tb, dm), lambda i: (i, 0))],
        out_shape=[jax.ShapeDtypeStruct((n, HEAD_DIM), F32)] * 2 + [jax.ShapeDtypeStruct((n, dm), BF16)],
        compiler_params=_cparams(("arbitrary",)),
        name="rope_tables",
    )(pos_pair, inv2, sign, x)


SUB_COLS = 256


def _proj_kernel(*refs, groups, cast_x, has_rope):
    it = iter(refs)
    x_ref, wa_ref, wb_ref, ws_ref = next(it), next(it), next(it), next(it)
    if has_rope:
        c_ref, s_ref, sc_ref = next(it), next(it), next(it)
    o_ref, os_ref = next(it), next(it)
    xb_ref = next(it) if cast_x else x_ref
    j = pl.program_id(1)

    @pl.when(j == 0)
    def _():
        if cast_x:
            xb_ref[...] = x_ref[...].astype(BF16)
        os_ref[...] = _nn(xb_ref[...], ws_ref[...])

    def rope_block(acc, cols):
        cos, sin = c_ref[...], s_ref[...]
        for k in range(SUB_COLS // HEAD_DIM):
            sl = slice(cols.start + k * HEAD_DIM, cols.start + (k + 1) * HEAD_DIM)
            a = acc[:, k * HEAD_DIM:(k + 1) * HEAD_DIM] * sc_ref[:, sl]
            o_ref[:, sl] = (a * cos + pltpu.roll(a, HEAD_DIM // 2, 1) * sin).astype(o_ref.dtype)

    def emit(w_ref, kinds):
        if kinds is None:
            o_ref[...] = _nn(xb_ref[...], w_ref[...]).astype(o_ref.dtype)
            return
        for sb, kind in enumerate(kinds):
            cols = slice(sb * SUB_COLS, (sb + 1) * SUB_COLS)
            acc = _nn(xb_ref[...], w_ref[:, cols])
            if kind == "rope":
                rope_block(acc, cols)
            else:
                o_ref[:, cols] = acc.astype(o_ref.dtype)

    for lo, hi, src, kinds in groups:
        pl.when(jnp.logical_and(j >= lo, j < hi))(functools.partial(emit, wb_ref if src else wa_ref, kinds))


def _project(x, w_a, a_cols, w_b, w_small, rope=None, tm=2048, tn=1024):
    n, kdim = x.shape
    m = a_cols + w_b.shape[1]
    split = a_cols // tn
    n_tiles = m // tn
    cast_x = x.dtype != BF16
    per_tile = tn // SUB_COLS
    kinds = [["plain"] * per_tile for _ in range(n_tiles)]
    in_specs = [
        pl.BlockSpec((tm, kdim), lambda i, j: (i, 0)),
        pl.BlockSpec((kdim, tn), lambda i, j: (0, jnp.minimum(j, split - 1))),
        pl.BlockSpec((kdim, tn), lambda i, j: (0, jnp.maximum(j - split, 0))),
        pl.BlockSpec((kdim, LANES), lambda i, j: (0, 0)),
    ]
    args = [x, w_a, w_b, w_small]
    scratch = [pltpu.VMEM((tm, kdim), BF16)] if cast_x else []
    if rope is not None:
        cos, sin, col_scale, first_col, last_col = rope
        for c in range(first_col // SUB_COLS, last_col // SUB_COLS):
            kinds[c // per_tile][c % per_tile] = "rope"
        in_specs += [
            pl.BlockSpec((tm, HEAD_DIM), lambda i, j: (i, 0)),
            pl.BlockSpec((tm, HEAD_DIM), lambda i, j: (i, 0)),
            pl.BlockSpec((1, tn), lambda i, j: (0, j)),
        ]
        args += [cos, sin, col_scale]
    groups = []
    for t in range(n_tiles):
        plan = None if all(k == "plain" for k in kinds[t]) else tuple(kinds[t])
        src = int(t >= split)
        if groups and groups[-1][2] == src and groups[-1][3] == plan:
            groups[-1] = (groups[-1][0], t + 1, src, plan)
        else:
            groups.append((t, t + 1, src, plan))
    return pl.pallas_call(
        functools.partial(_proj_kernel, groups=tuple(groups), cast_x=cast_x, has_rope=rope is not None),
        grid=(n // tm, n_tiles),
        in_specs=in_specs,
        out_specs=[
            pl.BlockSpec((tm, tn), lambda i, j: (i, j)),
            pl.BlockSpec((tm, LANES), lambda i, j: (i, 0)),
        ],
        out_shape=[jax.ShapeDtypeStruct((n, m), BF16), jax.ShapeDtypeStruct((n, LANES), F32)],
        scratch_shapes=scratch,
        compiler_params=_cparams(("arbitrary", "arbitrary")),
        name="in_proj",
    )(*args)


GLA_BATCH_GROUP = 2


def _gla_kernel(q_ref, k_ref, v_ref, g_ref, ga_ref, wg_ref, bg_ref, nw_ref, o_ref, st_ref, la_ref, *, n_chunks):
    c_ = GLA_CHUNK
    group = q_ref.shape[0]

    @pl.when(pl.program_id(1) == 0)
    def _():
        st_ref[...] = jnp.zeros_like(st_ref)

    p_ = 2 * c_
    row = lax.broadcasted_iota(jnp.int32, (p_, p_), 0)
    col = lax.broadcasted_iota(jnp.int32, (p_, p_), 1)
    same_chunk = (row < c_) == (col < c_)
    tril = jnp.where(jnp.logical_and(same_chunk, col <= row), 1.0, 0.0).astype(BF16)
    row_a = lax.broadcasted_iota(jnp.int32, (c_, c_), 0)
    col_a = lax.broadcasted_iota(jnp.int32, (c_, c_), 1)
    causal_a = col_a <= row_a
    row_b = lax.broadcasted_iota(jnp.int32, (c_, p_), 0)
    col_b = lax.broadcasted_iota(jnp.int32, (c_, p_), 1)
    causal_b = col_b - c_ <= row_b
    in_a = lax.broadcasted_iota(jnp.int32, (p_, 1), 0) < c_
    nw = nw_ref[...]
    for b in range(group):
        z = _nn(ga_ref[b].astype(BF16), wg_ref[...]) + bg_ref[...]
        la_ref[b] = (jnp.minimum(z, 0.0) - jnp.log1p(jnp.exp(-jnp.abs(z)))) * (1.0 / GLA_GATE_NORMALIZER)

    def pair(pi, carry):
        r0 = pl.multiple_of(pi * p_, p_)
        rows = pl.ds(r0, p_)
        gates = []
        for b in range(group):
            bcum = _cumsum_rows(tril, la_ref[b, rows, :])
            blast_a = bcum[c_ - 1:c_, :]
            blast_b = bcum[p_ - 1:p_, :]
            blast = jnp.where(in_a, blast_a, blast_b)
            gates.append((jnp.exp(bcum), jnp.exp(-bcum), jnp.exp(blast - bcum), jnp.exp(blast_a), jnp.exp(blast_b)))
        for h in range(GLA_HEADS):
            for b in range(group):
                e_pos, e_neg, e_tail, d_a, d_b = gates[b]
                ks = slice(h * GLA_DK, (h + 1) * GLA_DK)
                vs = slice(h * GLA_DV, (h + 1) * GLA_DV)
                q = q_ref[b, rows, ks].astype(F32) * (GLA_DK ** -0.5)
                k = k_ref[b, rows, ks].astype(F32)
                q_dec = q * e_pos[:, ks]
                k_tail = k * e_tail[:, ks]
                q_dec_bf = q_dec.astype(BF16)
                k_inv_bf = (k * e_neg[:, ks]).astype(BF16)
                k_tail_bf = k_tail.astype(BF16)
                v = v_ref[b, rows, vs]
                s_a = jnp.where(causal_a, _nt(q_dec_bf[0:c_], k_inv_bf[0:c_]), 0.0)
                keys_b = jnp.concatenate([k_tail_bf[0:c_], k_inv_bf[c_:p_]], axis=0)
                s_b = jnp.where(causal_b, _nt(q_dec_bf[c_:p_], keys_b), 0.0)
                o_intra = jnp.concatenate([_nn(s_a.astype(BF16), v[0:c_]), _nn(s_b.astype(BF16), v)], axis=0)
                q_state = jnp.concatenate([q_dec_bf[0:c_], (q_dec[c_:p_] * d_a[:, ks]).astype(BF16)], axis=0)
                st = st_ref[b, h]
                o = o_intra + _nt(q_state, st.astype(BF16))
                k_end = jnp.concatenate([(k_tail[0:c_] * d_b[:, ks]).astype(BF16), k_tail_bf[c_:p_]], axis=0)
                st_ref[b, h] = st * (d_a[:, ks] * d_b[:, ks]) + _tn(v, k_end)
                ms = jnp.mean(o * o, axis=-1, keepdims=True)
                o = o * lax.rsqrt(ms + RMS_EPS) * nw
                o_ref[b, rows, vs] = (o * _silu(g_ref[b, rows, vs].astype(F32))).astype(o_ref.dtype)
        return carry

    lax.fori_loop(0, n_chunks // 2, pair, 0)


def _gla(h_main, h_small, w_gate, b_gate, norm_w, batch, seq, tb=512):
    n = batch * seq
    nt = seq // tb
    group = GLA_BATCH_GROUP if batch % GLA_BATCH_GROUP == 0 else 1
    wg = jnp.zeros((LANES, GLA_HEADS * GLA_DK), BF16).at[:GLA_GATE_RANK].set(w_gate.astype(BF16))
    h4 = h_main.reshape(batch // group, group, seq, h_main.shape[1])
    s4 = h_small.reshape(batch // group, group, seq, LANES)
    idx = lambda c: (lambda b, t: (b, 0, t, c))
    const = lambda b, t: (0, 0)
    width = GLA_HEADS * GLA_DV
    out = pl.pallas_call(
        functools.partial(_gla_kernel, n_chunks=tb // GLA_CHUNK),
        grid=(batch // group, nt),
        in_specs=[
            pl.BlockSpec((None, group, tb, 512), idx(L0_GQ // 512)),
            pl.BlockSpec((None, group, tb, 512), idx(L0_GK // 512)),
            pl.BlockSpec((None, group, tb, 1024), idx(L0_GV // 1024)),
            pl.BlockSpec((None, group, tb, 1024), idx(L0_GG // 1024)),
            pl.BlockSpec((None, group, tb, LANES), idx(0)),
            pl.BlockSpec((LANES, 512), const),
            pl.BlockSpec((1, 512), const),
            pl.BlockSpec((1, GLA_DV), const),
        ],
        out_specs=pl.BlockSpec((None, group, tb, width), idx(0)),
        out_shape=jax.ShapeDtypeStruct((batch // group, group, seq, width), BF16),
        scratch_shapes=[pltpu.VMEM((group, GLA_HEADS, GLA_DV, GLA_DK), F32),
                        pltpu.VMEM((group, tb, GLA_HEADS * GLA_DK), F32)],
        compiler_params=_cparams(("arbitrary", "arbitrary")),
        name="gla",
    )(h4, h4, h4, h4, s4, wg, b_gate.reshape(1, -1), norm_w.reshape(1, -1))
    return out.reshape(n, width)


DIL_WINDOW = 2048
SLAB_GROUP = 16
SLAB_PITCH = 24
SEG = 32
TILE_UNROLL = 8


def _dil_kernel(q0_ref, q1_ref, q2_ref, kc_ref, kp_ref, vc_ref, vp_ref, g_ref, out_ref,
                kcat, vcat, ks, vs, q1s, q2s, o0s, l0s, o1s, l1s, o2s, l2s, bias):
    blk = DIL_BLOCK
    w = DIL_WINDOW
    groups = w // SLAB_GROUP
    t = pl.program_id(1)
    first_window = t == 0

    @pl.when(jnp.logical_and(jnp.logical_and(pl.program_id(0) == 0, t == 0), pl.program_id(2) == 0))
    def _():
        rho = lax.broadcasted_iota(jnp.int32, (blk, 2 * blk), 0)
        kap = lax.broadcasted_iota(jnp.int32, (blk, 2 * blk), 1)
        in_prev = kap < blk
        kap_l = jnp.where(in_prev, kap, kap - blk)
        perm = lambda x: 4 * (x % SEG) + x // SEG
        neg = jnp.float32(-jnp.inf)
        for slot, (jq, jk) in enumerate(((rho, kap_l), (perm(rho), perm(kap_l)))):
            ok_prev = jnp.where(jk >= jq, 0.0, neg)
            ok_cur = jnp.where(jk <= jq, 0.0, neg)
            bias[2 * slot] = jnp.where(in_prev, ok_prev, ok_cur)
            bias[2 * slot + 1] = jnp.where(in_prev, neg, ok_cur)

    kcat[0:blk, :] = kp_ref[w - blk:w, :]
    kcat[blk:blk + w, :] = kc_ref[...]
    vcat[0:blk, :] = vp_ref[w - blk:w, :]
    vcat[blk:blk + w, :] = vc_ref[...]

    def to_slabs(i, carry):
        src = pl.ds(pl.multiple_of(i * SLAB_GROUP, SLAB_GROUP), SLAB_GROUP)
        prev = pl.ds(pl.multiple_of(i * SLAB_PITCH, 8), SLAB_GROUP)
        cur = pl.ds(pl.multiple_of((groups + i) * SLAB_PITCH, 8), SLAB_GROUP)
        ks[prev, :] = kp_ref[src, :].astype(F32)
        ks[cur, :] = kc_ref[src, :].astype(F32)
        vs[prev, :] = vp_ref[src, :].astype(F32)
        vs[cur, :] = vc_ref[src, :].astype(F32)
        q1s[prev, :] = q1_ref[src, :].astype(F32)
        q2s[prev, :] = q2_ref[src, :].astype(F32)
        return carry

    lax.fori_loop(0, groups, to_slabs, 0, unroll=4)

    def attend(q, k, v, b):
        s = _nt(q, k) + b
        m = jnp.max(s, -1, keepdims=True)
        p = jnp.exp2(s - m)
        den = jnp.sum(p, -1, keepdims=True)
        o = _nn(p.astype(BF16), v) * (1.0 / den)
        return o, jnp.broadcast_to(m + jnp.log2(den), (blk, LANES))

    def group0(i, carry):
        r0 = pl.multiple_of(i * blk, blk)
        sel = jnp.where(jnp.logical_and(first_window, i == 0), 1, 0)
        o, l = attend(q0_ref[pl.ds(r0, blk), :], kcat[pl.ds(r0, 2 * blk), :], vcat[pl.ds(r0, 2 * blk), :],
                      bias[sel])
        o0s[pl.ds(r0, blk), :] = o
        l0s[pl.ds(r0, blk), :] = l
        return carry

    lax.fori_loop(0, w // blk, group0, 0, unroll=TILE_UNROLL)

    def seg(ref, group, b):
        return ref[pl.ds(group * SLAB_PITCH + b, SEG, stride=SLAB_PITCH), :]

    def group1(idx, carry):
        r = idx // 4
        n = idx % 4
        g_cur = groups + SEG * n
        q = jnp.concatenate([seg(q1s, SEG * n, r + 4 * j) for j in range(4)], axis=0).astype(BF16)
        k = jnp.concatenate([seg(ks, g_cur - SEG, r + 4 * j) for j in range(4)]
                            + [seg(ks, g_cur, r + 4 * j) for j in range(4)], axis=0).astype(BF16)
        v = jnp.concatenate([seg(vs, g_cur - SEG, r + 4 * j) for j in range(4)]
                            + [seg(vs, g_cur, r + 4 * j) for j in range(4)], axis=0).astype(BF16)
        sel = jnp.where(jnp.logical_and(first_window, n == 0), 3, 2)
        o, l = attend(q, k, v, bias[sel])
        for j in range(4):
            dst = pl.ds(SEG * n * SLAB_PITCH + r + 4 * j, SEG, stride=SLAB_PITCH)
            o1s[dst, :] = o[j * SEG:(j + 1) * SEG]
            l1s[dst, :] = l[j * SEG:(j + 1) * SEG]
        return carry

    lax.fori_loop(0, 16, group1, 0, unroll=TILE_UNROLL)

    def group2(r, carry):
        prev = pl.ds(r, blk, stride=SLAB_PITCH)
        cur = pl.ds(groups * SLAB_PITCH + r, blk, stride=SLAB_PITCH)
        q = q2s[prev, :].astype(BF16)
        k = jnp.concatenate([ks[prev, :], ks[cur, :]], axis=0).astype(BF16)
        v = jnp.concatenate([vs[prev, :], vs[cur, :]], axis=0).astype(BF16)
        o, l = attend(q, k, v, bias[jnp.where(first_window, 1, 0)])
        o2s[prev, :] = o
        l2s[prev, :] = l
        return carry

    lax.fori_loop(0, SLAB_GROUP, group2, 0, unroll=TILE_UNROLL)

    def merge(i, carry):
        r0 = pl.multiple_of(i * blk, blk)
        rows = pl.ds(r0, blk)
        pieces = [pl.ds(pl.multiple_of((i * (blk // SLAB_GROUP) + j) * SLAB_PITCH, 8), SLAB_GROUP)
                  for j in range(blk // SLAB_GROUP)]
        gather = lambda ref: jnp.concatenate([ref[p, :] for p in pieces], axis=0)
        l0, l1, l2 = l0s[rows, :], gather(l1s), gather(l2s)
        m = jnp.maximum(jnp.maximum(l0, l1), l2)
        w0, w1, w2 = jnp.exp2(l0 - m), jnp.exp2(l1 - m), jnp.exp2(l2 - m)
        o = (w0 * o0s[rows, :] + w1 * gather(o1s) + w2 * gather(o2s)) * (1.0 / (w0 + w1 + w2))
        out_ref[rows, :] = (o * _silu(g_ref[rows, :].astype(F32))).astype(out_ref.dtype)
        return carry

    lax.fori_loop(0, w // blk, merge, 0)


def _dilated_attention(h_main, batch, seq):
    n = batch * seq
    w = DIL_WINDOW
    nt = seq // w
    slab_rows = (w // SLAB_GROUP) * SLAB_PITCH

    def cur(col):
        return pl.BlockSpec((w, HEAD_DIM), lambda b, t, h: (b * nt + t, col // HEAD_DIM + h))

    def prev(col):
        return pl.BlockSpec((w, HEAD_DIM), lambda b, t, h: (b * nt + jnp.maximum(t - 1, 0), col // HEAD_DIM + h))

    width = DIL_HEADS * HEAD_DIM
    slab = lambda rows: pltpu.VMEM((rows, LANES), F32)
    return pl.pallas_call(
        _dil_kernel,
        grid=(batch, nt, DIL_HEADS),
        in_specs=[cur(L0_DQ), cur(L0_DQ + width), cur(L0_DQ + 2 * width),
                  cur(L0_DK), prev(L0_DK), cur(L0_DV), prev(L0_DV), cur(L0_DG)],
        out_specs=pl.BlockSpec((w, HEAD_DIM), lambda b, t, h: (b * nt + t, h)),
        out_shape=jax.ShapeDtypeStruct((n, width), BF16),
        scratch_shapes=[
            pltpu.VMEM((w + DIL_BLOCK, HEAD_DIM), BF16), pltpu.VMEM((w + DIL_BLOCK, HEAD_DIM), BF16),
            slab(2 * slab_rows), slab(2 * slab_rows), slab(slab_rows), slab(slab_rows),
            slab(w), slab(w), slab(slab_rows), slab(slab_rows), slab(slab_rows), slab(slab_rows),
            pltpu.VMEM((4, DIL_BLOCK, 2 * DIL_BLOCK), F32),
        ],
        compiler_params=_cparams(("arbitrary", "arbitrary", "arbitrary")),
        name="dilated_attn",
    )(*([h_main] * 8))


def _mem_kv_kernel(m_ref, w_ref, o_ref):
    o_ref[...] = _nn(m_ref[...].astype(BF16), w_ref[...]).astype(o_ref.dtype)


def _mem_kv(mem2d, wk, wv):
    rows, kdim = mem2d.shape
    w = jnp.concatenate([wk, wv], axis=1).astype(BF16)
    tm = MEM_LEN
    return pl.pallas_call(
        _mem_kv_kernel,
        grid=(rows // tm,),
        in_specs=[pl.BlockSpec((tm, kdim), lambda i: (i, 0)), pl.BlockSpec(w.shape, lambda i: (0, 0))],
        out_specs=pl.BlockSpec((tm, w.shape[1]), lambda i: (i, 0)),
        out_shape=jax.ShapeDtypeStruct((rows, w.shape[1]), BF16),
        compiler_params=_cparams(("arbitrary",)),
        name="mem_kv",
    )(mem2d, w)


def _mem_attn_kernel(q_ref, g_ref, k_ref, v_ref, o_ref, *, tq):
    for h in range(MEM_HEADS):
        hs = slice(h * HEAD_DIM, (h + 1) * HEAD_DIM)
        s = _nt(q_ref[:, hs], k_ref[:, hs]) * (HEAD_DIM ** -0.5 * LOG2_E)
        m = jnp.max(s, -1, keepdims=True)
        p = jnp.exp2(s - m)
        den = jnp.sum(p, -1, keepdims=True)
        o = _nn(p.astype(BF16), v_ref[:, hs]) * (1.0 / den)
        o_ref[:, hs] = (o * _silu(g_ref[:, hs].astype(F32))).astype(o_ref.dtype)


def _mem_attn(h_main, kv, batch, seq, q_col, g_col, tq=1024):
    n = batch * seq
    width = MEM_HEADS * HEAD_DIM
    nt = seq // tq
    return pl.pallas_call(
        functools.partial(_mem_attn_kernel, tq=tq),
        grid=(batch, nt),
        in_specs=[
            pl.BlockSpec((tq, width), lambda b, t: (b * nt + t, q_col // width)),
            pl.BlockSpec((tq, width), lambda b, t: (b * nt + t, g_col // width)),
            pl.BlockSpec((MEM_LEN, width), lambda b, t: (b, 0)),
            pl.BlockSpec((MEM_LEN, width), lambda b, t: (b, 1)),
        ],
        out_specs=pl.BlockSpec((tq, width), lambda b, t: (b * nt + t, 0)),
        out_shape=jax.ShapeDtypeStruct((n, width), BF16),
        compiler_params=_cparams(("arbitrary", "arbitrary")),
        name="mem_attn",
    )(h_main, h_main, kv, kv)


OUT_SUB_ROWS = 256


def _out_kernel(*refs, n_parts, emit_bf16):
    a_refs = refs[:n_parts]
    w_refs = refs[n_parts:2 * n_parts]
    x_ref, lw_ref, lb_ref, o_ref = refs[2 * n_parts:2 * n_parts + 4]
    for r0 in range(0, x_ref.shape[0], OUT_SUB_ROWS):
        rows = slice(r0, r0 + OUT_SUB_ROWS)
        acc = DEEPNORM_ALPHA * x_ref[rows, :]
        for a_ref, w_ref in zip(a_refs, w_refs):
            acc = acc + _nn(a_ref[rows, :], w_ref[...])
        mu = jnp.mean(acc, -1, keepdims=True)
        d = acc - mu
        var = jnp.mean(d * d, -1, keepdims=True)
        y = d * lax.rsqrt(var + LN_EPS) * lw_ref[...] + lb_ref[...]
        o_ref[rows, :] = y
        if emit_bf16:
            refs[2 * n_parts + 4][rows, :] = y.astype(BF16)


def _out_proj_ln(parts, w_out, x, ln_w, ln_b, emit_bf16=False, tm=1024):
    n, dm = x.shape
    w_bf = w_out.astype(BF16)
    w_specs, off = [], 0
    for p in parts:
        width = p.shape[1]
        w_specs.append(pl.BlockSpec((width, dm), functools.partial(lambda i, blk: (blk, 0), blk=off // width)))
        assert off % width == 0
        off += width
    row = lambda width: pl.BlockSpec((tm, width), lambda i: (i, 0))
    const = lambda shape: pl.BlockSpec(shape, lambda i: (0, 0))
    out_specs = [row(dm)] + ([row(dm)] if emit_bf16 else [])
    out_shape = [jax.ShapeDtypeStruct((n, dm), F32)] + ([jax.ShapeDtypeStruct((n, dm), BF16)] if emit_bf16 else [])
    return pl.pallas_call(
        functools.partial(_out_kernel, n_parts=len(parts), emit_bf16=emit_bf16),
        grid=(n // tm,),
        in_specs=[row(p.shape[1]) for p in parts] + w_specs + [row(dm), const((1, dm)), const((1, dm))],
        out_specs=out_specs,
        out_shape=out_shape,
        compiler_params=_cparams(("arbitrary",)),
        name="out_proj_ln",
    )(*parts, *([w_bf] * len(parts)), x, ln_w.reshape(1, dm), ln_b.reshape(1, dm))


def _ssd_kernel(z_ref, xs_ref, bm_ref, cm_ref, dt_ref, cw_ref, cb_ref, dtb_ref, a_ref, dsk_ref, nw_ref,
                o_ref, u_ref, pair_ref, lag_ref, act_ref, st_ref, *, n_chunks):
    q_ = SSD_CHUNK
    halo = 8

    @pl.when(pl.program_id(1) == 0)
    def _():
        st_ref[...] = jnp.zeros_like(st_ref)
        u_ref[0:halo, :] = jnp.zeros((halo, SSD_CONV_DIM), F32)
        pair_ref[0:halo, :] = jnp.zeros((halo, SSD_CONV_DIM), F32)

    row = lax.broadcasted_iota(jnp.int32, (q_, q_), 0)
    col = lax.broadcasted_iota(jnp.int32, (q_, q_), 1)
    causal = col <= row
    tril = jnp.where(causal, 1.0, 0.0).astype(BF16)
    lane = lax.broadcasted_iota(jnp.int32, (q_, LANES), 1)
    lo_half = lane < SSD_HEAD_DIM
    lane_row = lax.broadcasted_iota(jnp.int32, (1, LANES), 1)
    lo_half_row = lane_row < SSD_HEAD_DIM
    lo_half_f = lo_half.astype(F32)
    a_log2 = -jnp.exp(a_ref[...]) * LOG2_E

    def row_bcast(t, r):
        return jnp.tile(jnp.broadcast_to(t[r:r + 1, :], (8, q_)), (q_ // 8, 1))

    def chunk(ci, carry):
        r0 = pl.multiple_of(ci * q_, q_)
        rows = pl.ds(r0, q_)
        bc_w = SSD_GROUPS * SSD_STATE
        u_ref[halo:halo + q_, 0:SSD_INNER] = xs_ref[rows, :].astype(F32)
        u_ref[halo:halo + q_, SSD_INNER:SSD_INNER + bc_w] = bm_ref[rows, :].astype(F32)
        u_ref[halo:halo + q_, SSD_INNER + bc_w:SSD_CONV_DIM] = cm_ref[rows, :].astype(F32)
        for c0 in range(0, SSD_CONV_DIM, CONV_COLS):
            cs = slice(c0, c0 + CONV_COLS)
            u_cur = u_ref[halo:halo + q_, cs]
            lag_ref[...] = u_ref[halo - 1:halo - 1 + q_, cs]
            u_lag = lag_ref[...]
            pair_ref[halo:halo + q_, cs] = cw_ref[1:2, cs] * u_cur + cw_ref[0:1, cs] * u_lag
            conv = ((cb_ref[:, cs] + cw_ref[3:4, cs] * u_cur) + cw_ref[2:3, cs] * u_lag
                    + pair_ref[halo - 2:halo - 2 + q_, cs])
            act_ref[:, cs] = _silu(conv)
        u_ref[0:halo, :] = u_ref[q_:q_ + halo, :]
        pair_ref[0:halo, :] = pair_ref[q_:q_ + halo, :]
        dt_in = dt_ref[rows, :] + dtb_ref[...]
        dt = jnp.maximum(dt_in, 0.0) + jnp.log1p(jnp.exp(-jnp.abs(dt_in)))
        acum = _cumsum_rows(tril, dt * a_log2)
        acum_t = acum.T
        dt_t = dt.T
        ldt_t = jnp.log2(dt_t)
        rowp_t = acum_t - ldt_t
        w_t = dt_t * jnp.exp2(acum_t[:, q_ - 1:q_] - acum_t)
        c_decay = jnp.exp2(acum[q_ - 1:q_, :])
        for g in range(SSD_GROUPS):
            b_f = act_ref[:, SSD_INNER + g * SSD_STATE:SSD_INNER + (g + 1) * SSD_STATE]
            c_f = act_ref[:, SSD_INNER + (SSD_GROUPS + g) * SSD_STATE:SSD_INNER + (SSD_GROUPS + g + 1) * SSD_STATE]
            c_bf = c_f.astype(BF16)
            cbm = jnp.where(causal, _nt(c_bf, b_f.astype(BF16)), 0.0)
            b_t = b_f.T
            st = st_ref[g]
            y_off_all = _nn(c_bf, st.astype(BF16))
            ys, news, decs = [], [], []
            ss = jnp.zeros((q_, 1), F32)
            for j in range(SSD_GROUP_WIDTH // LANES):
                h1 = (g * SSD_GROUP_WIDTH + j * LANES) // SSD_HEAD_DIM
                ps = slice(g * SSD_GROUP_WIDTH + j * LANES, g * SSD_GROUP_WIDTH + (j + 1) * LANES)
                xs = act_ref[:, ps]
                intra, inter, e_l = [], [], []
                for hh in (h1, h1 + 1):
                    colb = jnp.broadcast_to(acum[:, hh:hh + 1], (q_, q_))
                    decay = jnp.exp2(jnp.minimum(colb - row_bcast(rowp_t, hh), row_bcast(ldt_t, hh)))
                    intra.append((cbm * decay).astype(BF16))
                    inter.append((b_t * row_bcast(w_t, hh)).astype(BF16))
                    e_l.append(jnp.exp2(colb))
                lhs = jnp.concatenate([jnp.concatenate(intra, axis=1), jnp.concatenate(inter, axis=1)], axis=0)
                xs_lo = xs * lo_half_f
                rhs = jnp.concatenate([xs_lo, xs - xs_lo], axis=0).astype(BF16)
                prod = _nn(lhs, rhs)
                y = prod[0:q_] + y_off_all[:, j * LANES:(j + 1) * LANES] * jnp.where(lo_half, e_l[0], e_l[1])
                y = y + dsk_ref[:, ps] * xs
                y = y * _silu(z_ref[rows, ps].astype(F32))
                ss = ss + jnp.sum(y * y, -1, keepdims=True)
                ys.append(y)
                news.append(prod[q_:2 * q_])
                decs.append(jnp.where(lo_half_row, c_decay[:, h1:h1 + 1], c_decay[:, h1 + 1:h1 + 2]))
            st_ref[g] = st * jnp.concatenate(decs, axis=1) + jnp.concatenate(news, axis=1)
            inv = lax.rsqrt(ss * (1.0 / SSD_GROUP_WIDTH) + RMS_EPS)
            for j, y in enumerate(ys):
                ps = slice(g * SSD_GROUP_WIDTH + j * LANES, g * SSD_GROUP_WIDTH + (j + 1) * LANES)
                o_ref[rows, ps] = (y * inv * nw_ref[:, ps]).astype(o_ref.dtype)
        return carry

    lax.fori_loop(0, n_chunks, chunk, 0)


def _ssd(h_main, h_small, conv_w, conv_b, dt_bias, a_log, d_skip, norm_w, batch, seq, tb=512):
    n = batch * seq
    nt = seq // tb
    bc_w = SSD_GROUPS * SSD_STATE
    pad = lambda v: jnp.zeros((1, LANES), F32).at[0, :SSD_HEADS].set(v.astype(F32))
    idx = lambda c: (lambda b, t: (b * nt + t, c))
    const = lambda b, t: (0, 0)
    return pl.pallas_call(
        functools.partial(_ssd_kernel, n_chunks=tb // SSD_CHUNK),
        grid=(batch, nt),
        in_specs=[
            pl.BlockSpec((tb, SSD_INNER), idx(L1_Z // SSD_INNER)),
            pl.BlockSpec((tb, SSD_INNER), idx(L1_XBC // SSD_INNER)),
            pl.BlockSpec((tb, bc_w), idx((L1_XBC + SSD_INNER) // bc_w)),
            pl.BlockSpec((tb, bc_w), idx((L1_XBC + SSD_INNER + bc_w) // bc_w)),
            pl.BlockSpec((tb, LANES), idx(0)),
            pl.BlockSpec((SSD_CONV, SSD_CONV_DIM), const),
            pl.BlockSpec((1, SSD_CONV_DIM), const),
            pl.BlockSpec((1, LANES), const),
            pl.BlockSpec((1, LANES), const),
            pl.BlockSpec((1, SSD_INNER), const),
            pl.BlockSpec((1, SSD_INNER), const),
        ],
        out_specs=pl.BlockSpec((tb, SSD_INNER), idx(0)),
        out_shape=jax.ShapeDtypeStruct((n, SSD_INNER), BF16),
        scratch_shapes=[
            pltpu.VMEM((SSD_CHUNK + 8, SSD_CONV_DIM), F32),
            pltpu.VMEM((SSD_CHUNK + 8, SSD_CONV_DIM), F32),
            pltpu.VMEM((SSD_CHUNK, CONV_COLS), F32),
            pltpu.VMEM((SSD_CHUNK, SSD_CONV_DIM), F32),
            pltpu.VMEM((SSD_GROUPS, SSD_STATE, SSD_GROUP_WIDTH), F32),
        ],
        compiler_params=_cparams(("arbitrary", "arbitrary")),
        name="ssd",
    )(h_main, h_main, h_main, h_main, h_small, conv_w, conv_b.reshape(1, -1),
      pad(dt_bias), pad(a_log), jnp.repeat(d_skip.astype(F32), SSD_HEAD_DIM).reshape(1, -1),
      norm_w.reshape(1, -1))


def _pad_cols(w, width=LANES):
    return jnp.zeros((w.shape[0], width), BF16).at[:, :w.shape[1]].set(w.astype(BF16))


def kernel(x, mem, positions, l0_w_in, l0_gla_w_gate, l0_gla_b_gate, l0_gla_norm_w, l0_mem_wk, l0_mem_wv, l0_w_out, l0_ln_w, l0_ln_b, l1_w_in, l1_conv_w, l1_conv_b, l1_dt_bias, l1_a_log, l1_d_skip, l1_ssd_norm_w, l1_mem_wk, l1_mem_wv, l1_w_out, l1_ln_w, l1_ln_b):
    batch, seq, dm = x.shape
    n = batch * seq
    x2 = x.reshape(n, dm)
    mem2 = mem.reshape(batch * MEM_LEN, dm)

    ga0 = L0_DQ + GLA_GATE_RANK
    w0 = l0_w_in.astype(BF16)
    w0_b = w0[:, ga0:]
    w0_small = _pad_cols(w0[:, L0_DQ:ga0])
    cos, sin, x2_bf = _rope_tables(positions, x2)
    col_scale = jnp.ones((1, L0_MAIN), F32).at[0, L0_DQ:L0_DK].set(HEAD_DIM ** -0.5 * LOG2_E)
    h0, h0s = _project(x2_bf, w0, L0_DQ, w0_b, w0_small, rope=(cos, sin, col_scale, L0_DQ, L0_DV))
    o_a = _gla(h0, h0s, l0_gla_w_gate, l0_gla_b_gate, l0_gla_norm_w, batch, seq)
    o_b = _dilated_attention(h0, batch, seq)
    kv0 = _mem_kv(mem2, l0_mem_wk, l0_mem_wv)
    o_m = _mem_attn(h0, kv0, batch, seq, L0_MQ, L0_MG)
    x1, x1_bf = _out_proj_ln([o_a, o_b, o_m], l0_w_out, x2, l0_ln_w, l0_ln_b, emit_bf16=True)

    dt0 = SSD_INNER + SSD_CONV_DIM
    w1 = l1_w_in.astype(BF16)
    w1_b = w1[:, dt0 + SSD_HEADS:]
    w1_small = _pad_cols(w1[:, dt0:dt0 + SSD_HEADS])
    h1, h1s = _project(x1_bf, w1, dt0, w1_b, w1_small)
    y = _ssd(h1, h1s, l1_conv_w, l1_conv_b, l1_dt_bias, l1_a_log, l1_d_skip, l1_ssd_norm_w, batch, seq)
    kv1 = _mem_kv(mem2, l1_mem_wk, l1_mem_wv)
    o_m1 = _mem_attn(h1, kv1, batch, seq, L1_MQ, L1_MG)
    (x2_out,) = _out_proj_ln([y, o_m1], l1_w_out, x1, l1_ln_w, l1_ln_b)
    return x2_out.reshape(batch, seq, dm)
```

```python
import functools

import jax
import jax.numpy as jnp
from jax import lax
from jax.experimental import pallas as pl
from jax.experimental.pallas import tpu as pltpu

F32 = jnp.float32
BF16 = jnp.bfloat16

DEPTH = 2
DEEPNORM_ALPHA = (2 * DEPTH) ** 0.25
LN_EPS = 1e-5
RMS_EPS = 1e-6
ROPE_THETA = 10000.0
MEM_LEN = 256

GLA_HEADS = 4
GLA_DK = 128
GLA_DV = 256
GLA_GATE_RANK = 16
GLA_GATE_NORMALIZER = 16.0
GLA_CHUNK = 64

DIL_PATTERNS = ((128, 1), (512, 4), (2048, 16))
DIL_HEADS = 4
HEAD_DIM = 128
DIL_BLOCK = 128

MEM_HEADS = 4

SSD_HEADS = 24
SSD_HEAD_DIM = 64
SSD_GROUPS = 4
SSD_STATE = 128
SSD_CONV = 4
SSD_CHUNK = 128
SSD_INNER = SSD_HEADS * SSD_HEAD_DIM
SSD_GROUP_WIDTH = SSD_INNER // SSD_GROUPS
SSD_CONV_DIM = SSD_INNER + 2 * SSD_GROUPS * SSD_STATE
CONV_COLS = 256

LANES = 128
LOG2_E = 1.4426950408889634
VMEM_LIMIT = 48 * 1024 * 1024

L0_GQ, L0_GK, L0_GV, L0_GG = 0, 512, 1024, 2048
L0_DQ, L0_DK, L0_DV, L0_DG = 3072, 4608, 5120, 5632
L0_MQ, L0_MG = 6144, 6656
L0_MAIN = 7168
L1_Z, L1_XBC, L1_MQ, L1_MG = 0, 1536, 4096, 4608
L1_MAIN = 5120


def _nt(a, b):
    return lax.dot_general(a, b, (((1,), (1,)), ((), ())), preferred_element_type=F32)


def _tn(a, b):
    return lax.dot_general(a, b, (((0,), (0,)), ((), ())), preferred_element_type=F32)


def _nn(a, b):
    return jnp.dot(a, b, preferred_element_type=F32)


def _cumsum_rows(tril, x):
    hi = x.astype(BF16)
    rest = x - hi.astype(F32)
    mid = rest.astype(BF16)
    lo = (rest - mid.astype(F32)).astype(BF16)
    return _nn(tril, hi) + _nn(tril, mid) + _nn(tril, lo)


def _silu(x):
    return 0.5 * x * (1.0 + jnp.tanh(0.5 * x))


def _cparams(sem):
    return pltpu.CompilerParams(dimension_semantics=sem, vmem_limit_bytes=VMEM_LIMIT)


def _rope_table_kernel(pos_ref, inv_ref, sign_ref, x_ref, cos_ref, sin_ref, xb_ref):
    xb_ref[...] = x_ref[...].astype(BF16)
    rows = pos_ref.shape[0]
    half = HEAD_DIM // 2
    ang = pos_ref[...].astype(F32) * inv_ref[...]
    cos = jnp.cos(ang)
    sin = jnp.sin(ang)
    lo = lax.broadcasted_iota(jnp.int32, (rows, HEAD_DIM), 1) < half
    sign = sign_ref[...]
    for table, ref, scale in ((cos, cos_ref, None), (sin, sin_ref, sign)):
        swapped = pltpu.roll(table, half, 1)
        first = jnp.where(lo, table, swapped)
        second = jnp.where(lo, swapped, table)
        if scale is not None:
            first, second = first * scale, second * scale
        ref[0:rows, :] = first
        ref[rows:2 * rows, :] = second


def _rope_tables(positions, x):
    n, dm = x.shape
    half = HEAD_DIM // 2
    inv = 1.0 / (ROPE_THETA ** (jnp.arange(0, HEAD_DIM, 2, dtype=F32) / HEAD_DIM))
    inv2 = jnp.concatenate([inv, inv]).reshape(1, HEAD_DIM)
    sign = jnp.concatenate([-jnp.ones((half,), F32), jnp.ones((half,), F32)]).reshape(1, HEAD_DIM)
    tb = 2048
    pos_pair = positions.reshape(n // tb, 2, tb // 2).transpose(0, 2, 1)
    pos_pair = jnp.repeat(pos_pair, half, axis=2).reshape(n // 2, HEAD_DIM)
    row = pl.BlockSpec((tb, HEAD_DIM), lambda i: (i, 0))
    vec = pl.BlockSpec((1, HEAD_DIM), lambda i: (0, 0))
    return pl.pallas_call(
        _rope_table_kernel,
        grid=(n // tb,),
        in_specs=[pl.BlockSpec((tb // 2, HEAD_DIM), lambda i: (i, 0)), vec, vec,
                  pl.BlockSpec((tb, dm), lambda i: (i, 0))],
        out_specs=[row, row, pl.BlockSpec((tb, dm), lambda i: (i, 0))],
        out_shape=[jax.ShapeDtypeStruct((n, HEAD_DIM), F32)] * 2 + [jax.ShapeDtypeStruct((n, dm), BF16)],
        compiler_params=_cparams(("arbitrary",)),
        name="rope_tables",
    )(pos_pair, inv2, sign, x)


SUB_COLS = 256


def _proj_kernel(*refs, groups, cast_x, has_rope):
    it = iter(refs)
    x_ref, wa_ref, wb_ref, ws_ref = next(it), next(it), next(it), next(it)
    if has_rope:
        c_ref, s_ref, sc_ref = next(it), next(it), next(it)
    o_ref, os_ref = next(it), next(it)
    xb_ref = next(it) if cast_x else x_ref
    j = pl.program_id(1)

    @pl.when(j == 0)
    def _():
        if cast_x:
            xb_ref[...] = x_ref[...].astype(BF16)
        os_ref[...] = _nn(xb_ref[...], ws_ref[...])

    def rope_block(acc, cols):
        cos, sin = c_ref[...], s_ref[...]
        for k in range(SUB_COLS // HEAD_DIM):
            sl = slice(cols.start + k * HEAD_DIM, cols.start + (k + 1) * HEAD_DIM)
            a = acc[:, k * HEAD_DIM:(k + 1) * HEAD_DIM] * sc_ref[:, sl]
            o_ref[:, sl] = (a * cos + pltpu.roll(a, HEAD_DIM // 2, 1) * sin).astype(o_ref.dtype)

    def emit(w_ref, kinds):
        if kinds is None:
            o_ref[...] = _nn(xb_ref[...], w_ref[...]).astype(o_ref.dtype)
            return
        for sb, kind in enumerate(kinds):
            cols = slice(sb * SUB_COLS, (sb + 1) * SUB_COLS)
            acc = _nn(xb_ref[...], w_ref[:, cols])
            if kind == "rope":
                rope_block(acc, cols)
            else:
                o_ref[:, cols] = acc.astype(o_ref.dtype)

    for lo, hi, src, kinds in groups:
        pl.when(jnp.logical_and(j >= lo, j < hi))(functools.partial(emit, wb_ref if src else wa_ref, kinds))


def _project(x, w_a, a_cols, w_b, w_small, rope=None, tm=2048, tn=1024):
    n, kdim = x.shape
    m = a_cols + w_b.shape[1]
    split = a_cols // tn
    n_tiles = m // tn
    cast_x = x.dtype != BF16
    per_tile = tn // SUB_COLS
    kinds = [["plain"] * per_tile for _ in range(n_tiles)]
    in_specs = [
        pl.BlockSpec((tm, kdim), lambda i, j: (i, 0)),
        pl.BlockSpec((kdim, tn), lambda i, j: (0, jnp.minimum(j, split - 1))),
        pl.BlockSpec((kdim, tn), lambda i, j: (0, jnp.maximum(j - split, 0))),
        pl.BlockSpec((kdim, LANES), lambda i, j: (0, 0)),
    ]
    args = [x, w_a, w_b, w_small]
    scratch = [pltpu.VMEM((tm, kdim), BF16)] if cast_x else []
    if rope is not None:
        cos, sin, col_scale, first_col, last_col = rope
        for c in range(first_col // SUB_COLS, last_col // SUB_COLS):
            kinds[c // per_tile][c % per_tile] = "rope"
        in_specs += [
            pl.BlockSpec((tm, HEAD_DIM), lambda i, j: (i, 0)),
            pl.BlockSpec((tm, HEAD_DIM), lambda i, j: (i, 0)),
            pl.BlockSpec((1, tn), lambda i, j: (0, j)),
        ]
        args += [cos, sin, col_scale]
    groups = []
    for t in range(n_tiles):
        plan = None if all(k == "plain" for k in kinds[t]) else tuple(kinds[t])
        src = int(t >= split)
        if groups and groups[-1][2] == src and groups[-1][3] == plan:
            groups[-1] = (groups[-1][0], t + 1, src, plan)
        else:
            groups.append((t, t + 1, src, plan))
    return pl.pallas_call(
        functools.partial(_proj_kernel, groups=tuple(groups), cast_x=cast_x, has_rope=rope is not None),
        grid=(n // tm, n_tiles),
        in_specs=in_specs,
        out_specs=[
            pl.BlockSpec((tm, tn), lambda i, j: (i, j)),
            pl.BlockSpec((tm, LANES), lambda i, j: (i, 0)),
        ],
        out_shape=[jax.ShapeDtypeStruct((n, m), BF16), jax.ShapeDtypeStruct((n, LANES), F32)],
        scratch_shapes=scratch,
        compiler_params=_cparams(("arbitrary", "arbitrary")),
        name="in_proj",
    )(*args)


GLA_BATCH_GROUP = 2


def _gla_kernel(q_ref, k_ref, v_ref, g_ref, ga_ref, wg_ref, bg_ref, nw_ref, o_ref, st_ref, la_ref, *, n_chunks):
    c_ = GLA_CHUNK
    group = q_ref.shape[0]

    @pl.when(pl.program_id(1) == 0)
    def _():
        st_ref[...] = jnp.zeros_like(st_ref)

    p_ = 2 * c_
    row = lax.broadcasted_iota(jnp.int32, (p_, p_), 0)
    col = lax.broadcasted_iota(jnp.int32, (p_, p_), 1)
    same_chunk = (row < c_) == (col < c_)
    tril = jnp.where(jnp.logical_and(same_chunk, col <= row), 1.0, 0.0).astype(BF16)
    row_a = lax.broadcasted_iota(jnp.int32, (c_, c_), 0)
    col_a = lax.broadcasted_iota(jnp.int32, (c_, c_), 1)
    causal_a = col_a <= row_a
    row_b = lax.broadcasted_iota(jnp.int32, (c_, p_), 0)
    col_b = lax.broadcasted_iota(jnp.int32, (c_, p_), 1)
    causal_b = col_b - c_ <= row_b
    in_a = lax.broadcasted_iota(jnp.int32, (p_, 1), 0) < c_
    nw = nw_ref[...]
    for b in range(group):
        z = _nn(ga_ref[b].astype(BF16), wg_ref[...]) + bg_ref[...]
        la_ref[b] = (jnp.minimum(z, 0.0) - jnp.log1p(jnp.exp(-jnp.abs(z)))) * (1.0 / GLA_GATE_NORMALIZER)

    def pair(pi, carry):
        r0 = pl.multiple_of(pi * p_, p_)
        rows = pl.ds(r0, p_)
        gates = []
        for b in range(group):
            bcum = _cumsum_rows(tril, la_ref[b, rows, :])
            blast_a = bcum[c_ - 1:c_, :]
            blast_b = bcum[p_ - 1:p_, :]
            blast = jnp.where(in_a, blast_a, blast_b)
            gates.append((jnp.exp(bcum), jnp.exp(-bcum), jnp.exp(blast - bcum), jnp.exp(blast_a), jnp.exp(blast_b)))
        for h in range(GLA_HEADS):
            for b in range(group):
                e_pos, e_neg, e_tail, d_a, d_b = gates[b]
                ks = slice(h * GLA_DK, (h + 1) * GLA_DK)
                vs = slice(h * GLA_DV, (h + 1) * GLA_DV)
                q = q_ref[b, rows, ks].astype(F32) * (GLA_DK ** -0.5)
                k = k_ref[b, rows, ks].astype(F32)
                q_dec = q * e_pos[:, ks]
                k_tail = k * e_tail[:, ks]
                q_dec_bf = q_dec.astype(BF16)
                k_inv_bf = (k * e_neg[:, ks]).astype(BF16)
                k_tail_bf = k_tail.astype(BF16)
                v = v_ref[b, rows, vs]
                s_a = jnp.where(causal_a, _nt(q_dec_bf[0:c_], k_inv_bf[0:c_]), 0.0)
                keys_b = jnp.concatenate([k_tail_bf[0:c_], k_inv_bf[c_:p_]], axis=0)
                s_b = jnp.where(causal_b, _nt(q_dec_bf[c_:p_], keys_b), 0.0)
                o_intra = jnp.concatenate([_nn(s_a.astype(BF16), v[0:c_]), _nn(s_b.astype(BF16), v)], axis=0)
                q_state = jnp.concatenate([q_dec_bf[0:c_], (q_dec[c_:p_] * d_a[:, ks]).astype(BF16)], axis=0)
                st = st_ref[b, h]
                o = o_intra + _nt(q_state, st.astype(BF16))
                k_end = jnp.concatenate([(k_tail[0:c_] * d_b[:, ks]).astype(BF16), k_tail_bf[c_:p_]], axis=0)
                st_ref[b, h] = st * (d_a[:, ks] * d_b[:, ks]) + _tn(v, k_end)
                ms = jnp.mean(o * o, axis=-1, keepdims=True)
                o = o * lax.rsqrt(ms + RMS_EPS) * nw
                o_ref[b, rows, vs] = (o * _silu(g_ref[b, rows, vs].astype(F32))).astype(o_ref.dtype)
        return carry

    lax.fori_loop(0, n_chunks // 2, pair, 0)


def _gla(h_main, h_small, w_gate, b_gate, norm_w, batch, seq, tb=512):
    n = batch * seq
    nt = seq // tb
    group = GLA_BATCH_GROUP if batch % GLA_BATCH_GROUP == 0 else 1
    wg = jnp.zeros((LANES, GLA_HEADS * GLA_DK), BF16).at[:GLA_GATE_RANK].set(w_gate.astype(BF16))
    h4 = h_main.reshape(batch // group, group, seq, h_main.shape[1])
    s4 = h_small.reshape(batch // group, group, seq, LANES)
    idx = lambda c: (lambda b, t: (b, 0, t, c))
    const = lambda b, t: (0, 0)
    width = GLA_HEADS * GLA_DV
    out = pl.pallas_call(
        functools.partial(_gla_kernel, n_chunks=tb // GLA_CHUNK),
        grid=(batch // group, nt),
        in_specs=[
            pl.BlockSpec((None, group, tb, 512), idx(L0_GQ // 512)),
            pl.BlockSpec((None, group, tb, 512), idx(L0_GK // 512)),
            pl.BlockSpec((None, group, tb, 1024), idx(L0_GV // 1024)),
            pl.BlockSpec((None, group, tb, 1024), idx(L0_GG // 1024)),
            pl.BlockSpec((None, group, tb, LANES), idx(0)),
            pl.BlockSpec((LANES, 512), const),
            pl.BlockSpec((1, 512), const),
            pl.BlockSpec((1, GLA_DV), const),
        ],
        out_specs=pl.BlockSpec((None, group, tb, width), idx(0)),
        out_shape=jax.ShapeDtypeStruct((batch // group, group, seq, width), BF16),
        scratch_shapes=[pltpu.VMEM((group, GLA_HEADS, GLA_DV, GLA_DK), F32),
                        pltpu.VMEM((group, tb, GLA_HEADS * GLA_DK), F32)],
        compiler_params=_cparams(("arbitrary", "arbitrary")),
        name="gla",
    )(h4, h4, h4, h4, s4, wg, b_gate.reshape(1, -1), norm_w.reshape(1, -1))
    return out.reshape(n, width)


DIL_WINDOW = 2048
SLAB_GROUP = 16
SLAB_PITCH = 24
SEG = 32
TILE_UNROLL = 8


def _dil_kernel(q0_ref, q1_ref, q2_ref, kc_ref, kp_ref, vc_ref, vp_ref, g_ref, out_ref,
                kcat, vcat, ks, vs, q1s, q2s, o0s, l0s, o1s, l1s, o2s, l2s, bias):
    blk = DIL_BLOCK
    w = DIL_WINDOW
    groups = w // SLAB_GROUP
    t = pl.program_id(1)
    first_window = t == 0

    @pl.when(jnp.logical_and(jnp.logical_and(pl.program_id(0) == 0, t == 0), pl.program_id(2) == 0))
    def _():
        rho = lax.broadcasted_iota(jnp.int32, (blk, 2 * blk), 0)
        kap = lax.broadcasted_iota(jnp.int32, (blk, 2 * blk), 1)
        in_prev = kap < blk
        kap_l = jnp.where(in_prev, kap, kap - blk)
        perm = lambda x: 4 * (x % SEG) + x // SEG
        neg = jnp.float32(-jnp.inf)
        for slot, (jq, jk) in enumerate(((rho, kap_l), (perm(rho), perm(kap_l)))):
            ok_prev = jnp.where(jk >= jq, 0.0, neg)
            ok_cur = jnp.where(jk <= jq, 0.0, neg)
            bias[2 * slot] = jnp.where(in_prev, ok_prev, ok_cur)
            bias[2 * slot + 1] = jnp.where(in_prev, neg, ok_cur)

    kcat[0:blk, :] = kp_ref[w - blk:w, :]
    kcat[blk:blk + w, :] = kc_ref[...]
    vcat[0:blk, :] = vp_ref[w - blk:w, :]
    vcat[blk:blk + w, :] = vc_ref[...]

    def to_slabs(i, carry):
        src = pl.ds(pl.multiple_of(i * SLAB_GROUP, SLAB_GROUP), SLAB_GROUP)
        prev = pl.ds(pl.multiple_of(i * SLAB_PITCH, 8), SLAB_GROUP)
        cur = pl.ds(pl.multiple_of((groups + i) * SLAB_PITCH, 8), SLAB_GROUP)
        ks[prev, :] = kp_ref[src, :].astype(F32)
        ks[cur, :] = kc_ref[src, :].astype(F32)
        vs[prev, :] = vp_ref[src, :].astype(F32)
        vs[cur, :] = vc_ref[src, :].astype(F32)
        q1s[prev, :] = q1_ref[src, :].astype(F32)
        q2s[prev, :] = q2_ref[src, :].astype(F32)
        return carry

    lax.fori_loop(0, groups, to_slabs, 0, unroll=4)

    def attend(q, k, v, b):
        s = _nt(q, k) + b
        m = jnp.max(s, -1, keepdims=True)
        p = jnp.exp2(s - m)
        den = jnp.sum(p, -1, keepdims=True)
        o = _nn(p.astype(BF16), v) * (1.0 / den)
        return o, jnp.broadcast_to(m + jnp.log2(den), (blk, LANES))

    def group0(i, carry):
        r0 = pl.multiple_of(i * blk, blk)
        sel = jnp.where(jnp.logical_and(first_window, i == 0), 1, 0)
        o, l = attend(q0_ref[pl.ds(r0, blk), :], kcat[pl.ds(r0, 2 * blk), :], vcat[pl.ds(r0, 2 * blk), :],
                      bias[sel])
        o0s[pl.ds(r0, blk), :] = o
        l0s[pl.ds(r0, blk), :] = l
        return carry


    def seg(ref, group, b):
        return ref[pl.ds(group * SLAB_PITCH + b, SEG, stride=SLAB_PITCH), :]

    def group1(idx, carry):
        r = idx // 4
        n = idx % 4
        g_cur = groups + SEG * n
        q = jnp.concatenate([seg(q1s, SEG * n, r + 4 * j) for j in range(4)], axis=0).astype(BF16)
        k = jnp.concatenate([seg(ks, g_cur - SEG, r + 4 * j) for j in range(4)]
                            + [seg(ks, g_cur, r + 4 * j) for j in range(4)], axis=0).astype(BF16)
        v = jnp.concatenate([seg(vs, g_cur - SEG, r + 4 * j) for j in range(4)]
                            + [seg(vs, g_cur, r + 4 * j) for j in range(4)], axis=0).astype(BF16)
        sel = jnp.where(jnp.logical_and(first_window, n == 0), 3, 2)
        o, l = attend(q, k, v, bias[sel])
        for j in range(4):
            dst = pl.ds(SEG * n * SLAB_PITCH + r + 4 * j, SEG, stride=SLAB_PITCH)
            o1s[dst, :] = o[j * SEG:(j + 1) * SEG]
            l1s[dst, :] = l[j * SEG:(j + 1) * SEG]
        return carry


    def group2(r, carry):
        prev = pl.ds(r, blk, stride=SLAB_PITCH)
        cur = pl.ds(groups * SLAB_PITCH + r, blk, stride=SLAB_PITCH)
        q = q2s[prev, :].astype(BF16)
        k = jnp.concatenate([ks[prev, :], ks[cur, :]], axis=0).astype(BF16)
        v = jnp.concatenate([vs[prev, :], vs[cur, :]], axis=0).astype(BF16)
        o, l = attend(q, k, v, bias[jnp.where(first_window, 1, 0)])
        o2s[prev, :] = o
        l2s[prev, :] = l
        return carry

    def tiles(i, carry):
        return group2(i, group1(i, group0(i, carry)))

    lax.fori_loop(0, w // blk, tiles, 0, unroll=TILE_UNROLL)

    def merge(i, carry):
        r0 = pl.multiple_of(i * blk, blk)
        rows = pl.ds(r0, blk)
        pieces = [pl.ds(pl.multiple_of((i * (blk // SLAB_GROUP) + j) * SLAB_PITCH, 8), SLAB_GROUP)
                  for j in range(blk // SLAB_GROUP)]
        gather = lambda ref: jnp.concatenate([ref[p, :] for p in pieces], axis=0)
        l0, l1, l2 = l0s[rows, :], gather(l1s), gather(l2s)
        m = jnp.maximum(jnp.maximum(l0, l1), l2)
        w0, w1, w2 = jnp.exp2(l0 - m), jnp.exp2(l1 - m), jnp.exp2(l2 - m)
        o = (w0 * o0s[rows, :] + w1 * gather(o1s) + w2 * gather(o2s)) * (1.0 / (w0 + w1 + w2))
        out_ref[rows, :] = (o * _silu(g_ref[rows, :].astype(F32))).astype(out_ref.dtype)
        return carry

    lax.fori_loop(0, w // blk, merge, 0)


def _dilated_attention(h_main, batch, seq):
    n = batch * seq
    w = DIL_WINDOW
    nt = seq // w
    slab_rows = (w // SLAB_GROUP) * SLAB_PITCH

    def cur(col):
        return pl.BlockSpec((w, HEAD_DIM), lambda b, t, h: (b * nt + t, col // HEAD_DIM + h))

    def prev(col):
        return pl.BlockSpec((w, HEAD_DIM), lambda b, t, h: (b * nt + jnp.maximum(t - 1, 0), col // HEAD_DIM + h))

    width = DIL_HEADS * HEAD_DIM
    slab = lambda rows: pltpu.VMEM((rows, LANES), F32)
    return pl.pallas_call(
        _dil_kernel,
        grid=(batch, nt, DIL_HEADS),
        in_specs=[cur(L0_DQ), cur(L0_DQ + width), cur(L0_DQ + 2 * width),
                  cur(L0_DK), prev(L0_DK), cur(L0_DV), prev(L0_DV), cur(L0_DG)],
        out_specs=pl.BlockSpec((w, HEAD_DIM), lambda b, t, h: (b * nt + t, h)),
        out_shape=jax.ShapeDtypeStruct((n, width), BF16),
        scratch_shapes=[
            pltpu.VMEM((w + DIL_BLOCK, HEAD_DIM), BF16), pltpu.VMEM((w + DIL_BLOCK, HEAD_DIM), BF16),
            slab(2 * slab_rows), slab(2 * slab_rows), slab(slab_rows), slab(slab_rows),
            slab(w), slab(w), slab(slab_rows), slab(slab_rows), slab(slab_rows), slab(slab_rows),
            pltpu.VMEM((4, DIL_BLOCK, 2 * DIL_BLOCK), F32),
        ],
        compiler_params=_cparams(("arbitrary", "arbitrary", "arbitrary")),
        name="dilated_attn",
    )(*([h_main] * 8))


def _mem_kv_kernel(m_ref, w_ref, o_ref):
    o_ref[...] = _nn(m_ref[...].astype(BF16), w_ref[...]).astype(o_ref.dtype)


def _mem_kv(mem2d, wk, wv):
    rows, kdim = mem2d.shape
    w = jnp.concatenate([wk, wv], axis=1).astype(BF16)
    tm = MEM_LEN
    return pl.pallas_call(
        _mem_kv_kernel,
        grid=(rows // tm,),
        in_specs=[pl.BlockSpec((tm, kdim), lambda i: (i, 0)), pl.BlockSpec(w.shape, lambda i: (0, 0))],
        out_specs=pl.BlockSpec((tm, w.shape[1]), lambda i: (i, 0)),
        out_shape=jax.ShapeDtypeStruct((rows, w.shape[1]), BF16),
        compiler_params=_cparams(("arbitrary",)),
        name="mem_kv",
    )(mem2d, w)


def _mem_attn_kernel(q_ref, g_ref, k_ref, v_ref, o_ref, *, tq):
    for h in range(MEM_HEADS):
        hs = slice(h * HEAD_DIM, (h + 1) * HEAD_DIM)
        s = _nt(q_ref[:, hs], k_ref[:, hs]) * (HEAD_DIM ** -0.5 * LOG2_E)
        m = jnp.max(s, -1, keepdims=True)
        p = jnp.exp2(s - m)
        den = jnp.sum(p, -1, keepdims=True)
        o = _nn(p.astype(BF16), v_ref[:, hs]) * (1.0 / den)
        o_ref[:, hs] = (o * _silu(g_ref[:, hs].astype(F32))).astype(o_ref.dtype)


def _mem_attn(h_main, kv, batch, seq, q_col, g_col, tq=1024):
    n = batch * seq
    width = MEM_HEADS * HEAD_DIM
    nt = seq // tq
    return pl.pallas_call(
        functools.partial(_mem_attn_kernel, tq=tq),
        grid=(batch, nt),
        in_specs=[
            pl.BlockSpec((tq, width), lambda b, t: (b * nt + t, q_col // width)),
            pl.BlockSpec((tq, width), lambda b, t: (b * nt + t, g_col // width)),
            pl.BlockSpec((MEM_LEN, width), lambda b, t: (b, 0)),
            pl.BlockSpec((MEM_LEN, width), lambda b, t: (b, 1)),
        ],
        out_specs=pl.BlockSpec((tq, width), lambda b, t: (b * nt + t, 0)),
        out_shape=jax.ShapeDtypeStruct((n, width), BF16),
        compiler_params=_cparams(("arbitrary", "arbitrary")),
        name="mem_attn",
    )(h_main, h_main, kv, kv)


OUT_SUB_ROWS = 256


def _out_kernel(*refs, n_parts, emit_bf16):
    a_refs = refs[:n_parts]
    w_refs = refs[n_parts:2 * n_parts]
    x_ref, lw_ref, lb_ref, o_ref = refs[2 * n_parts:2 * n_parts + 4]
    for r0 in range(0, x_ref.shape[0], OUT_SUB_ROWS):
        rows = slice(r0, r0 + OUT_SUB_ROWS)
        acc = DEEPNORM_ALPHA * x_ref[rows, :]
        for a_ref, w_ref in zip(a_refs, w_refs):
            acc = acc + _nn(a_ref[rows, :], w_ref[...])
        mu = jnp.mean(acc, -1, keepdims=True)
        d = acc - mu
        var = jnp.mean(d * d, -1, keepdims=True)
        y = d * lax.rsqrt(var + LN_EPS) * lw_ref[...] + lb_ref[...]
        o_ref[rows, :] = y
        if emit_bf16:
            refs[2 * n_parts + 4][rows, :] = y.astype(BF16)


def _out_proj_ln(parts, w_out, x, ln_w, ln_b, emit_bf16=False, tm=1024):
    n, dm = x.shape
    w_bf = w_out.astype(BF16)
    w_specs, off = [], 0
    for p in parts:
        width = p.shape[1]
        w_specs.append(pl.BlockSpec((width, dm), functools.partial(lambda i, blk: (blk, 0), blk=off // width)))
        assert off % width == 0
        off += width
    row = lambda width: pl.BlockSpec((tm, width), lambda i: (i, 0))
    const = lambda shape: pl.BlockSpec(shape, lambda i: (0, 0))
    out_specs = [row(dm)] + ([row(dm)] if emit_bf16 else [])
    out_shape = [jax.ShapeDtypeStruct((n, dm), F32)] + ([jax.ShapeDtypeStruct((n, dm), BF16)] if emit_bf16 else [])
    return pl.pallas_call(
        functools.partial(_out_kernel, n_parts=len(parts), emit_bf16=emit_bf16),
        grid=(n // tm,),
        in_specs=[row(p.shape[1]) for p in parts] + w_specs + [row(dm), const((1, dm)), const((1, dm))],
        out_specs=out_specs,
        out_shape=out_shape,
        compiler_params=_cparams(("arbitrary",)),
        name="out_proj_ln",
    )(*parts, *([w_bf] * len(parts)), x, ln_w.reshape(1, dm), ln_b.reshape(1, dm))


def _ssd_kernel(z_ref, xs_ref, bm_ref, cm_ref, dt_ref, cw_ref, cb_ref, dtb_ref, a_ref, dsk_ref, nw_ref,
                o_ref, u_ref, pair_ref, lag_ref, act_ref, st_ref, *, n_chunks):
    q_ = SSD_CHUNK
    halo = 8

    @pl.when(pl.program_id(1) == 0)
    def _():
        st_ref[...] = jnp.zeros_like(st_ref)
        u_ref[0:halo, :] = jnp.zeros((halo, SSD_CONV_DIM), F32)
        pair_ref[0:halo, :] = jnp.zeros((halo, SSD_CONV_DIM), F32)

    row = lax.broadcasted_iota(jnp.int32, (q_, q_), 0)
    col = lax.broadcasted_iota(jnp.int32, (q_, q_), 1)
    causal = col <= row
    tril = jnp.where(causal, 1.0, 0.0).astype(BF16)
    lane = lax.broadcasted_iota(jnp.int32, (q_, LANES), 1)
    lo_half = lane < SSD_HEAD_DIM
    lane_row = lax.broadcasted_iota(jnp.int32, (1, LANES), 1)
    lo_half_row = lane_row < SSD_HEAD_DIM
    lo_half_f = lo_half.astype(F32)
    a_log2 = -jnp.exp(a_ref[...]) * LOG2_E

    def row_bcast(t, r):
        return jnp.tile(jnp.broadcast_to(t[r:r + 1, :], (8, q_)), (q_ // 8, 1))

    def chunk(ci, carry):
        r0 = pl.multiple_of(ci * q_, q_)
        rows = pl.ds(r0, q_)
        bc_w = SSD_GROUPS * SSD_STATE
        u_ref[halo:halo + q_, 0:SSD_INNER] = xs_ref[rows, :].astype(F32)
        u_ref[halo:halo + q_, SSD_INNER:SSD_INNER + bc_w] = bm_ref[rows, :].astype(F32)
        u_ref[halo:halo + q_, SSD_INNER + bc_w:SSD_CONV_DIM] = cm_ref[rows, :].astype(F32)
        for c0 in range(0, SSD_CONV_DIM, CONV_COLS):
            cs = slice(c0, c0 + CONV_COLS)
            u_cur = u_ref[halo:halo + q_, cs]
            lag_ref[...] = u_ref[halo - 1:halo - 1 + q_, cs]
            u_lag = lag_ref[...]
            pair_ref[halo:halo + q_, cs] = cw_ref[1:2, cs] * u_cur + cw_ref[0:1, cs] * u_lag
            conv = ((cb_ref[:, cs] + cw_ref[3:4, cs] * u_cur) + cw_ref[2:3, cs] * u_lag
                    + pair_ref[halo - 2:halo - 2 + q_, cs])
            act_ref[:, cs] = _silu(conv)
        u_ref[0:halo, :] = u_ref[q_:q_ + halo, :]
        pair_ref[0:halo, :] = pair_ref[q_:q_ + halo, :]
        dt_in = dt_ref[rows, :] + dtb_ref[...]
        dt = jnp.maximum(dt_in, 0.0) + jnp.log1p(jnp.exp(-jnp.abs(dt_in)))
        acum = _cumsum_rows(tril, dt * a_log2)
        acum_t = acum.T
        dt_t = dt.T
        ldt_t = jnp.log2(dt_t)
        rowp_t = acum_t - ldt_t
        w_t = dt_t * jnp.exp2(acum_t[:, q_ - 1:q_] - acum_t)
        c_decay = jnp.exp2(acum[q_ - 1:q_, :])
        for g in range(SSD_GROUPS):
            b_f = act_ref[:, SSD_INNER + g * SSD_STATE:SSD_INNER + (g + 1) * SSD_STATE]
            c_f = act_ref[:, SSD_INNER + (SSD_GROUPS + g) * SSD_STATE:SSD_INNER + (SSD_GROUPS + g + 1) * SSD_STATE]
            c_bf = c_f.astype(BF16)
            cbm = jnp.where(causal, _nt(c_bf, b_f.astype(BF16)), 0.0)
            b_t = b_f.T
            st = st_ref[g]
            y_off_all = _nn(c_bf, st.astype(BF16))
            ys, news, decs = [], [], []
            ss = jnp.zeros((q_, 1), F32)
            for j in range(SSD_GROUP_WIDTH // LANES):
                h1 = (g * SSD_GROUP_WIDTH + j * LANES) // SSD_HEAD_DIM
                ps = slice(g * SSD_GROUP_WIDTH + j * LANES, g * SSD_GROUP_WIDTH + (j + 1) * LANES)
                xs = act_ref[:, ps]
                intra, inter, e_l = [], [], []
                for hh in (h1, h1 + 1):
                    colb = jnp.broadcast_to(acum[:, hh:hh + 1], (q_, q_))
                    decay = jnp.exp2(jnp.minimum(colb - row_bcast(rowp_t, hh), row_bcast(ldt_t, hh)))
                    intra.append((cbm * decay).astype(BF16))
                    inter.append((b_t * row_bcast(w_t, hh)).astype(BF16))
                    e_l.append(jnp.exp2(colb))
                lhs = jnp.concatenate([jnp.concatenate(intra, axis=1), jnp.concatenate(inter, axis=1)], axis=0)
                xs_lo = xs * lo_half_f
                rhs = jnp.concatenate([xs_lo, xs - xs_lo], axis=0).astype(BF16)
                prod = _nn(lhs, rhs)
                y = prod[0:q_] + y_off_all[:, j * LANES:(j + 1) * LANES] * jnp.where(lo_half, e_l[0], e_l[1])
                y = y + dsk_ref[:, ps] * xs
                y = y * _silu(z_ref[rows, ps].astype(F32))
                ss = ss + jnp.sum(y * y, -1, keepdims=True)
                ys.append(y)
                news.append(prod[q_:2 * q_])
                decs.append(jnp.where(lo_half_row, c_decay[:, h1:h1 + 1], c_decay[:, h1 + 1:h1 + 2]))
            st_ref[g] = st * jnp.concatenate(decs, axis=1) + jnp.concatenate(news, axis=1)
            inv = lax.rsqrt(ss * (1.0 / SSD_GROUP_WIDTH) + RMS_EPS)
            for j, y in enumerate(ys):
                ps = slice(g * SSD_GROUP_WIDTH + j * LANES, g * SSD_GROUP_WIDTH + (j + 1) * LANES)
                o_ref[rows, ps] = (y * inv * nw_ref[:, ps]).astype(o_ref.dtype)
        return carry

    lax.fori_loop(0, n_chunks, chunk, 0)


def _ssd(h_main, h_small, conv_w, conv_b, dt_bias, a_log, d_skip, norm_w, batch, seq, tb=512):
    n = batch * seq
    nt = seq // tb
    bc_w = SSD_GROUPS * SSD_STATE
    pad = lambda v: jnp.zeros((1, LANES), F32).at[0, :SSD_HEADS].set(v.astype(F32))
    idx = lambda c: (lambda b, t: (b * nt + t, c))
    const = lambda b, t: (0, 0)
    return pl.pallas_call(
        functools.partial(_ssd_kernel, n_chunks=tb // SSD_CHUNK),
        grid=(batch, nt),
        in_specs=[
            pl.BlockSpec((tb, SSD_INNER), idx(L1_Z // SSD_INNER)),
            pl.BlockSpec((tb, SSD_INNER), idx(L1_XBC // SSD_INNER)),
            pl.BlockSpec((tb, bc_w), idx((L1_XBC + SSD_INNER) // bc_w)),
            pl.BlockSpec((tb, bc_w), idx((L1_XBC + SSD_INNER + bc_w) // bc_w)),
            pl.BlockSpec((tb, LANES), idx(0)),
            pl.BlockSpec((SSD_CONV, SSD_CONV_DIM), const),
            pl.BlockSpec((1, SSD_CONV_DIM), const),
            pl.BlockSpec((1, LANES), const),
            pl.BlockSpec((1, LANES), const),
            pl.BlockSpec((1, SSD_INNER), const),
            pl.BlockSpec((1, SSD_INNER), const),
        ],
        out_specs=pl.BlockSpec((tb, SSD_INNER), idx(0)),
        out_shape=jax.ShapeDtypeStruct((n, SSD_INNER), BF16),
        scratch_shapes=[
            pltpu.VMEM((SSD_CHUNK + 8, SSD_CONV_DIM), F32),
            pltpu.VMEM((SSD_CHUNK + 8, SSD_CONV_DIM), F32),
            pltpu.VMEM((SSD_CHUNK, CONV_COLS), F32),
            pltpu.VMEM((SSD_CHUNK, SSD_CONV_DIM), F32),
            pltpu.VMEM((SSD_GROUPS, SSD_STATE, SSD_GROUP_WIDTH), F32),
        ],
        compiler_params=_cparams(("arbitrary", "arbitrary")),
        name="ssd",
    )(h_main, h_main, h_main, h_main, h_small, conv_w, conv_b.reshape(1, -1),
      pad(dt_bias), pad(a_log), jnp.repeat(d_skip.astype(F32), SSD_HEAD_DIM).reshape(1, -1),
      norm_w.reshape(1, -1))


def _pad_cols(w, width=LANES):
    return jnp.zeros((w.shape[0], width), BF16).at[:, :w.shape[1]].set(w.astype(BF16))


def kernel(x, mem, positions, l0_w_in, l0_gla_w_gate, l0_gla_b_gate, l0_gla_norm_w, l0_mem_wk, l0_mem_wv, l0_w_out, l0_ln_w, l0_ln_b, l1_w_in, l1_conv_w, l1_conv_b, l1_dt_bias, l1_a_log, l1_d_skip, l1_ssd_norm_w, l1_mem_wk, l1_mem_wv, l1_w_out, l1_ln_w, l1_ln_b):
    batch, seq, dm = x.shape
    n = batch * seq
    x2 = x.reshape(n, dm)
    mem2 = mem.reshape(batch * MEM_LEN, dm)

    ga0 = L0_DQ + GLA_GATE_RANK
    w0 = l0_w_in.astype(BF16)
    w0_b = w0[:, ga0:]
    w0_small = _pad_cols(w0[:, L0_DQ:ga0])
    cos, sin, x2_bf = _rope_tables(positions, x2)
    col_scale = jnp.ones((1, L0_MAIN), F32).at[0, L0_DQ:L0_DK].set(HEAD_DIM ** -0.5 * LOG2_E)
    h0, h0s = _project(x2_bf, w0, L0_DQ, w0_b, w0_small, rope=(cos, sin, col_scale, L0_DQ, L0_DV))
    o_a = _gla(h0, h0s, l0_gla_w_gate, l0_gla_b_gate, l0_gla_norm_w, batch, seq)
    o_b = _dilated_attention(h0, batch, seq)
    kv0 = _mem_kv(mem2, l0_mem_wk, l0_mem_wv)
    o_m = _mem_attn(h0, kv0, batch, seq, L0_MQ, L0_MG)
    x1, x1_bf = _out_proj_ln([o_a, o_b, o_m], l0_w_out, x2, l0_ln_w, l0_ln_b, emit_bf16=True)

    dt0 = SSD_INNER + SSD_CONV_DIM
    w1 = l1_w_in.astype(BF16)
    w1_b = w1[:, dt0 + SSD_HEADS:]
    w1_small = _pad_cols(w1[:, dt0:dt0 + SSD_HEADS])
    h1, h1s = _project(x1_bf, w1, dt0, w1_b, w1_small)
    y = _ssd(h1, h1s, l1_conv_w, l1_conv_b, l1_dt_bias, l1_a_log, l1_d_skip, l1_ssd_norm_w, batch, seq)
    kv1 = _mem_kv(mem2, l1_mem_wk, l1_mem_wv)
    o_m1 = _mem_attn(h1, kv1, batch, seq, L1_MQ, L1_MG)
    (x2_out,) = _out_proj_ln([y, o_m1], l1_w_out, x1, l1_ln_w, l1_ln_b)
    return x2_out.reshape(batch, seq, dm)
```

```python
import functools

import jax
import jax.numpy as jnp
from jax import lax
from jax.experimental import pallas as pl
from jax.experimental.pallas import tpu as pltpu

F32 = jnp.float32
BF16 = jnp.bfloat16

DEPTH = 2
DEEPNORM_ALPHA = (2 * DEPTH) ** 0.25
LN_EPS = 1e-5
RMS_EPS = 1e-6
ROPE_THETA = 10000.0
MEM_LEN = 256

GLA_HEADS = 4
GLA_DK = 128
GLA_DV = 256
GLA_GATE_RANK = 16
GLA_GATE_NORMALIZER = 16.0
GLA_CHUNK = 64

DIL_PATTERNS = ((128, 1), (512, 4), (2048, 16))
DIL_HEADS = 4
HEAD_DIM = 128
DIL_BLOCK = 128

MEM_HEADS = 4

SSD_HEADS = 24
SSD_HEAD_DIM = 64
SSD_GROUPS = 4
SSD_STATE = 128
SSD_CONV = 4
SSD_CHUNK = 128
SSD_INNER = SSD_HEADS * SSD_HEAD_DIM
SSD_GROUP_WIDTH = SSD_INNER // SSD_GROUPS
SSD_CONV_DIM = SSD_INNER + 2 * SSD_GROUPS * SSD_STATE
CONV_COLS = 256

LANES = 128
LOG2_E = 1.4426950408889634
VMEM_LIMIT = 48 * 1024 * 1024

L0_GQ, L0_GK, L0_GV, L0_GG = 0, 512, 1024, 2048
L0_DQ, L0_DK, L0_DV, L0_DG = 3072, 4608, 5120, 5632
L0_MQ, L0_MG = 6144, 6656
L0_MAIN = 7168
L1_Z, L1_XBC, L1_MQ, L1_MG = 0, 1536, 4096, 4608
L1_MAIN = 5120


def _nt(a, b):
    return lax.dot_general(a, b, (((1,), (1,)), ((), ())), preferred_element_type=F32)


def _tn(a, b):
    return lax.dot_general(a, b, (((0,), (0,)), ((), ())), preferred_element_type=F32)


def _nn(a, b):
    return jnp.dot(a, b, preferred_element_type=F32)


def _cumsum_rows(tril, x):
    hi = x.astype(BF16)
    rest = x - hi.astype(F32)
    mid = rest.astype(BF16)
    lo = (rest - mid.astype(F32)).astype(BF16)
    return _nn(tril, hi) + _nn(tril, mid) + _nn(tril, lo)


def _silu(x):
    return 0.5 * x * (1.0 + jnp.tanh(0.5 * x))


def _cparams(sem):
    return pltpu.CompilerParams(dimension_semantics=sem, vmem_limit_bytes=VMEM_LIMIT)


def _rope_table_kernel(pos_ref, inv_ref, sign_ref, x_ref, cos_ref, sin_ref, xb_ref):
    xb_ref[...] = x_ref[...].astype(BF16)
    rows = pos_ref.shape[0]
    half = HEAD_DIM // 2
    ang = pos_ref[...].astype(F32) * inv_ref[...]
    cos = jnp.cos(ang)
    sin = jnp.sin(ang)
    lo = lax.broadcasted_iota(jnp.int32, (rows, HEAD_DIM), 1) < half
    sign = sign_ref[...]
    for table, ref, scale in ((cos, cos_ref, None), (sin, sin_ref, sign)):
        swapped = pltpu.roll(table, half, 1)
        first = jnp.where(lo, table, swapped)
        second = jnp.where(lo, swapped, table)
        if scale is not None:
            first, second = first * scale, second * scale
        ref[0:rows, :] = first
        ref[rows:2 * rows, :] = second


def _rope_tables(positions, x):
    n, dm = x.shape
    half = HEAD_DIM // 2
    inv = 1.0 / (ROPE_THETA ** (jnp.arange(0, HEAD_DIM, 2, dtype=F32) / HEAD_DIM))
    inv2 = jnp.concatenate([inv, inv]).reshape(1, HEAD_DIM)
    sign = jnp.concatenate([-jnp.ones((half,), F32), jnp.ones((half,), F32)]).reshape(1, HEAD_DIM)
    tb = 2048
    pos_pair = positions.reshape(n // tb, 2, tb // 2).transpose(0, 2, 1)
    pos_pair = jnp.repeat(pos_pair, half, axis=2).reshape(n // 2, HEAD_DIM)
    row = pl.BlockSpec((tb, HEAD_DIM), lambda i: (i, 0))
    vec = pl.BlockSpec((1, HEAD_DIM), lambda i: (0, 0))
    return pl.pallas_call(
        _rope_table_kernel,
        grid=(n // tb,),
        in_specs=[pl.BlockSpec((tb // 2, HEAD_DIM), lambda i: (i, 0)), vec, vec,
                  pl.BlockSpec((tb, dm), lambda i: (i, 0))],
        out_specs=[row, row, pl.BlockSpec((tb, dm), lambda i: (i, 0))],
        out_shape=[jax.ShapeDtypeStruct((n, HEAD_DIM), F32)] * 2 + [jax.ShapeDtypeStruct((n, dm), BF16)],
        compiler_params=_cparams(("arbitrary",)),
        name="rope_tables",
    )(pos_pair, inv2, sign, x)


SUB_COLS = 256


def _proj_kernel(*refs, groups, cast_x, has_rope):
    it = iter(refs)
    x_ref, wa_ref, wb_ref, ws_ref = next(it), next(it), next(it), next(it)
    if has_rope:
        c_ref, s_ref, sc_ref = next(it), next(it), next(it)
    o_ref, os_ref = next(it), next(it)
    xb_ref = next(it) if cast_x else x_ref
    j = pl.program_id(1)

    @pl.when(j == 0)
    def _():
        if cast_x:
            xb_ref[...] = x_ref[...].astype(BF16)
        os_ref[...] = _nn(xb_ref[...], ws_ref[...])

    def rope_block(acc, cols):
        cos, sin = c_ref[...], s_ref[...]
        for k in range(SUB_COLS // HEAD_DIM):
            sl = slice(cols.start + k * HEAD_DIM, cols.start + (k + 1) * HEAD_DIM)
            a = acc[:, k * HEAD_DIM:(k + 1) * HEAD_DIM] * sc_ref[:, sl]
            o_ref[:, sl] = (a * cos + pltpu.roll(a, HEAD_DIM // 2, 1) * sin).astype(o_ref.dtype)

    def emit(w_ref, kinds):
        if kinds is None:
            o_ref[...] = _nn(xb_ref[...], w_ref[...]).astype(o_ref.dtype)
            return
        for sb, kind in enumerate(kinds):
            cols = slice(sb * SUB_COLS, (sb + 1) * SUB_COLS)
            acc = _nn(xb_ref[...], w_ref[:, cols])
            if kind == "rope":
                rope_block(acc, cols)
            else:
                o_ref[:, cols] = acc.astype(o_ref.dtype)

    for lo, hi, src, kinds in groups:
        pl.when(jnp.logical_and(j >= lo, j < hi))(functools.partial(emit, wb_ref if src else wa_ref, kinds))


def _project(x, w_a, a_cols, w_b, w_small, rope=None, tm=2048, tn=1024):
    n, kdim = x.shape
    m = a_cols + w_b.shape[1]
    split = a_cols // tn
    n_tiles = m // tn
    cast_x = x.dtype != BF16
    per_tile = tn // SUB_COLS
    kinds = [["plain"] * per_tile for _ in range(n_tiles)]
    in_specs = [
        pl.BlockSpec((tm, kdim), lambda i, j: (i, 0)),
        pl.BlockSpec((kdim, tn), lambda i, j: (0, jnp.minimum(j, split - 1))),
        pl.BlockSpec((kdim, tn), lambda i, j: (0, jnp.maximum(j - split, 0))),
        pl.BlockSpec((kdim, LANES), lambda i, j: (0, 0)),
    ]
    args = [x, w_a, w_b, w_small]
    scratch = [pltpu.VMEM((tm, kdim), BF16)] if cast_x else []
    if rope is not None:
        cos, sin, col_scale, first_col, last_col = rope
        for c in range(first_col // SUB_COLS, last_col // SUB_COLS):
            kinds[c // per_tile][c % per_tile] = "rope"
        in_specs += [
            pl.BlockSpec((tm, HEAD_DIM), lambda i, j: (i, 0)),
            pl.BlockSpec((tm, HEAD_DIM), lambda i, j: (i, 0)),
            pl.BlockSpec((1, tn), lambda i, j: (0, j)),
        ]
        args += [cos, sin, col_scale]
    groups = []
    for t in range(n_tiles):
        plan = None if all(k == "plain" for k in kinds[t]) else tuple(kinds[t])
        src = int(t >= split)
        if groups and groups[-1][2] == src and groups[-1][3] == plan:
            groups[-1] = (groups[-1][0], t + 1, src, plan)
        else:
            groups.append((t, t + 1, src, plan))
    return pl.pallas_call(
        functools.partial(_proj_kernel, groups=tuple(groups), cast_x=cast_x, has_rope=rope is not None),
        grid=(n // tm, n_tiles),
        in_specs=in_specs,
        out_specs=[
            pl.BlockSpec((tm, tn), lambda i, j: (i, j)),
            pl.BlockSpec((tm, LANES), lambda i, j: (i, 0)),
        ],
        out_shape=[jax.ShapeDtypeStruct((n, m), BF16), jax.ShapeDtypeStruct((n, LANES), F32)],
        scratch_shapes=scratch,
        compiler_params=_cparams(("arbitrary", "arbitrary")),
        name="in_proj",
    )(*args)


GLA_BATCH_GROUP = 2


def _gla_kernel(q_ref, k_ref, v_ref, g_ref, ga_ref, wg_ref, bg_ref, nw_ref, o_ref, st_ref, la_ref, *, n_chunks):
    c_ = GLA_CHUNK
    group = q_ref.shape[0]

    @pl.when(pl.program_id(1) == 0)
    def _():
        st_ref[...] = jnp.zeros_like(st_ref)

    p_ = 2 * c_
    row = lax.broadcasted_iota(jnp.int32, (p_, p_), 0)
    col = lax.broadcasted_iota(jnp.int32, (p_, p_), 1)
    same_chunk = (row < c_) == (col < c_)
    tril = jnp.where(jnp.logical_and(same_chunk, col <= row), 1.0, 0.0).astype(BF16)
    row_a = lax.broadcasted_iota(jnp.int32, (c_, c_), 0)
    col_a = lax.broadcasted_iota(jnp.int32, (c_, c_), 1)
    causal_a = col_a <= row_a
    row_b = lax.broadcasted_iota(jnp.int32, (c_, p_), 0)
    col_b = lax.broadcasted_iota(jnp.int32, (c_, p_), 1)
    causal_b = col_b - c_ <= row_b
    in_a = lax.broadcasted_iota(jnp.int32, (p_, 1), 0) < c_
    nw = nw_ref[...]
    for b in range(group):
        z = _nn(ga_ref[b].astype(BF16), wg_ref[...]) + bg_ref[...]
        la_ref[b] = (jnp.minimum(z, 0.0) - jnp.log1p(jnp.exp(-jnp.abs(z)))) * (1.0 / GLA_GATE_NORMALIZER)

    def pair(pi, carry):
        r0 = pl.multiple_of(pi * p_, p_)
        rows = pl.ds(r0, p_)
        gates = []
        for b in range(group):
            bcum = _cumsum_rows(tril, la_ref[b, rows, :])
            blast_a = bcum[c_ - 1:c_, :]
            blast_b = bcum[p_ - 1:p_, :]
            blast = jnp.where(in_a, blast_a, blast_b)
            gates.append((jnp.exp(bcum), jnp.exp(-bcum), jnp.exp(blast - bcum), jnp.exp(blast_a), jnp.exp(blast_b)))
        for h in range(GLA_HEADS):
            for b in range(group):
                e_pos, e_neg, e_tail, d_a, d_b = gates[b]
                ks = slice(h * GLA_DK, (h + 1) * GLA_DK)
                vs = slice(h * GLA_DV, (h + 1) * GLA_DV)
                q = q_ref[b, rows, ks].astype(F32) * (GLA_DK ** -0.5)
                k = k_ref[b, rows, ks].astype(F32)
                q_dec = q * e_pos[:, ks]
                k_tail = k * e_tail[:, ks]
                q_dec_bf = q_dec.astype(BF16)
                k_inv_bf = (k * e_neg[:, ks]).astype(BF16)
                k_tail_bf = k_tail.astype(BF16)
                v = v_ref[b, rows, vs]
                s_a = jnp.where(causal_a, _nt(q_dec_bf[0:c_], k_inv_bf[0:c_]), 0.0)
                keys_b = jnp.concatenate([k_tail_bf[0:c_], k_inv_bf[c_:p_]], axis=0)
                s_b = jnp.where(causal_b, _nt(q_dec_bf[c_:p_], keys_b), 0.0)
                o_intra = jnp.concatenate([_nn(s_a.astype(BF16), v[0:c_]), _nn(s_b.astype(BF16), v)], axis=0)
                q_state = jnp.concatenate([q_dec_bf[0:c_], (q_dec[c_:p_] * d_a[:, ks]).astype(BF16)], axis=0)
                st = st_ref[b, h]
                o = o_intra + _nt(q_state, st.astype(BF16))
                k_end = jnp.concatenate([(k_tail[0:c_] * d_b[:, ks]).astype(BF16), k_tail_bf[c_:p_]], axis=0)
                st_ref[b, h] = st * (d_a[:, ks] * d_b[:, ks]) + _tn(v, k_end)
                ms = jnp.mean(o * o, axis=-1, keepdims=True)
                o = o * lax.rsqrt(ms + RMS_EPS) * nw
                o_ref[b, rows, vs] = (o * _silu(g_ref[b, rows, vs].astype(F32))).astype(o_ref.dtype)
        return carry

    lax.fori_loop(0, n_chunks // 2, pair, 0)


def _gla(h_main, h_small, w_gate, b_gate, norm_w, batch, seq, tb=512):
    n = batch * seq
    nt = seq // tb
    group = GLA_BATCH_GROUP if batch % GLA_BATCH_GROUP == 0 else 1
    wg = jnp.zeros((LANES, GLA_HEADS * GLA_DK), BF16).at[:GLA_GATE_RANK].set(w_gate.astype(BF16))
    h4 = h_main.reshape(batch // group, group, seq, h_main.shape[1])
    s4 = h_small.reshape(batch // group, group, seq, LANES)
    idx = lambda c: (lambda b, t: (b, 0, t, c))
    const = lambda b, t: (0, 0)
    width = GLA_HEADS * GLA_DV
    out = pl.pallas_call(
        functools.partial(_gla_kernel, n_chunks=tb // GLA_CHUNK),
        grid=(batch // group, nt),
        in_specs=[
            pl.BlockSpec((None, group, tb, 512), idx(L0_GQ // 512)),
            pl.BlockSpec((None, group, tb, 512), idx(L0_GK // 512)),
            pl.BlockSpec((None, group, tb, 1024), idx(L0_GV // 1024)),
            pl.BlockSpec((None, group, tb, 1024), idx(L0_GG // 1024)),
            pl.BlockSpec((None, group, tb, LANES), idx(0)),
            pl.BlockSpec((LANES, 512), const),
            pl.BlockSpec((1, 512), const),
            pl.BlockSpec((1, GLA_DV), const),
        ],
        out_specs=pl.BlockSpec((None, group, tb, width), idx(0)),
        out_shape=jax.ShapeDtypeStruct((batch // group, group, seq, width), BF16),
        scratch_shapes=[pltpu.VMEM((group, GLA_HEADS, GLA_DV, GLA_DK), F32),
                        pltpu.VMEM((group, tb, GLA_HEADS * GLA_DK), F32)],
        compiler_params=_cparams(("arbitrary", "arbitrary")),
        name="gla",
    )(h4, h4, h4, h4, s4, wg, b_gate.reshape(1, -1), norm_w.reshape(1, -1))
    return out.reshape(n, width)


DIL_WINDOW = 2048
SLAB_GROUP = 16
SLAB_PITCH = 24
SEG = 32
TILE_UNROLL = 8


def _dil_kernel(q0_ref, q1_ref, q2_ref, kc_ref, kp_ref, vc_ref, vp_ref, g_ref, out_ref,
                kcat, vcat, ks, vs, q1s, q2s, o0s, l0s, o1s, l1s, o2s, l2s, bias):
    blk = DIL_BLOCK
    w = DIL_WINDOW
    groups = w // SLAB_GROUP
    t = pl.program_id(1)
    first_window = t == 0

    @pl.when(jnp.logical_and(jnp.logical_and(pl.program_id(0) == 0, t == 0), pl.program_id(2) == 0))
    def _():
        rho = lax.broadcasted_iota(jnp.int32, (blk, 2 * blk), 0)
        kap = lax.broadcasted_iota(jnp.int32, (blk, 2 * blk), 1)
        in_prev = kap < blk
        kap_l = jnp.where(in_prev, kap, kap - blk)
        perm = lambda x: 4 * (x % SEG) + x // SEG
        neg = jnp.float32(-jnp.inf)
        for slot, (jq, jk) in enumerate(((rho, kap_l), (perm(rho), perm(kap_l)))):
            ok_prev = jnp.where(jk >= jq, 0.0, neg)
            ok_cur = jnp.where(jk <= jq, 0.0, neg)
            bias[2 * slot] = jnp.where(in_prev, ok_prev, ok_cur)
            bias[2 * slot + 1] = jnp.where(in_prev, neg, ok_cur)

    kcat[0:blk, :] = kp_ref[w - blk:w, :]
    kcat[blk:blk + w, :] = kc_ref[...]
    vcat[0:blk, :] = vp_ref[w - blk:w, :]
    vcat[blk:blk + w, :] = vc_ref[...]

    def to_slabs(i, carry):
        src = pl.ds(pl.multiple_of(i * SLAB_GROUP, SLAB_GROUP), SLAB_GROUP)
        prev = pl.ds(pl.multiple_of(i * SLAB_PITCH, 8), SLAB_GROUP)
        cur = pl.ds(pl.multiple_of((groups + i) * SLAB_PITCH, 8), SLAB_GROUP)
        ks[prev, :] = kp_ref[src, :].astype(F32)
        ks[cur, :] = kc_ref[src, :].astype(F32)
        vs[prev, :] = vp_ref[src, :].astype(F32)
        vs[cur, :] = vc_ref[src, :].astype(F32)
        q1s[prev, :] = q1_ref[src, :].astype(F32)
        q2s[prev, :] = q2_ref[src, :].astype(F32)
        return carry

    lax.fori_loop(0, groups, to_slabs, 0, unroll=4)

    def attend(q, k, v, b):
        s = _nt(q, k) + b
        m = jnp.max(s, -1, keepdims=True)
        p = jnp.exp2(s - m)
        den = jnp.sum(p, -1, keepdims=True)
        o = _nn(p.astype(BF16), v) * (1.0 / den)
        return o, jnp.broadcast_to(m + jnp.log2(den), (blk, LANES))

    def group0(i, carry):
        r0 = pl.multiple_of(i * blk, blk)
        sel = jnp.where(jnp.logical_and(first_window, i == 0), 1, 0)
        o, l = attend(q0_ref[pl.ds(r0, blk), :], kcat[pl.ds(r0, 2 * blk), :], vcat[pl.ds(r0, 2 * blk), :],
                      bias[sel])
        o0s[pl.ds(r0, blk), :] = o
        l0s[pl.ds(r0, blk), :] = l
        return carry


    def seg(ref, group, b):
        return ref[pl.ds(group * SLAB_PITCH + b, SEG, stride=SLAB_PITCH), :]

    def group1(idx, carry):
        r = idx // 4
        n = idx % 4
        g_cur = groups + SEG * n
        q = jnp.concatenate([seg(q1s, SEG * n, r + 4 * j) for j in range(4)], axis=0).astype(BF16)
        k = jnp.concatenate([seg(ks, g_cur - SEG, r + 4 * j) for j in range(4)]
                            + [seg(ks, g_cur, r + 4 * j) for j in range(4)], axis=0).astype(BF16)
        v = jnp.concatenate([seg(vs, g_cur - SEG, r + 4 * j) for j in range(4)]
                            + [seg(vs, g_cur, r + 4 * j) for j in range(4)], axis=0).astype(BF16)
        sel = jnp.where(jnp.logical_and(first_window, n == 0), 3, 2)
        o, l = attend(q, k, v, bias[sel])
        for j in range(4):
            dst = pl.ds(SEG * n * SLAB_PITCH + r + 4 * j, SEG, stride=SLAB_PITCH)
            o1s[dst, :] = o[j * SEG:(j + 1) * SEG]
            l1s[dst, :] = l[j * SEG:(j + 1) * SEG]
        return carry


    def group2(r, carry):
        prev = pl.ds(r, blk, stride=SLAB_PITCH)
        cur = pl.ds(groups * SLAB_PITCH + r, blk, stride=SLAB_PITCH)
        q = q2s[prev, :].astype(BF16)
        k = jnp.concatenate([ks[prev, :], ks[cur, :]], axis=0).astype(BF16)
        v = jnp.concatenate([vs[prev, :], vs[cur, :]], axis=0).astype(BF16)
        o, l = attend(q, k, v, bias[jnp.where(first_window, 1, 0)])
        o2s[prev, :] = o
        l2s[prev, :] = l
        return carry

    def tiles(i, carry):
        return group2(i, group1(i, group0(i, carry)))

    lax.fori_loop(0, w // blk, tiles, 0, unroll=TILE_UNROLL)

    def merge(i, carry):
        r0 = pl.multiple_of(i * blk, blk)
        rows = pl.ds(r0, blk)
        pieces = [pl.ds(pl.multiple_of((i * (blk // SLAB_GROUP) + j) * SLAB_PITCH, 8), SLAB_GROUP)
                  for j in range(blk // SLAB_GROUP)]
        gather = lambda ref: jnp.concatenate([ref[p, :] for p in pieces], axis=0)
        l0, l1, l2 = l0s[rows, :], gather(l1s), gather(l2s)
        m = jnp.maximum(jnp.maximum(l0, l1), l2)
        w0, w1, w2 = jnp.exp2(l0 - m), jnp.exp2(l1 - m), jnp.exp2(l2 - m)
        o = (w0 * o0s[rows, :] + w1 * gather(o1s) + w2 * gather(o2s)) * (1.0 / (w0 + w1 + w2))
        out_ref[rows, :] = (o * _silu(g_ref[rows, :].astype(F32))).astype(out_ref.dtype)
        return carry

    lax.fori_loop(0, w // blk, merge, 0)


def _dilated_attention(h_main, batch, seq):
    n = batch * seq
    w = DIL_WINDOW
    nt = seq // w
    slab_rows = (w // SLAB_GROUP) * SLAB_PITCH

    def cur(col):
        return pl.BlockSpec((w, HEAD_DIM), lambda b, t, h: (b * nt + t, col // HEAD_DIM + h))

    def prev(col):
        return pl.BlockSpec((w, HEAD_DIM), lambda b, t, h: (b * nt + jnp.maximum(t - 1, 0), col // HEAD_DIM + h))

    width = DIL_HEADS * HEAD_DIM
    slab = lambda rows: pltpu.VMEM((rows, LANES), F32)
    return pl.pallas_call(
        _dil_kernel,
        grid=(batch, nt, DIL_HEADS),
        in_specs=[cur(L0_DQ), cur(L0_DQ + width), cur(L0_DQ + 2 * width),
                  cur(L0_DK), prev(L0_DK), cur(L0_DV), prev(L0_DV), cur(L0_DG)],
        out_specs=pl.BlockSpec((w, HEAD_DIM), lambda b, t, h: (b * nt + t, h)),
        out_shape=jax.ShapeDtypeStruct((n, width), BF16),
        scratch_shapes=[
            pltpu.VMEM((w + DIL_BLOCK, HEAD_DIM), BF16), pltpu.VMEM((w + DIL_BLOCK, HEAD_DIM), BF16),
            slab(2 * slab_rows), slab(2 * slab_rows), slab(slab_rows), slab(slab_rows),
            slab(w), slab(w), slab(slab_rows), slab(slab_rows), slab(slab_rows), slab(slab_rows),
            pltpu.VMEM((4, DIL_BLOCK, 2 * DIL_BLOCK), F32),
        ],
        compiler_params=_cparams(("arbitrary", "arbitrary", "arbitrary")),
        name="dilated_attn",
    )(*([h_main] * 8))


def _mem_kv_kernel(m_ref, w_ref, o_ref):
    o_ref[...] = _nn(m_ref[...].astype(BF16), w_ref[...]).astype(o_ref.dtype)


def _mem_kv(mem2d, wk, wv):
    rows, kdim = mem2d.shape
    w = jnp.concatenate([wk, wv], axis=1).astype(BF16)
    tm = MEM_LEN
    return pl.pallas_call(
        _mem_kv_kernel,
        grid=(rows // tm,),
        in_specs=[pl.BlockSpec((tm, kdim), lambda i: (i, 0)), pl.BlockSpec(w.shape, lambda i: (0, 0))],
        out_specs=pl.BlockSpec((tm, w.shape[1]), lambda i: (i, 0)),
        out_shape=jax.ShapeDtypeStruct((rows, w.shape[1]), BF16),
        compiler_params=_cparams(("arbitrary",)),
        name="mem_kv",
    )(mem2d, w)


def _mem_attn_kernel(q_ref, g_ref, k_ref, v_ref, o_ref, *, tq):
    for h in range(MEM_HEADS):
        hs = slice(h * HEAD_DIM, (h + 1) * HEAD_DIM)
        s = _nt(q_ref[:, hs], k_ref[:, hs]) * (HEAD_DIM ** -0.5 * LOG2_E)
        m = jnp.max(s, -1, keepdims=True)
        p = jnp.exp2(s - m)
        den = jnp.sum(p, -1, keepdims=True)
        o = _nn(p.astype(BF16), v_ref[:, hs]) * (1.0 / den)
        o_ref[:, hs] = (o * _silu(g_ref[:, hs].astype(F32))).astype(o_ref.dtype)


def _mem_attn(h_main, kv, batch, seq, q_col, g_col, tq=1024):
    n = batch * seq
    width = MEM_HEADS * HEAD_DIM
    nt = seq // tq
    return pl.pallas_call(
        functools.partial(_mem_attn_kernel, tq=tq),
        grid=(batch, nt),
        in_specs=[
            pl.BlockSpec((tq, width), lambda b, t: (b * nt + t, q_col // width)),
            pl.BlockSpec((tq, width), lambda b, t: (b * nt + t, g_col // width)),
            pl.BlockSpec((MEM_LEN, width), lambda b, t: (b, 0)),
            pl.BlockSpec((MEM_LEN, width), lambda b, t: (b, 1)),
        ],
        out_specs=pl.BlockSpec((tq, width), lambda b, t: (b * nt + t, 0)),
        out_shape=jax.ShapeDtypeStruct((n, width), BF16),
        compiler_params=_cparams(("arbitrary", "arbitrary")),
        name="mem_attn",
    )(h_main, h_main, kv, kv)


OUT_SUB_ROWS = 256


def _out_kernel(*refs, n_parts, emit_bf16):
    a_refs = refs[:n_parts]
    w_refs = refs[n_parts:2 * n_parts]
    x_ref, lw_ref, lb_ref, o_ref = refs[2 * n_parts:2 * n_parts + 4]
    for r0 in range(0, x_ref.shape[0], OUT_SUB_ROWS):
        rows = slice(r0, r0 + OUT_SUB_ROWS)
        acc = DEEPNORM_ALPHA * x_ref[rows, :]
        for a_ref, w_ref in zip(a_refs, w_refs):
            acc = acc + _nn(a_ref[rows, :], w_ref[...])
        mu = jnp.mean(acc, -1, keepdims=True)
        d = acc - mu
        var = jnp.mean(d * d, -1, keepdims=True)
        y = d * lax.rsqrt(var + LN_EPS) * lw_ref[...] + lb_ref[...]
        o_ref[rows, :] = y
        if emit_bf16:
            refs[2 * n_parts + 4][rows, :] = y.astype(BF16)


def _out_proj_ln(parts, w_out, x, ln_w, ln_b, emit_bf16=False, tm=1024):
    n, dm = x.shape
    w_bf = w_out.astype(BF16)
    w_specs, off = [], 0
    for p in parts:
        width = p.shape[1]
        w_specs.append(pl.BlockSpec((width, dm), functools.partial(lambda i, blk: (blk, 0), blk=off // width)))
        assert off % width == 0
        off += width
    row = lambda width: pl.BlockSpec((tm, width), lambda i: (i, 0))
    const = lambda shape: pl.BlockSpec(shape, lambda i: (0, 0))
    out_specs = [row(dm)] + ([row(dm)] if emit_bf16 else [])
    out_shape = [jax.ShapeDtypeStruct((n, dm), F32)] + ([jax.ShapeDtypeStruct((n, dm), BF16)] if emit_bf16 else [])
    return pl.pallas_call(
        functools.partial(_out_kernel, n_parts=len(parts), emit_bf16=emit_bf16),
        grid=(n // tm,),
        in_specs=[row(p.shape[1]) for p in parts] + w_specs + [row(dm), const((1, dm)), const((1, dm))],
        out_specs=out_specs,
        out_shape=out_shape,
        compiler_params=_cparams(("arbitrary",)),
        name="out_proj_ln",
    )(*parts, *([w_bf] * len(parts)), x, ln_w.reshape(1, dm), ln_b.reshape(1, dm))


def _ssd_kernel(z_ref, xs_ref, bm_ref, cm_ref, dt_ref, cw_ref, cb_ref, dtb_ref, a_ref, dsk_ref, nw_ref,
                o_ref, u_ref, pair_ref, lag_ref, act_ref, st_ref, *, n_chunks):
    q_ = SSD_CHUNK
    halo = 8

    @pl.when(pl.program_id(1) == 0)
    def _():
        st_ref[...] = jnp.zeros_like(st_ref)
        u_ref[0:halo, :] = jnp.zeros((halo, SSD_CONV_DIM), F32)
        pair_ref[0:halo, :] = jnp.zeros((halo, SSD_CONV_DIM), F32)

    row = lax.broadcasted_iota(jnp.int32, (q_, q_), 0)
    col = lax.broadcasted_iota(jnp.int32, (q_, q_), 1)
    causal = col <= row
    tril = jnp.where(causal, 1.0, 0.0).astype(BF16)
    lane = lax.broadcasted_iota(jnp.int32, (q_, LANES), 1)
    lo_half = lane < SSD_HEAD_DIM
    lane_row = lax.broadcasted_iota(jnp.int32, (1, LANES), 1)
    lo_half_row = lane_row < SSD_HEAD_DIM
    lo_half_bf = jnp.where(lo_half, 1.0, 0.0).astype(BF16)
    a_log2 = -jnp.exp(a_ref[...]) * LOG2_E

    def row_bcast(t, r):
        return jnp.tile(jnp.broadcast_to(t[r:r + 1, :], (8, q_)), (q_ // 8, 1))

    def chunk(ci, carry):
        r0 = pl.multiple_of(ci * q_, q_)
        rows = pl.ds(r0, q_)
        bc_w = SSD_GROUPS * SSD_STATE
        u_ref[halo:halo + q_, 0:SSD_INNER] = xs_ref[rows, :].astype(F32)
        u_ref[halo:halo + q_, SSD_INNER:SSD_INNER + bc_w] = bm_ref[rows, :].astype(F32)
        u_ref[halo:halo + q_, SSD_INNER + bc_w:SSD_CONV_DIM] = cm_ref[rows, :].astype(F32)
        for c0 in range(0, SSD_CONV_DIM, CONV_COLS):
            cs = slice(c0, c0 + CONV_COLS)
            u_cur = u_ref[halo:halo + q_, cs]
            lag_ref[...] = u_ref[halo - 1:halo - 1 + q_, cs]
            u_lag = lag_ref[...]
            pair_ref[halo:halo + q_, cs] = cw_ref[1:2, cs] * u_cur + cw_ref[0:1, cs] * u_lag
            conv = ((cb_ref[:, cs] + cw_ref[3:4, cs] * u_cur) + cw_ref[2:3, cs] * u_lag
                    + pair_ref[halo - 2:halo - 2 + q_, cs])
            act_ref[:, cs] = _silu(conv)
        u_ref[0:halo, :] = u_ref[q_:q_ + halo, :]
        pair_ref[0:halo, :] = pair_ref[q_:q_ + halo, :]
        dt_in = dt_ref[rows, :] + dtb_ref[...]
        dt = jnp.maximum(dt_in, 0.0) + jnp.log1p(jnp.exp(-jnp.abs(dt_in)))
        acum = _cumsum_rows(tril, dt * a_log2)
        acum_t = acum.T
        dt_t = dt.T
        ldt_t = jnp.log2(dt_t)
        rowp_t = acum_t - ldt_t
        w_t = dt_t * jnp.exp2(acum_t[:, q_ - 1:q_] - acum_t)
        c_decay = jnp.exp2(acum[q_ - 1:q_, :])
        for g in range(SSD_GROUPS):
            b_f = act_ref[:, SSD_INNER + g * SSD_STATE:SSD_INNER + (g + 1) * SSD_STATE]
            c_f = act_ref[:, SSD_INNER + (SSD_GROUPS + g) * SSD_STATE:SSD_INNER + (SSD_GROUPS + g + 1) * SSD_STATE]
            c_bf = c_f.astype(BF16)
            cbm = jnp.where(causal, _nt(c_bf, b_f.astype(BF16)), 0.0)
            b_t = b_f.T
            st = st_ref[g]
            y_off_all = _nn(c_bf, st.astype(BF16))
            ys, news, decs = [], [], []
            ss = jnp.zeros((q_, 1), F32)
            for j in range(SSD_GROUP_WIDTH // LANES):
                h1 = (g * SSD_GROUP_WIDTH + j * LANES) // SSD_HEAD_DIM
                ps = slice(g * SSD_GROUP_WIDTH + j * LANES, g * SSD_GROUP_WIDTH + (j + 1) * LANES)
                xs = act_ref[:, ps]
                intra, inter, e_l = [], [], []
                for hh in (h1, h1 + 1):
                    colb = jnp.broadcast_to(acum[:, hh:hh + 1], (q_, q_))
                    decay = jnp.exp2(jnp.minimum(colb - row_bcast(rowp_t, hh), row_bcast(ldt_t, hh)))
                    intra.append((cbm * decay).astype(BF16))
                    inter.append((b_t * row_bcast(w_t, hh)).astype(BF16))
                    e_l.append(jnp.exp2(colb))
                lhs = jnp.concatenate([jnp.concatenate(intra, axis=1), jnp.concatenate(inter, axis=1)], axis=0)
                xs_bf = xs.astype(BF16)
                xs_lo = xs_bf * lo_half_bf
                rhs = jnp.concatenate([xs_lo, xs_bf - xs_lo], axis=0)
                prod = _nn(lhs, rhs)
                y = prod[0:q_] + y_off_all[:, j * LANES:(j + 1) * LANES] * jnp.where(lo_half, e_l[0], e_l[1])
                y = y + dsk_ref[:, ps] * xs
                y = y * _silu(z_ref[rows, ps].astype(F32))
                ss = ss + jnp.sum(y * y, -1, keepdims=True)
                ys.append(y)
                news.append(prod[q_:2 * q_])
                decs.append(jnp.where(lo_half_row, c_decay[:, h1:h1 + 1], c_decay[:, h1 + 1:h1 + 2]))
            st_ref[g] = st * jnp.concatenate(decs, axis=1) + jnp.concatenate(news, axis=1)
            inv = lax.rsqrt(ss * (1.0 / SSD_GROUP_WIDTH) + RMS_EPS)
            for j, y in enumerate(ys):
                ps = slice(g * SSD_GROUP_WIDTH + j * LANES, g * SSD_GROUP_WIDTH + (j + 1) * LANES)
                o_ref[rows, ps] = (y * inv * nw_ref[:, ps]).astype(o_ref.dtype)
        return carry

    lax.fori_loop(0, n_chunks, chunk, 0)


def _ssd(h_main, h_small, conv_w, conv_b, dt_bias, a_log, d_skip, norm_w, batch, seq, tb=1024):
    n = batch * seq
    nt = seq // tb
    bc_w = SSD_GROUPS * SSD_STATE
    pad = lambda v: jnp.zeros((1, LANES), F32).at[0, :SSD_HEADS].set(v.astype(F32))
    idx = lambda c: (lambda b, t: (b * nt + t, c))
    const = lambda b, t: (0, 0)
    return pl.pallas_call(
        functools.partial(_ssd_kernel, n_chunks=tb // SSD_CHUNK),
        grid=(batch, nt),
        in_specs=[
            pl.BlockSpec((tb, SSD_INNER), idx(L1_Z // SSD_INNER)),
            pl.BlockSpec((tb, SSD_INNER), idx(L1_XBC // SSD_INNER)),
            pl.BlockSpec((tb, bc_w), idx((L1_XBC + SSD_INNER) // bc_w)),
            pl.BlockSpec((tb, bc_w), idx((L1_XBC + SSD_INNER + bc_w) // bc_w)),
            pl.BlockSpec((tb, LANES), idx(0)),
            pl.BlockSpec((SSD_CONV, SSD_CONV_DIM), const),
            pl.BlockSpec((1, SSD_CONV_DIM), const),
            pl.BlockSpec((1, LANES), const),
            pl.BlockSpec((1, LANES), const),
            pl.BlockSpec((1, SSD_INNER), const),
            pl.BlockSpec((1, SSD_INNER), const),
        ],
        out_specs=pl.BlockSpec((tb, SSD_INNER), idx(0)),
        out_shape=jax.ShapeDtypeStruct((n, SSD_INNER), BF16),
        scratch_shapes=[
            pltpu.VMEM((SSD_CHUNK + 8, SSD_CONV_DIM), F32),
            pltpu.VMEM((SSD_CHUNK + 8, SSD_CONV_DIM), F32),
            pltpu.VMEM((SSD_CHUNK, CONV_COLS), F32),
            pltpu.VMEM((SSD_CHUNK, SSD_CONV_DIM), F32),
            pltpu.VMEM((SSD_GROUPS, SSD_STATE, SSD_GROUP_WIDTH), F32),
        ],
        compiler_params=_cparams(("arbitrary", "arbitrary")),
        name="ssd",
    )(h_main, h_main, h_main, h_main, h_small, conv_w, conv_b.reshape(1, -1),
      pad(dt_bias), pad(a_log), jnp.repeat(d_skip.astype(F32), SSD_HEAD_DIM).reshape(1, -1),
      norm_w.reshape(1, -1))


def _pad_cols(w, width=LANES):
    return jnp.zeros((w.shape[0], width), BF16).at[:, :w.shape[1]].set(w.astype(BF16))


def kernel(x, mem, positions, l0_w_in, l0_gla_w_gate, l0_gla_b_gate, l0_gla_norm_w, l0_mem_wk, l0_mem_wv, l0_w_out, l0_ln_w, l0_ln_b, l1_w_in, l1_conv_w, l1_conv_b, l1_dt_bias, l1_a_log, l1_d_skip, l1_ssd_norm_w, l1_mem_wk, l1_mem_wv, l1_w_out, l1_ln_w, l1_ln_b):
    batch, seq, dm = x.shape
    n = batch * seq
    x2 = x.reshape(n, dm)
    mem2 = mem.reshape(batch * MEM_LEN, dm)

    ga0 = L0_DQ + GLA_GATE_RANK
    w0 = l0_w_in.astype(BF16)
    w0_b = w0[:, ga0:]
    w0_small = _pad_cols(w0[:, L0_DQ:ga0])
    cos, sin, x2_bf = _rope_tables(positions, x2)
    col_scale = jnp.ones((1, L0_MAIN), F32).at[0, L0_DQ:L0_DK].set(HEAD_DIM ** -0.5 * LOG2_E)
    h0, h0s = _project(x2_bf, w0, L0_DQ, w0_b, w0_small, rope=(cos, sin, col_scale, L0_DQ, L0_DV))
    o_a = _gla(h0, h0s, l0_gla_w_gate, l0_gla_b_gate, l0_gla_norm_w, batch, seq)
    o_b = _dilated_attention(h0, batch, seq)
    kv0 = _mem_kv(mem2, l0_mem_wk, l0_mem_wv)
    o_m = _mem_attn(h0, kv0, batch, seq, L0_MQ, L0_MG)
    x1, x1_bf = _out_proj_ln([o_a, o_b, o_m], l0_w_out, x2, l0_ln_w, l0_ln_b, emit_bf16=True)

    dt0 = SSD_INNER + SSD_CONV_DIM
    w1 = l1_w_in.astype(BF16)
    w1_b = w1[:, dt0 + SSD_HEADS:]
    w1_small = _pad_cols(w1[:, dt0:dt0 + SSD_HEADS])
    h1, h1s = _project(x1_bf, w1, dt0, w1_b, w1_small)
    y = _ssd(h1, h1s, l1_conv_w, l1_conv_b, l1_dt_bias, l1_a_log, l1_d_skip, l1_ssd_norm_w, batch, seq)
    kv1 = _mem_kv(mem2, l1_mem_wk, l1_mem_wv)
    o_m1 = _mem_attn(h1, kv1, batch, seq, L1_MQ, L1_MG)
    (x2_out,) = _out_proj_ln([y, o_m1], l1_w_out, x1, l1_ln_w, l1_ln_b)
    return x2_out.reshape(batch, seq, dm)
```
